```python
import math
import jax, jax.numpy as jnp
from jax import lax
import numpy as np

D_MODEL = 1024
BATCH = 16
SEQ = 2048
DEPTH = 2
DEC_BATCH = 128
DEC_SEQ = 1
PAST_LEN = 16384
PAGE_SIZE = 128

N_A = DEPTH // 2
N_B = DEPTH - N_A
PLE_DIM = 256
CONV_W = 31
N_HEADS = 16
QK_NOPE = 64
QK_ROPE = 32
V_HEAD = 64
Q_LORA = 384
KV_LORA = 256
ROPE_THETA = 10000.0
SCALE = 1.0 / math.sqrt(QK_NOPE + QK_ROPE)
N_GROUPS = 4
EXPERTS_PER_GROUP = 8
N_EXPERTS = N_GROUPS * EXPERTS_PER_GROUP
TOP_K = 2
EXPERT_FF = 512
MOE_BLOCK = 128
Q_BLOCK = 128
EPS = 1e-6

kernel_name = 'yoco_conformer_mla_hmoe_step'


def rmsnorm(x, g):
    xf = x.astype(jnp.float32)
    y = xf * lax.rsqrt(jnp.mean(xf * xf, axis=-1, keepdims=True) + EPS)
    return (y * g.astype(jnp.float32)).astype(x.dtype)


def layernorm(x, g, b):
    xf = x.astype(jnp.float32)
    xc = xf - jnp.mean(xf, axis=-1, keepdims=True)
    var = jnp.mean(xc * xc, axis=-1, keepdims=True)
    return (xc * lax.rsqrt(var + EPS) * g.astype(jnp.float32) + b.astype(jnp.float32)).astype(x.dtype)


def rope(x, pos):
    half = x.shape[-1] // 2
    inv = ROPE_THETA ** (-jnp.arange(half, dtype=jnp.float32) / half)
    ang = pos.astype(jnp.float32)[:, None] * inv[None, :]
    ang = ang.reshape((ang.shape[0],) + (1,) * (x.ndim - 3) + (half,))
    cos, sin = jnp.cos(ang), jnp.sin(ang)
    xf = x.astype(jnp.float32)
    x1, x2 = xf[..., :half], xf[..., half:]
    return jnp.concatenate([x1 * cos - x2 * sin, x2 * cos + x1 * sin], axis=-1).astype(x.dtype)


def conv_module(u, ctx, w_in, b_in, w_dw, b_dw, ln_g, ln_b, w_out, b_out):
    a = u @ w_in + b_in
    z = a[..., :D_MODEL] * jax.nn.sigmoid(a[..., D_MODEL:])
    full = jnp.concatenate([ctx.astype(z.dtype), z], axis=1)
    y = lax.conv_general_dilated(full, w_dw[:, None, :].astype(z.dtype), window_strides=(1,), padding='VALID',
                                 dimension_numbers=('NWC', 'WIO', 'NWC'), feature_group_count=D_MODEL) + b_dw
    n = layernorm(y, ln_g, ln_b)
    out = jax.nn.silu(n) @ w_out + b_out
    return out, full[:, -(CONV_W - 1):]


def moe_experts(x, expert_id, gate, w1, w3, w2):
    T = x.shape[0]
    A = T * TOP_K
    flat_e = expert_id.reshape(A)
    flat_t = jnp.repeat(jnp.arange(T, dtype=jnp.int32), TOP_K)
    flat_w = gate.reshape(A)
    order = jnp.argsort(flat_e)
    se = flat_e[order]
    counts = jnp.bincount(flat_e, length=N_EXPERTS)
    starts = jnp.cumsum(counts) - counts
    padded = (counts + MOE_BLOCK - 1) // MOE_BLOCK * MOE_BLOCK
    pends = jnp.cumsum(padded)
    pstarts = pends - padded
    dest = pstarts[se] + (jnp.arange(A) - starts[se])
    n_blocks = (A + N_EXPERTS * (MOE_BLOCK - 1) + MOE_BLOCK - 1) // MOE_BLOCK
    R = n_blocks * MOE_BLOCK
    row_tok = jnp.full((R,), T, jnp.int32).at[dest].set(flat_t[order])
    row_w = jnp.zeros((R,), jnp.float32).at[dest].set(flat_w[order])
    blk_e = jnp.minimum(jnp.searchsorted(pends, jnp.arange(n_blocks) * MOE_BLOCK, side='right'), N_EXPERTS - 1)
    xp = jnp.concatenate([x, jnp.zeros((1, x.shape[1]), x.dtype)], axis=0)
    xb = xp[row_tok].reshape(n_blocks, MOE_BLOCK, x.shape[1])

    def expert_block(args):
        xblk, e = args
        hmid = jax.nn.silu(xblk @ w1[e]) * (xblk @ w3[e])
        return hmid @ w2[e]

    yb = lax.map(expert_block, (xb, blk_e)).reshape(R, x.shape[1])
    y = jnp.zeros((T + 1, x.shape[1]), x.dtype).at[row_tok].add(yb * row_w[:, None].astype(x.dtype))
    return y[:T]


def hier_moe(u, w_group, b_group, w_inner, b_inner, w1, w3, w2):
    shp = u.shape
    x = u.reshape(-1, D_MODEL)
    g_logits = (x @ w_group).astype(jnp.float32) + b_group.astype(jnp.float32)
    g_prob = jax.nn.softmax(g_logits, axis=-1)
    g_sel = jnp.argmax(g_logits, axis=-1).astype(jnp.int32)
    g_w = jnp.take_along_axis(g_prob, g_sel[:, None], axis=1)
    in_logits = jnp.einsum('td,dge->tge', x, w_inner).astype(jnp.float32) + b_inner.astype(jnp.float32)
    in_sel = jnp.take_along_axis(in_logits, g_sel[:, None, None], axis=1)[:, 0]
    vals, idx = lax.top_k(in_sel, TOP_K)
    gate = jax.nn.softmax(vals, axis=-1) * g_w
    expert_id = g_sel[:, None] * EXPERTS_PER_GROUP + idx.astype(jnp.int32)
    return moe_experts(x, expert_id, gate, w1, w3, w2).reshape(shp)


def ple_add(h, p_i, g, w_gate, b_gate, w_proj):
    gate = jax.nn.sigmoid(rmsnorm(h, g) @ w_gate + b_gate)
    return h + gate * (p_i @ w_proj)


def shared_kv(h, g_in, w_dkv, g_lat, w_kr, pos):
    u = rmsnorm(h, g_in)
    c = rmsnorm(u @ w_dkv, g_lat)
    k_pe = rope(u @ w_kr, pos)
    return c, k_pe


def mla_query(u, w_dq, g_q, w_uq, pos):
    b, s = u.shape[:2]
    cq = rmsnorm(u @ w_dq, g_q)
    q = (cq @ w_uq).reshape(b, s, N_HEADS, QK_NOPE + QK_ROPE)
    return q[..., :QK_NOPE], rope(q[..., QK_NOPE:], pos)


def mla_attend_prompt(q_nope, q_pe, c, k_pe, w_uk, w_uv):
    b, s = c.shape[:2]
    nb = s // Q_BLOCK
    k_nope = jnp.einsum('bsc,chn->bshn', c, w_uk)
    v = jnp.einsum('bsc,chv->bshv', c, w_uv)
    qn = jnp.moveaxis(q_nope.reshape(b, nb, Q_BLOCK, N_HEADS, QK_NOPE), 1, 0)
    qp = jnp.moveaxis(q_pe.reshape(b, nb, Q_BLOCK, N_HEADS, QK_ROPE), 1, 0)
    kpos = jnp.arange(s)

    def block(args):
        qn_b, qp_b, i = args
        sc = (jnp.einsum('bqhn,bkhn->bhqk', qn_b, k_nope) + jnp.einsum('bqhr,bkr->bhqk', qp_b, k_pe)).astype(jnp.float32) * SCALE
        qpos = i * Q_BLOCK + jnp.arange(Q_BLOCK)
        sc = jnp.where(kpos[None, :] <= qpos[:, None], sc, -jnp.inf)
        prob = jax.nn.softmax(sc, axis=-1).astype(v.dtype)
        return jnp.einsum('bhqk,bkhv->bqhv', prob, v)

    o = lax.map(block, (qn, qp, jnp.arange(nb)))
    return jnp.moveaxis(o, 0, 1).reshape(b, s, N_HEADS * V_HEAD)


def mla_attend_sample(q_nope, q_pe, c_new, kpe_new, cache_latent, cache_kpe, page_table, w_uk, w_uv):
    b, sd = q_nope.shape[:2]
    q_lat = jnp.einsum('bshn,chn->bshc', q_nope, w_uk)
    causal = jnp.tril(jnp.ones((sd, sd), dtype=bool))

    def one(args):
        ql, qp, cn, kn, pt = args
        cp = cache_latent[pt].reshape(-1, KV_LORA).astype(ql.dtype)
        kp = cache_kpe[pt].reshape(-1, QK_ROPE).astype(ql.dtype)
        s_past = (jnp.einsum('shc,tc->hst', ql, cp) + jnp.einsum('shr,tr->hst', qp, kp)).astype(jnp.float32)
        s_new = (jnp.einsum('shc,tc->hst', ql, cn) + jnp.einsum('shr,tr->hst', qp, kn)).astype(jnp.float32)
        s_new = jnp.where(causal[None], s_new, -jnp.inf)
        prob = jax.nn.softmax(jnp.concatenate([s_past, s_new], axis=-1) * SCALE, axis=-1).astype(cp.dtype)
        n_past = cp.shape[0]
        return jnp.einsum('hst,tc->shc', prob[..., :n_past], cp) + jnp.einsum('hst,tc->shc', prob[..., n_past:], cn)

    o_lat = lax.map(one, (q_lat, q_pe, c_new, kpe_new, page_table))
    o = jnp.einsum('bshc,chv->bshv', o_lat, w_uv)
    return o.reshape(b, sd, N_HEADS * V_HEAD)


def setup_inputs(seed: int = 0) -> dict:
    key = jax.random.key(seed)
    ks = jax.random.split(key, 64)
    ctr = [0]

    def nk():
        ctr[0] += 1
        return ks[ctr[0] - 1]

    def nrm(shape, scale):
        return jax.random.normal(nk(), shape, jnp.float32) * scale

    def gain(shape):
        return 1.0 + nrm(shape, 0.02)

    D = D_MODEL
    n_pages = PAST_LEN // PAGE_SIZE
    n_used = DEC_BATCH * n_pages
    n_pool = n_used + n_used // 4
    x_prompt = nrm((BATCH, SEQ, D), 1.0)
    x_sample = nrm((DEC_BATCH, DEC_SEQ, D), 1.0)
    cache_conv = nrm((N_A, DEC_BATCH, CONV_W - 1, D), 0.5)
    cache_latent = nrm((n_pool, PAGE_SIZE, KV_LORA), 1.0)
    cache_kpe = nrm((n_pool, PAGE_SIZE, QK_ROPE), 1.0)
    page_table = jax.random.permutation(nk(), n_pool)[:n_used].reshape(DEC_BATCH, n_pages).astype(jnp.int32)
    p_prompt = nrm((DEPTH, BATCH, SEQ, PLE_DIM), 1.0)
    p_sample = nrm((DEPTH, DEC_BATCH, DEC_SEQ, PLE_DIM), 1.0)
    return {
        'x_prompt': x_prompt, 'x_sample': x_sample,
        'cache_conv': cache_conv, 'cache_latent': cache_latent, 'cache_kpe': cache_kpe,
        'page_table': page_table, 'p_prompt': p_prompt, 'p_sample': p_sample,
        'g_mix': gain((DEPTH, D)), 'g_ffn': gain((DEPTH, D)), 'g_ple': gain((DEPTH, D)), 'g_final': gain((D,)),
        'conv_w_in': nrm((N_A, D, 2 * D), D ** -0.5), 'conv_b_in': nrm((N_A, 2 * D), 0.02),
        'conv_w_dw': nrm((N_A, CONV_W, D), CONV_W ** -0.5), 'conv_b_dw': nrm((N_A, D), 0.02),
        'conv_ln_g': gain((N_A, D)), 'conv_ln_b': nrm((N_A, D), 0.02),
        'conv_w_out': nrm((N_A, D, D), D ** -0.5), 'conv_b_out': nrm((N_A, D), 0.02),
        'g_kv_in': gain((D,)), 'w_dkv': nrm((D, KV_LORA), D ** -0.5), 'g_kv_lat': gain((KV_LORA,)),
        'w_kr': nrm((D, QK_ROPE), D ** -0.5),
        'w_uk': nrm((KV_LORA, N_HEADS, QK_NOPE), KV_LORA ** -0.5),
        'w_uv': nrm((KV_LORA, N_HEADS, V_HEAD), KV_LORA ** -0.5),
        'w_dq': nrm((N_B, D, Q_LORA), D ** -0.5), 'g_q_lat': gain((N_B, Q_LORA)),
        'w_uq': nrm((N_B, Q_LORA, N_HEADS * (QK_NOPE + QK_ROPE)), Q_LORA ** -0.5),
        'w_o': nrm((N_B, N_HEADS * V_HEAD, D), (N_HEADS * V_HEAD) ** -0.5),
        'w_group': nrm((DEPTH, D, N_GROUPS), D ** -0.5), 'b_group': nrm((DEPTH, N_GROUPS), 0.01),
        'w_inner': nrm((DEPTH, D, N_GROUPS, EXPERTS_PER_GROUP), D ** -0.5),
        'b_inner': nrm((DEPTH, N_GROUPS, EXPERTS_PER_GROUP), 0.01),
        'w_exp_gate': nrm((DEPTH, N_EXPERTS, D, EXPERT_FF), D ** -0.5),
        'w_exp_up': nrm((DEPTH, N_EXPERTS, D, EXPERT_FF), D ** -0.5),
        'w_exp_down': nrm((DEPTH, N_EXPERTS, EXPERT_FF, D), EXPERT_FF ** -0.5),
        'w_ple_gate': nrm((DEPTH, D, D), D ** -0.5), 'b_ple_gate': nrm((DEPTH, D), 0.02),
        'w_ple_proj': nrm((DEPTH, PLE_DIM, D), PLE_DIM ** -0.5),
    }


def reference(x_prompt, x_sample, cache_conv, cache_latent, cache_kpe, page_table, p_prompt, p_sample,
              g_mix, g_ffn, g_ple, g_final,
              conv_w_in, conv_b_in, conv_w_dw, conv_b_dw, conv_ln_g, conv_ln_b, conv_w_out, conv_b_out,
              g_kv_in, w_dkv, g_kv_lat, w_kr, w_uk, w_uv,
              w_dq, g_q_lat, w_uq, w_o,
              w_group, b_group, w_inner, b_inner, w_exp_gate, w_exp_up, w_exp_down,
              w_ple_gate, b_ple_gate, w_ple_proj):

    def run_group(x, p, conv_ctx, pos, attend):
        h = x
        conv_new = []
        kv = None
        for i in range(DEPTH):
            u = rmsnorm(h, g_mix[i])
            if i < N_A:
                out, st = conv_module(u, conv_ctx[i], conv_w_in[i], conv_b_in[i], conv_w_dw[i], conv_b_dw[i],
                                      conv_ln_g[i], conv_ln_b[i], conv_w_out[i], conv_b_out[i])
                conv_new.append(st)
            else:
                j = i - N_A
                q_nope, q_pe = mla_query(u, w_dq[j], g_q_lat[j], w_uq[j], pos)
                out = attend(q_nope, q_pe, kv[0], kv[1]) @ w_o[j]
            h = h + out
            h = h + hier_moe(rmsnorm(h, g_ffn[i]), w_group[i], b_group[i], w_inner[i], b_inner[i],
                             w_exp_gate[i], w_exp_up[i], w_exp_down[i])
            h = ple_add(h, p[i], g_ple[i], w_ple_gate[i], b_ple_gate[i], w_ple_proj[i])
            if i == N_A - 1:
                kv = shared_kv(h, g_kv_in, w_dkv, g_kv_lat, w_kr, pos)
        return rmsnorm(h, g_final), jnp.stack(conv_new), kv[0], kv[1]

    bp, sp = x_prompt.shape[:2]
    pos_p = jnp.arange(sp, dtype=jnp.int32)
    ctx_p = jnp.zeros((N_A, bp, CONV_W - 1, D_MODEL), x_prompt.dtype)
    attend_p = lambda qn, qp, c, k: mla_attend_prompt(qn, qp, c, k, w_uk, w_uv)
    y_prompt, conv_prompt, latent_prompt, kpe_prompt = run_group(x_prompt, p_prompt, ctx_p, pos_p, attend_p)

    sd = x_sample.shape[1]
    past_len = page_table.shape[1] * PAGE_SIZE
    pos_s = past_len + jnp.arange(sd, dtype=jnp.int32)
    attend_s = lambda qn, qp, c, k: mla_attend_sample(qn, qp, c, k, cache_latent, cache_kpe, page_table, w_uk, w_uv)
    y_sample, conv_sample, latent_sample, kpe_sample = run_group(x_sample, p_sample, cache_conv, pos_s, attend_s)

    return (y_prompt, y_sample, conv_prompt, conv_sample, latent_prompt, kpe_prompt, latent_sample, kpe_sample)
```

```python
import functools
import math

import jax
import jax.numpy as jnp
from jax import lax
from jax.experimental import pallas as pl
from jax.experimental.pallas import tpu as pltpu

F32 = jnp.float32
BF16 = jnp.bfloat16

D_MODEL = 1024
PLE_DIM = 256
CONV_W = 31
N_HEADS = 16
QK_NOPE = 64
QK_ROPE = 32
V_HEAD = 64
Q_LORA = 384
KV_LORA = 256
ROPE_THETA = 10000.0
SCALE = 1.0 / math.sqrt(QK_NOPE + QK_ROPE)
LOG2E = 1.4426950408889634
N_GROUPS = 4
EXPERTS_PER_GROUP = 8
N_EXPERTS = N_GROUPS * EXPERTS_PER_GROUP
TOP_K = 2
EXPERT_FF = 512
PAGE_SIZE = 128
EPS = 1e-6

HEAD_PAD = 128
ROPE_LO = QK_NOPE
ROPE_MID = QK_NOPE + QK_ROPE // 2
ROPE_HI = QK_NOPE + QK_ROPE
HALO = 32
MOE_BM = 128
ROUTER_ROWS = 40
VMEM_LIMIT = 56 * 1024 * 1024


def _cparams(sem):
    return pltpu.CompilerParams(dimension_semantics=sem, vmem_limit_bytes=VMEM_LIMIT)


def _rms(x, g):
    return x * lax.rsqrt(jnp.mean(x * x, axis=-1, keepdims=True) + EPS) * g


def _full(shape):
    n = len(shape)
    return pl.BlockSpec(shape, lambda *_: (0,) * n)


def _conv_in_kernel(x_ref, g_ref, w_ref, b_ref, z_ref):
    u = _rms(x_ref[...], g_ref[...])
    a = jnp.dot(u.astype(BF16), w_ref[...], preferred_element_type=F32) + b_ref[...]
    z_ref[...] = a[:, :D_MODEL] * jax.nn.sigmoid(a[:, D_MODEL:])


def _conv_in(x, g, w_in, b_in, tm):
    t = x.shape[0]
    return pl.pallas_call(
        _conv_in_kernel,
        grid=(t // tm,),
        in_specs=[pl.BlockSpec((tm, D_MODEL), lambda i: (i, 0)),
                  _full((1, D_MODEL)), _full((D_MODEL, 2 * D_MODEL)), _full((1, 2 * D_MODEL))],
        out_specs=pl.BlockSpec((tm, D_MODEL), lambda i: (i, 0)),
        out_shape=jax.ShapeDtypeStruct((t, D_MODEL), F32),
        compiler_params=_cparams(("parallel",)),
        name="conv_in",
    )(x, g, w_in, b_in)


def _ln_silu_out(y, lng, lnb, wout, bout):
    mu = jnp.mean(y, axis=-1, keepdims=True)
    yc = y - mu
    var = jnp.mean(yc * yc, axis=-1, keepdims=True)
    n = yc * lax.rsqrt(var + EPS) * lng + lnb
    act = n * jax.nn.sigmoid(n)
    return jnp.dot(act.astype(BF16), wout, preferred_element_type=F32) + bout


CONV_TS = 256
CONV_RC = 128
CONV_LC = 256


def _conv_mix_kernel(z_ref, halo_ref, ctx_ref, x_ref, wdw_ref, bdw_ref, lng_ref, lnb_ref,
                     wout_ref, bout_ref, h_ref, buf_ref, y_ref):
    s = pl.program_id(1)

    @pl.when(s == 0)
    def _():
        buf_ref[0:HALO, :] = ctx_ref[...]

    @pl.when(s > 0)
    def _():
        buf_ref[0:HALO, :] = halo_ref[...]

    buf_ref[HALO:, :] = z_ref[...]
    first = HALO - (CONV_W - 1)
    for lc in range(D_MODEL // CONV_LC):
        lanes = slice(lc * CONV_LC, (lc + 1) * CONV_LC)
        for rc in range(CONV_TS // CONV_RC):
            r0 = rc * CONV_RC
            acc = jnp.broadcast_to(bdw_ref[:, lanes], (CONV_RC, CONV_LC))
            for k in range(CONV_W):
                acc = acc + wdw_ref[k:k + 1, lanes] * buf_ref[r0 + first + k:r0 + first + k + CONV_RC, lanes]
            y_ref[r0:r0 + CONV_RC, lanes] = acc
    out = _ln_silu_out(y_ref[...], lng_ref[...], lnb_ref[...], wout_ref[...], bout_ref[...])
    h_ref[...] = x_ref[...] + out


def _conv_mix(z3, ctx, x3, wdw, bdw, lng, lnb, wout, bout):
    b, s, d = z3.shape
    ts = CONV_TS
    hb = ts // HALO
    return pl.pallas_call(
        _conv_mix_kernel,
        grid=(b, s // ts),
        in_specs=[pl.BlockSpec((None, ts, d), lambda i, j: (i, j, 0)),
                  pl.BlockSpec((None, HALO, d), lambda i, j: (i, jnp.maximum(j * hb - 1, 0), 0)),
                  pl.BlockSpec((None, HALO, d), lambda i, j: (i, 0, 0)),
                  pl.BlockSpec((None, ts, d), lambda i, j: (i, j, 0)),
                  _full((HALO, d)), _full((1, d)), _full((1, d)), _full((1, d)),
                  _full((d, d)), _full((1, d))],
        out_specs=pl.BlockSpec((None, ts, d), lambda i, j: (i, j, 0)),
        out_shape=jax.ShapeDtypeStruct((b, s, d), F32),
        scratch_shapes=[pltpu.VMEM((ts + HALO, d), F32), pltpu.VMEM((ts, d), F32)],
        compiler_params=_cparams(("parallel", "arbitrary")),
        name="conv_mix",
    )(z3, z3, ctx, x3, wdw, bdw, lng, lnb, wout, bout)


def _conv_step_kernel(ctx_ref, w_ref, z_ref, x_ref, bdw_ref, lng_ref, lnb_ref, wout_ref, bout_ref,
                      h_ref, acc_ref):
    k = pl.program_id(0)

    @pl.when(k == 0)
    def _():
        acc_ref[...] = jnp.broadcast_to(bdw_ref[...], acc_ref.shape)

    @pl.when(k < CONV_W - 1)
    def _():
        acc_ref[...] += w_ref[...] * ctx_ref[...]

    @pl.when(k == CONV_W - 1)
    def _():
        y = acc_ref[...] + w_ref[...] * z_ref[...]
        out = _ln_silu_out(y, lng_ref[...], lnb_ref[...], wout_ref[...], bout_ref[...])
        h_ref[...] = x_ref[...] + out


def _conv_step(ctx_flat, wdw3, z, x, bdw, lng, lnb, wout, bout):
    n, d = z.shape
    last = CONV_W - 2
    return pl.pallas_call(
        _conv_step_kernel,
        grid=(CONV_W,),
        in_specs=[pl.BlockSpec((n, d), lambda k: (0, jnp.minimum(k, last))),
                  pl.BlockSpec((None, 1, d), lambda k: (k, 0, 0)),
                  _full((n, d)), _full((n, d)), _full((1, d)), _full((1, d)), _full((1, d)),
                  _full((d, d)), _full((1, d))],
        out_specs=_full((n, d)),
        out_shape=jax.ShapeDtypeStruct((n, d), F32),
        scratch_shapes=[pltpu.VMEM((n, d), F32)],
        compiler_params=_cparams(("arbitrary",)),
        name="conv_step",
    )(ctx_flat, wdw3, z, x, bdw, lng, lnb, wout, bout)


def _router_kernel(h_ref, g_ref, wr_ref, br_ref, u_ref, eid_ref, gate_ref, rank_ref, cnt_ref, carry_ref):
    i = pl.program_id(0)
    tm = h_ref.shape[0]

    @pl.when(i == 0)
    def _():
        carry_ref[...] = jnp.zeros_like(carry_ref)

    u = _rms(h_ref[...], g_ref[...])
    u_ref[...] = u
    logits = lax.dot_general(wr_ref[...], u, (((1,), (1,)), ((), ())),
                             precision=lax.Precision.HIGHEST, preferred_element_type=F32) + br_ref[...]
    gl = logits[0:N_GROUPS]
    il = logits[8:8 + N_EXPERTS]
    gmax = jnp.max(gl, axis=0, keepdims=True)
    giota = lax.broadcasted_iota(jnp.int32, gl.shape, 0)
    gsel = jnp.min(jnp.where(gl == gmax, giota, N_GROUPS), axis=0, keepdims=True)
    gw = 1.0 / jnp.sum(jnp.exp(gl - gmax), axis=0, keepdims=True)
    eiota = lax.broadcasted_iota(jnp.int32, il.shape, 0)
    ml = jnp.where((eiota >> 3) == gsel, il, -jnp.inf)
    v1 = jnp.max(ml, axis=0, keepdims=True)
    i1 = jnp.min(jnp.where(ml == v1, eiota, N_EXPERTS), axis=0, keepdims=True)
    ml2 = jnp.where(eiota == i1, -jnp.inf, ml)
    v2 = jnp.max(ml2, axis=0, keepdims=True)
    i2 = jnp.min(jnp.where(ml2 == v2, eiota, N_EXPERTS), axis=0, keepdims=True)
    e2 = jnp.exp(v2 - v1)
    p1 = 1.0 / (1.0 + e2)
    eid_ref[0:1, :] = i1
    eid_ref[1:2, :] = i2
    gate_ref[0:1, :] = p1 * gw
    gate_ref[1:2, :] = e2 * p1 * gw
    hit1 = eiota == i1
    hit2 = eiota == i2
    onehot = jnp.where(hit1 | hit2, 1.0, 0.0)
    r = lax.broadcasted_iota(jnp.int32, (tm, tm), 0)
    c = lax.broadcasted_iota(jnp.int32, (tm, tm), 1)
    before = jnp.where(r < c, 1.0, 0.0).astype(BF16)
    cum = jnp.dot(onehot.astype(BF16), before, preferred_element_type=F32) + carry_ref[:, 0:1]
    rank_ref[0:1, :] = jnp.sum(jnp.where(hit1, cum, 0.0), axis=0, keepdims=True).astype(jnp.int32)
    rank_ref[1:2, :] = jnp.sum(jnp.where(hit2, cum, 0.0), axis=0, keepdims=True).astype(jnp.int32)
    carry_ref[...] += jnp.sum(onehot, axis=1, keepdims=True)
    cnt_ref[...] = carry_ref[...]


def _router(h, g, wr, br, tm):
    t = h.shape[0]
    row2 = pl.BlockSpec((TOP_K, tm), lambda i: (0, i))
    return pl.pallas_call(
        _router_kernel,
        grid=(t // tm,),
        in_specs=[pl.BlockSpec((tm, D_MODEL), lambda i: (i, 0)),
                  _full((1, D_MODEL)), _full((ROUTER_ROWS, D_MODEL)), _full((ROUTER_ROWS, 1))],
        out_specs=[pl.BlockSpec((tm, D_MODEL), lambda i: (i, 0)), row2, row2, row2,
                   _full((N_EXPERTS, 128))],
        out_shape=[jax.ShapeDtypeStruct((t, D_MODEL), F32),
                   jax.ShapeDtypeStruct((TOP_K, t), jnp.int32),
                   jax.ShapeDtypeStruct((TOP_K, t), F32),
                   jax.ShapeDtypeStruct((TOP_K, t), jnp.int32),
                   jax.ShapeDtypeStruct((N_EXPERTS, 128), F32)],
        scratch_shapes=[pltpu.VMEM((N_EXPERTS, 128), F32)],
        compiler_params=_cparams(("arbitrary",)),
        name="router",
    )(h, g, wr, br)


def _dispatch_kernel(dest_ref, u_ref, xb_in_ref, xb_ref, sem):
    del xb_in_ref
    i = pl.program_id(0)
    tm = u_ref.shape[0]
    t_all = dest_ref.shape[0] // TOP_K

    def row_copy(t, d):
        return pltpu.make_async_copy(u_ref.at[pl.ds(t, 1)], xb_ref.at[pl.ds(d, 1)], sem)

    def issue(t, carry):
        for k in range(TOP_K):
            row_copy(t, dest_ref[k * t_all + i * tm + t]).start()
        return carry

    lax.fori_loop(0, tm, issue, 0)

    def drain(t, carry):
        for k in range(TOP_K):
            row_copy(t, dest_ref[k * t_all + i * tm + t]).wait()
        return carry

    lax.fori_loop(0, tm, drain, 0)


def _dispatch(dflat, u, n_rows, tm):
    t = u.shape[0]
    xb0 = jnp.zeros((n_rows, D_MODEL), F32)
    return pl.pallas_call(
        _dispatch_kernel,
        grid_spec=pltpu.PrefetchScalarGridSpec(
            num_scalar_prefetch=1,
            grid=(t // tm,),
            in_specs=[pl.BlockSpec((tm, D_MODEL), lambda i, d: (i, 0)),
                      pl.BlockSpec(memory_space=pl.ANY)],
            out_specs=pl.BlockSpec(memory_space=pl.ANY),
            scratch_shapes=[pltpu.SemaphoreType.DMA(())],
        ),
        out_shape=jax.ShapeDtypeStruct((n_rows, D_MODEL), F32),
        input_output_aliases={2: 0},
        compiler_params=_cparams(("arbitrary",)),
        name="moe_dispatch",
    )(dflat, u, xb0)


def _expert_kernel(blk_ref, used_ref, x_ref, w1_ref, w3_ref, w2_ref, y_ref, w1b, w3b, w2b):
    j = pl.program_id(0)
    e = blk_ref[j]
    prev = blk_ref[jnp.maximum(j - 1, 0)]

    @pl.when((j == 0) | (e != prev))
    def _():
        w1b[...] = w1_ref[...].astype(BF16)
        w3b[...] = w3_ref[...].astype(BF16)
        w2b[...] = w2_ref[...].astype(BF16)

    @pl.when(j < used_ref[0])
    def _():
        x = x_ref[...].astype(BF16)
        a = jnp.dot(x, w1b[...], preferred_element_type=F32)
        b = jnp.dot(x, w3b[...], preferred_element_type=F32)
        hmid = (a * jax.nn.sigmoid(a)) * b
        y_ref[...] = jnp.dot(hmid.astype(BF16), w2b[...], preferred_element_type=F32)

    @pl.when(j >= used_ref[0])
    def _():
        y_ref[...] = jnp.zeros_like(y_ref)


def _experts(blk_e, n_used, xb, w1, w3, w2):
    r = xb.shape[0]
    nb = r // MOE_BM
    return pl.pallas_call(
        _expert_kernel,
        grid_spec=pltpu.PrefetchScalarGridSpec(
            num_scalar_prefetch=2,
            grid=(nb,),
            in_specs=[pl.BlockSpec((MOE_BM, D_MODEL), lambda j, b, u: (j, 0)),
                      pl.BlockSpec((None, D_MODEL, EXPERT_FF), lambda j, b, u: (b[j], 0, 0)),
                      pl.BlockSpec((None, D_MODEL, EXPERT_FF), lambda j, b, u: (b[j], 0, 0)),
                      pl.BlockSpec((None, EXPERT_FF, D_MODEL), lambda j, b, u: (b[j], 0, 0))],
            out_specs=pl.BlockSpec((MOE_BM, D_MODEL), lambda j, b, u: (j, 0)),
            scratch_shapes=[pltpu.VMEM((D_MODEL, EXPERT_FF), BF16),
                            pltpu.VMEM((D_MODEL, EXPERT_FF), BF16),
                            pltpu.VMEM((EXPERT_FF, D_MODEL), BF16)],
        ),
        out_shape=jax.ShapeDtypeStruct((r, D_MODEL), F32),
        compiler_params=_cparams(("arbitrary",)),
        name="moe_experts",
    )(blk_e, n_used, xb, w1, w3, w2)


def _combine_kernel(dest_ref, yb_ref, h_ref, gate_ref, p_ref, g_ref, wg_ref, bg_ref, wp_ref, gf_ref,
                    o_ref, ybuf, sem, *, final):
    i = pl.program_id(0)
    tm = h_ref.shape[0]
    t_all = dest_ref.shape[0] // TOP_K

    def row_copy(k, t):
        d = dest_ref[k * t_all + i * tm + t]
        return pltpu.make_async_copy(yb_ref.at[pl.ds(d, 1)], ybuf.at[k, pl.ds(t, 1)], sem)

    def issue(t, carry):
        for k in range(TOP_K):
            row_copy(k, t).start()
        return carry

    lax.fori_loop(0, tm, issue, 0)

    def drain(t, carry):
        for k in range(TOP_K):
            row_copy(k, t).wait()
        return carry

    lax.fori_loop(0, tm, drain, 0)

    gate = gate_ref[...]
    h = h_ref[...] + ybuf[0] * gate[:, 0:1] + ybuf[1] * gate[:, 1:2]
    u = _rms(h, g_ref[...])
    sg = jax.nn.sigmoid(jnp.dot(u.astype(BF16), wg_ref[...], preferred_element_type=F32) + bg_ref[...])
    pp = jnp.dot(p_ref[...].astype(BF16), wp_ref[...], preferred_element_type=F32)
    h = h + sg * pp
    if final:
        h = _rms(h, gf_ref[...])
    o_ref[...] = h


def _combine(dflat, yb, h, gate_t, p, g, wg, bg, wp, gf, tm, final):
    t = h.shape[0]
    c1 = lambda i, d: (0, 0)
    return pl.pallas_call(
        functools.partial(_combine_kernel, final=final),
        grid_spec=pltpu.PrefetchScalarGridSpec(
            num_scalar_prefetch=1,
            grid=(t // tm,),
            in_specs=[pl.BlockSpec(memory_space=pl.ANY),
                      pl.BlockSpec((tm, D_MODEL), lambda i, d: (i, 0)),
                      pl.BlockSpec((tm, TOP_K), lambda i, d: (i, 0)),
                      pl.BlockSpec((tm, PLE_DIM), lambda i, d: (i, 0)),
                      pl.BlockSpec((1, D_MODEL), c1),
                      pl.BlockSpec((D_MODEL, D_MODEL), c1),
                      pl.BlockSpec((1, D_MODEL), c1),
                      pl.BlockSpec((PLE_DIM, D_MODEL), c1),
                      pl.BlockSpec((1, D_MODEL), c1)],
            out_specs=pl.BlockSpec((tm, D_MODEL), lambda i, d: (i, 0)),
            scratch_shapes=[pltpu.VMEM((TOP_K, tm, D_MODEL), F32), pltpu.SemaphoreType.DMA(())],
        ),
        out_shape=jax.ShapeDtypeStruct((t, D_MODEL), F32),
        compiler_params=_cparams(("arbitrary",)),
        name="moe_combine_ple",
    )(dflat, yb, h, gate_t, p, g, wg, bg, wp, gf)


def _rope_group(x, cos, sin, lane):
    half = QK_ROPE // 2
    rot = jnp.where(lane < ROPE_MID, -pltpu.roll(x, HEAD_PAD - half, 1), pltpu.roll(x, half, 1))
    return x * cos + rot * sin


def _kvq_kernel(h_ref, gkv_ref, gmix_ref, wdkv_ref, glat_ref, wkr_ref, wuk_ref, wuv_ref,
                wdq_ref, gq_ref, wuq_ref, cos_ref, sin_ref,
                lat_ref, kpe_ref, kcat_ref, v_ref, q_ref):
    h = h_ref[...]
    tm = h.shape[0]
    cos = cos_ref[...]
    sin = sin_ref[...]
    lane = lax.broadcasted_iota(jnp.int32, (tm, HEAD_PAD), 1)
    ukv = _rms(h, gkv_ref[...]).astype(BF16)
    c = _rms(jnp.dot(ukv, wdkv_ref[...], preferred_element_type=F32), glat_ref[...])
    lat_ref[...] = c
    kpe = _rope_group(jnp.dot(ukv, wkr_ref[...], preferred_element_type=F32), cos, sin, lane)
    kpe_ref[...] = kpe
    cb = c.astype(BF16)
    v_ref[...] = jnp.dot(cb, wuv_ref[...], preferred_element_type=F32).astype(BF16)
    kn = jnp.dot(cb, wuk_ref[...], preferred_element_type=F32)
    umix = _rms(h, gmix_ref[...]).astype(BF16)
    cq = _rms(jnp.dot(umix, wdq_ref[...], preferred_element_type=F32), gq_ref[...])
    q = jnp.dot(cq.astype(BF16), wuq_ref[...], preferred_element_type=F32)
    for hd in range(N_HEADS):
        lanes = slice(hd * HEAD_PAD, (hd + 1) * HEAD_PAD)
        kcat_ref[:, lanes] = (kn[:, lanes] + kpe).astype(BF16)
        q_ref[:, lanes] = (_rope_group(q[:, lanes], cos, sin, lane) * (SCALE * LOG2E)).astype(BF16)


def _kvq(h, gkv, gmix, wdkv, glat, wkr, wuk, wuv, wdq, gq, wuq, cos, sin, tm, n_pos_blocks):
    t = h.shape[0]
    row = lambda w: pl.BlockSpec((tm, w), lambda i: (i, 0))
    return pl.pallas_call(
        _kvq_kernel,
        grid=(t // tm,),
        in_specs=[row(D_MODEL), _full((1, D_MODEL)), _full((1, D_MODEL)),
                  _full((D_MODEL, KV_LORA)), _full((1, KV_LORA)), _full((D_MODEL, HEAD_PAD)),
                  _full((KV_LORA, N_HEADS * HEAD_PAD)), _full((KV_LORA, N_HEADS * V_HEAD)),
                  _full((D_MODEL, Q_LORA)), _full((1, Q_LORA)), _full((Q_LORA, N_HEADS * HEAD_PAD)),
                  pl.BlockSpec((tm, HEAD_PAD), lambda i: (i % n_pos_blocks, 0)),
                  pl.BlockSpec((tm, HEAD_PAD), lambda i: (i % n_pos_blocks, 0))],
        out_specs=[row(KV_LORA), row(HEAD_PAD), row(N_HEADS * HEAD_PAD), row(N_HEADS * V_HEAD),
                   row(N_HEADS * HEAD_PAD)],
        out_shape=[jax.ShapeDtypeStruct((t, KV_LORA), F32),
                   jax.ShapeDtypeStruct((t, HEAD_PAD), F32),
                   jax.ShapeDtypeStruct((t, N_HEADS * HEAD_PAD), BF16),
                   jax.ShapeDtypeStruct((t, N_HEADS * V_HEAD), BF16),
                   jax.ShapeDtypeStruct((t, N_HEADS * HEAD_PAD), BF16)],
        compiler_params=_cparams(("parallel",)),
        name="kvq_proj",
    )(h, gkv, gmix, wdkv, glat, wkr, wuk, wuv, wdq, gq, wuq, cos, sin)


ATT_T = 256


def _flash_kernel(q_ref, k_ref, v_ref, o_ref):
    qi = pl.program_id(2)
    t = ATT_T
    row = lax.broadcasted_iota(jnp.int32, (t, t), 0)
    col = lax.broadcasted_iota(jnp.int32, (t, t), 1)
    outs = []
    for hh in range(2):
        lanes = slice(hh * HEAD_PAD, (hh + 1) * HEAD_PAD)
        q = q_ref[:, lanes]

        def tile(k0, carry, masked):
            m, l, acc = carry
            k = k_ref[pl.ds(k0, t), lanes]
            v = v_ref[pl.ds(k0, t), :]
            s = lax.dot_general(q, k, (((1,), (1,)), ((), ())), preferred_element_type=F32)
            if masked:
                s = jnp.where(col <= row, s, -jnp.inf)
            m_new = jnp.maximum(m, jnp.max(s, axis=-1, keepdims=True))
            alpha = jnp.exp2(m - m_new)
            p = jnp.exp2(s - m_new)
            l = alpha * l + jnp.sum(p, axis=-1, keepdims=True)
            acc = alpha * acc + jnp.dot(p.astype(BF16), v, preferred_element_type=F32)
            return m_new, l, acc

        init = (jnp.full((t, 1), -jnp.inf, F32), jnp.zeros((t, 1), F32), jnp.zeros((t, 2 * V_HEAD), F32))
        carry = lax.fori_loop(0, qi, lambda j, c: tile(pl.multiple_of(j * t, t), c, False), init)
        m, l, acc = tile(pl.multiple_of(qi * t, t), carry, True)
        outs.append(acc / l)
    lane = lax.broadcasted_iota(jnp.int32, (t, 2 * V_HEAD), 1)
    o_ref[...] = jnp.where(lane < V_HEAD, outs[0], outs[1]).astype(BF16)


def _flash(q, kcat, v, b, s):
    t = ATT_T
    nq = s // t
    return pl.pallas_call(
        _flash_kernel,
        grid=(b, N_HEADS // 2, nq),
        in_specs=[pl.BlockSpec((t, 2 * HEAD_PAD), lambda i, h, j: (i * nq + j, h)),
                  pl.BlockSpec((s, 2 * HEAD_PAD), lambda i, h, j: (i, h)),
                  pl.BlockSpec((s, 2 * V_HEAD), lambda i, h, j: (i, h))],
        out_specs=pl.BlockSpec((t, 2 * V_HEAD), lambda i, h, j: (i * nq + j, h)),
        out_shape=jax.ShapeDtypeStruct((b * s, N_HEADS * V_HEAD), BF16),
        compiler_params=_cparams(("parallel", "parallel", "arbitrary")),
        name="flash_attn",
    )(q, kcat, v)


def _proj_res_kernel(a_ref, w_ref, h_ref, o_ref):
    o_ref[...] = h_ref[...] + jnp.dot(a_ref[...], w_ref[...], preferred_element_type=F32)


def _proj_res(a, w, h, tm):
    t, k = a.shape
    return pl.pallas_call(
        _proj_res_kernel,
        grid=(t // tm,),
        in_specs=[pl.BlockSpec((tm, k), lambda i: (i, 0)), _full(w.shape),
                  pl.BlockSpec((tm, D_MODEL), lambda i: (i, 0))],
        out_specs=pl.BlockSpec((tm, D_MODEL), lambda i: (i, 0)),
        out_shape=jax.ShapeDtypeStruct((t, D_MODEL), F32),
        compiler_params=_cparams(("parallel",)),
        name="proj_residual",
    )(a, w, h)


def _qlat_kernel(q_ref, wukt_ref, o_ref):
    for hd in range(N_HEADS):
        qh = q_ref[:, hd * HEAD_PAD:(hd + 1) * HEAD_PAD]
        o_ref[:, hd * KV_LORA:(hd + 1) * KV_LORA] = jnp.dot(
            qh, wukt_ref[hd], preferred_element_type=F32).astype(BF16)


def _qlat(q, wukt):
    n = q.shape[0]
    return pl.pallas_call(
        _qlat_kernel,
        in_specs=[_full(q.shape), _full(wukt.shape)],
        out_specs=_full((n, N_HEADS * KV_LORA)),
        out_shape=jax.ShapeDtypeStruct((n, N_HEADS * KV_LORA), BF16),
        grid=(1,),
        compiler_params=_cparams(("arbitrary",)),
        name="q_absorb",
    )(q, wukt)


PAGES_PER_CHUNK = 8


def _paged_kernel(pt_ref, ql_ref, qp_ref, cn_ref, kn_ref, lat_hbm, kpe_hbm, o_ref, lbuf, kbuf, sems, *, nb):
    b = pl.program_id(0)
    n_pages = pt_ref.shape[0] // nb
    cp = PAGES_PER_CHUNK
    n_chunks = n_pages // cp

    def page_copies(seq, chunk, slot, j):
        page = pt_ref[seq * n_pages + chunk * cp + j]
        dst = pl.ds(j * PAGE_SIZE, PAGE_SIZE)
        return (pltpu.make_async_copy(lat_hbm.at[page], lbuf.at[slot, dst], sems.at[0, slot]),
                pltpu.make_async_copy(kpe_hbm.at[page], kbuf.at[slot, dst], sems.at[1, slot]))

    def start_chunk(seq, chunk, slot):
        for j in range(cp):
            for c in page_copies(seq, chunk, slot, j):
                c.start()

    def wait_chunk(seq, chunk, slot):
        for j in range(cp):
            for c in page_copies(seq, chunk, slot, j):
                c.wait()

    @pl.when(b == 0)
    def _():
        start_chunk(0, 0, 0)

    ql = ql_ref[...]
    qp = qp_ref[...]
    nt = (((1,), (1,)), ((), ()))

    def chunk_body(c, carry):
        m, l, acc = carry
        g = b * n_chunks + c
        slot = g % 2
        wait_chunk(b, c, slot)
        nxt = g + 1

        @pl.when(nxt < nb * n_chunks)
        def _():
            start_chunk(nxt // n_chunks, nxt % n_chunks, 1 - slot)

        kc = lbuf[slot].astype(BF16)
        kp = kbuf[slot].astype(BF16)
        s = (lax.dot_general(ql, kc, nt, preferred_element_type=F32)
             + lax.dot_general(qp, kp, nt, preferred_element_type=F32))
        m_new = jnp.maximum(m, jnp.max(s, axis=-1, keepdims=True))
        alpha = jnp.exp2(m - m_new)
        p = jnp.exp2(s - m_new)
        l = alpha * l + jnp.sum(p, axis=-1, keepdims=True)
        acc = alpha * acc + jnp.dot(p.astype(BF16), kc, preferred_element_type=F32)
        return m_new, l, acc

    init = (jnp.full((N_HEADS, 1), -jnp.inf, F32), jnp.zeros((N_HEADS, 1), F32),
            jnp.zeros((N_HEADS, KV_LORA), F32))
    m, l, acc = lax.fori_loop(0, n_chunks, chunk_body, init)
    cn = cn_ref[...].astype(BF16)
    kn = kn_ref[...].astype(BF16)
    s = (jnp.sum(ql.astype(F32) * cn.astype(F32), axis=-1, keepdims=True)
         + jnp.sum(qp.astype(F32) * kn.astype(F32), axis=-1, keepdims=True))
    m_new = jnp.maximum(m, s)
    alpha = jnp.exp2(m - m_new)
    p = jnp.exp2(s - m_new)
    l = alpha * l + p
    acc = alpha * acc + p.astype(BF16).astype(F32) * cn.astype(F32)
    o_ref[...] = acc / l


def _paged_attn(pt_flat, ql, qp, c_new, k_new, cache_latent, cache_kpe):
    n = ql.shape[0]
    rows = PAGES_PER_CHUNK * PAGE_SIZE
    per_seq = lambda w: pl.BlockSpec((None,) + w, lambda i, pt: (i, 0, 0))
    return pl.pallas_call(
        functools.partial(_paged_kernel, nb=n),
        grid_spec=pltpu.PrefetchScalarGridSpec(
            num_scalar_prefetch=1,
            grid=(n,),
            in_specs=[per_seq((N_HEADS, KV_LORA)), per_seq((N_HEADS, QK_ROPE)),
                      per_seq((1, KV_LORA)), per_seq((1, QK_ROPE)),
                      pl.BlockSpec(memory_space=pl.ANY), pl.BlockSpec(memory_space=pl.ANY)],
            out_specs=per_seq((N_HEADS, KV_LORA)),
            scratch_shapes=[pltpu.VMEM((2, rows, KV_LORA), F32),
                            pltpu.VMEM((2, rows, QK_ROPE), F32),
                            pltpu.SemaphoreType.DMA((2, 2))],
        ),
        out_shape=jax.ShapeDtypeStruct((n, N_HEADS, KV_LORA), F32),
        compiler_params=_cparams(("arbitrary",)),
        name="paged_attn",
    )(pt_flat, ql, qp, c_new, k_new, cache_latent, cache_kpe)


def _attn_out_kernel(ol_ref, wuv_ref, wo_ref, h_ref, o_ref):
    acc = h_ref[...]
    for hd in range(N_HEADS):
        oh = ol_ref[:, hd * KV_LORA:(hd + 1) * KV_LORA].astype(BF16)
        th = jnp.dot(oh, wuv_ref[hd], preferred_element_type=F32).astype(BF16)
        acc = acc + jnp.dot(th, wo_ref[hd * V_HEAD:(hd + 1) * V_HEAD, :], preferred_element_type=F32)
    o_ref[...] = acc


def _attn_out(ol, wuv_h, wo, h):
    n = h.shape[0]
    return pl.pallas_call(
        _attn_out_kernel,
        grid=(1,),
        in_specs=[_full(ol.shape), _full(wuv_h.shape), _full(wo.shape), _full(h.shape)],
        out_specs=_full(h.shape),
        out_shape=jax.ShapeDtypeStruct((n, D_MODEL), F32),
        compiler_params=_cparams(("arbitrary",)),
        name="attn_out",
    )(ol, wuv_h, wo, h)


def _rope_tables(pos):
    half = QK_ROPE // 2
    inv = ROPE_THETA ** (-jnp.arange(half, dtype=F32) / half)
    ang = pos.astype(F32)[:, None] * inv[None, :]
    cos, sin = jnp.cos(ang), jnp.sin(ang)
    n = pos.shape[0]
    cos_t = jnp.concatenate([jnp.ones((n, QK_NOPE), F32), cos, cos,
                             jnp.ones((n, HEAD_PAD - ROPE_HI), F32)], axis=1)
    sin_t = jnp.concatenate([jnp.zeros((n, QK_NOPE), F32), sin, sin,
                             jnp.zeros((n, HEAD_PAD - ROPE_HI), F32)], axis=1)
    return cos_t, sin_t


def _moe(h, g_ffn, wr, br, w1, w3, w2, tm_router, tm_rows):
    t = h.shape[0]
    u, eid, gate, rank, cnt = _router(h, g_ffn, wr, br, tm_router)
    counts = cnt[:, 0].astype(jnp.int32)
    padded = (counts + MOE_BM - 1) // MOE_BM * MOE_BM
    pends = jnp.cumsum(padded)
    pstarts = pends - padded
    dest = pstarts[eid] + rank
    n_blocks = (t * TOP_K + N_EXPERTS * (MOE_BM - 1) + MOE_BM - 1) // MOE_BM
    blk_e = jnp.minimum(jnp.searchsorted(pends, jnp.arange(n_blocks, dtype=jnp.int32) * MOE_BM,
                                         side='right'), N_EXPERTS - 1).astype(jnp.int32)
    n_used = (pends[-1:] // MOE_BM).astype(jnp.int32)
    dflat = dest.reshape(-1).astype(jnp.int32)
    xb = _dispatch(dflat, u, n_blocks * MOE_BM, tm_rows)
    yb = _experts(blk_e, n_used, xb, w1, w3, w2)
    return dflat, yb, gate.T


def kernel(x_prompt, x_sample, cache_conv, cache_latent, cache_kpe, page_table, p_prompt, p_sample, g_mix, g_ffn, g_ple, g_final, conv_w_in, conv_b_in, conv_w_dw, conv_b_dw, conv_ln_g, conv_ln_b, conv_w_out, conv_b_out, g_kv_in, w_dkv, g_kv_lat, w_kr, w_uk, w_uv, w_dq, g_q_lat, w_uq, w_o, w_group, b_group, w_inner, b_inner, w_exp_gate, w_exp_up, w_exp_down, w_ple_gate, b_ple_gate, w_ple_proj):
    bp, sp, d = x_prompt.shape
    nd = x_sample.shape[0]
    tp = bp * sp
    row = lambda a: a.reshape(1, -1)

    w_in_b = conv_w_in[0].astype(BF16)
    w_out_b = conv_w_out[0].astype(BF16)
    wdw_pad = jnp.concatenate([conv_w_dw[0], jnp.zeros((HALO - CONV_W, d), F32)], axis=0)
    wr, br = [], []
    for i in range(2):
        wr.append(jnp.concatenate([w_group[i].T, jnp.zeros((8 - N_GROUPS, d), F32),
                                   w_inner[i].reshape(d, N_EXPERTS).T], axis=0))
        br.append(jnp.concatenate([b_group[i], jnp.zeros((8 - N_GROUPS,), F32),
                                   b_inner[i].reshape(N_EXPERTS)]).reshape(ROUTER_ROWS, 1))
    w_dkv_b = w_dkv.astype(BF16)
    w_kr_pad = jnp.zeros((d, HEAD_PAD), F32).at[:, ROPE_LO:ROPE_HI].set(w_kr).astype(BF16)
    w_uk_pad = jnp.pad(w_uk, ((0, 0), (0, 0), (0, HEAD_PAD - QK_NOPE))).reshape(KV_LORA, -1).astype(BF16)
    w_uv_b = w_uv.reshape(KV_LORA, -1).astype(BF16)
    w_dq_b = w_dq[0].astype(BF16)
    w_uq_pad = jnp.pad(w_uq[0].reshape(Q_LORA, N_HEADS, QK_NOPE + QK_ROPE),
                       ((0, 0), (0, 0), (0, HEAD_PAD - ROPE_HI))).reshape(Q_LORA, -1).astype(BF16)
    w_o_b = w_o[0].astype(BF16)
    w_ukt_pad = jnp.pad(jnp.transpose(w_uk, (1, 2, 0)),
                        ((0, 0), (0, HEAD_PAD - QK_NOPE), (0, 0))).astype(BF16)
    w_uv_h = jnp.transpose(w_uv, (1, 0, 2)).astype(BF16)
    wg_b = [w_ple_gate[i].astype(BF16) for i in range(2)]
    wp_b = [w_ple_proj[i].astype(BF16) for i in range(2)]

    def moe_ple(h, layer, p, tm_router, tm_rows, final):
        dflat, yb, gate_t = _moe(h, row(g_ffn[layer]), wr[layer], br[layer],
                                 w_exp_gate[layer], w_exp_up[layer], w_exp_down[layer], tm_router, tm_rows)
        return _combine(dflat, yb, h, gate_t, p, row(g_ple[layer]), wg_b[layer], row(b_ple_gate[layer]),
                        wp_b[layer], row(g_final), tm_rows, final)

    def kvq(h, cos_t, sin_t, tm, n_pos_blocks):
        return _kvq(h, row(g_kv_in), row(g_mix[1]), w_dkv_b, row(g_kv_lat), w_kr_pad, w_uk_pad, w_uv_b,
                    w_dq_b, row(g_q_lat[0]), w_uq_pad, cos_t, sin_t, tm, n_pos_blocks)

    xp = x_prompt.reshape(tp, d)
    z = _conv_in(xp, row(g_mix[0]), w_in_b, row(conv_b_in[0]), 512)
    z3 = z.reshape(bp, sp, d)
    conv_prompt = z3[:, sp - (CONV_W - 1):, :][None]
    ctx0 = jnp.zeros((bp, HALO, d), F32)
    h = _conv_mix(z3, ctx0, x_prompt, wdw_pad, row(conv_b_dw[0]), row(conv_ln_g[0]), row(conv_ln_b[0]),
                  w_out_b, row(conv_b_out[0])).reshape(tp, d)
    h = moe_ple(h, 0, p_prompt[0].reshape(tp, PLE_DIM), 512, 256, False)
    cos_p, sin_p = _rope_tables(jnp.arange(sp, dtype=jnp.int32))
    lat_p, kpe_p, kcat, v, q = kvq(h, cos_p, sin_p, 512, sp // 512)
    o = _flash(q, kcat, v, bp, sp)
    h = _proj_res(o, w_o_b, h, 512)
    y_prompt = moe_ple(h, 1, p_prompt[1].reshape(tp, PLE_DIM), 512, 256, True).reshape(bp, sp, d)
    latent_prompt = lat_p.reshape(bp, sp, KV_LORA)
    kpe_prompt = kpe_p[:, ROPE_LO:ROPE_HI].reshape(bp, sp, QK_ROPE)

    xs = x_sample.reshape(nd, d)
    zs = _conv_in(xs, row(g_mix[0]), w_in_b, row(conv_b_in[0]), nd)
    conv_sample = jnp.concatenate([cache_conv[0][:, 1:, :], zs[:, None, :]], axis=1)[None]
    hs = _conv_step(cache_conv[0].reshape(nd, (CONV_W - 1) * d), conv_w_dw[0][:, None, :], zs, xs,
                    row(conv_b_dw[0]), row(conv_ln_g[0]), row(conv_ln_b[0]), w_out_b, row(conv_b_out[0]))
    hs = moe_ple(hs, 0, p_sample[0].reshape(nd, PLE_DIM), nd, nd, False)
    n_pages = page_table.shape[1]
    pos_s = jnp.full((nd,), n_pages * PAGE_SIZE, jnp.int32)
    cos_s, sin_s = _rope_tables(pos_s)
    lat_s, kpe_s, _, _, qs = kvq(hs, cos_s, sin_s, nd, 1)
    ql = _qlat(qs, w_ukt_pad).reshape(nd, N_HEADS, KV_LORA)
    qp = qs.reshape(nd, N_HEADS, HEAD_PAD)[:, :, ROPE_LO:ROPE_HI]
    kpe_new = kpe_s[:, ROPE_LO:ROPE_HI]
    o_lat = _paged_attn(page_table.reshape(-1), ql, qp, lat_s.reshape(nd, 1, KV_LORA),
                        kpe_new.reshape(nd, 1, QK_ROPE), cache_latent, cache_kpe)
    hs = _attn_out(o_lat.reshape(nd, N_HEADS * KV_LORA), w_uv_h, w_o_b, hs)
    y_sample = moe_ple(hs, 1, p_sample[1].reshape(nd, PLE_DIM), nd, nd, True).reshape(nd, 1, d)
    latent_sample = lat_s.reshape(nd, 1, KV_LORA)
    kpe_sample = kpe_new.reshape(nd, 1, QK_ROPE)

    return (y_prompt, y_sample, conv_prompt, conv_sample, latent_prompt, kpe_prompt, latent_sample, kpe_sample)
```

```python
import functools
import math

import jax
import jax.numpy as jnp
from jax import lax
from jax.experimental import pallas as pl
from jax.experimental.pallas import tpu as pltpu

F32 = jnp.float32
BF16 = jnp.bfloat16

D_MODEL = 1024
PLE_DIM = 256
CONV_W = 31
N_HEADS = 16
QK_NOPE = 64
QK_ROPE = 32
V_HEAD = 64
Q_LORA = 384
KV_LORA = 256
ROPE_THETA = 10000.0
SCALE = 1.0 / math.sqrt(QK_NOPE + QK_ROPE)
LOG2E = 1.4426950408889634
N_GROUPS = 4
EXPERTS_PER_GROUP = 8
N_EXPERTS = N_GROUPS * EXPERTS_PER_GROUP
TOP_K = 2
EXPERT_FF = 512
PAGE_SIZE = 128
EPS = 1e-6

HEAD_PAD = 128
ROPE_LO = QK_NOPE
ROPE_MID = QK_NOPE + QK_ROPE // 2
ROPE_HI = QK_NOPE + QK_ROPE
HALO = 32
MOE_HALF = 128
MOE_BM = 2 * MOE_HALF
ROUTER_ROWS = 40
RANK_BITS = 20
VMEM_LIMIT = 56 * 1024 * 1024


def _cparams(sem):
    return pltpu.CompilerParams(dimension_semantics=sem, vmem_limit_bytes=VMEM_LIMIT)


def _rms(x, g):
    return x * lax.rsqrt(jnp.mean(x * x, axis=-1, keepdims=True) + EPS) * g


def _full(shape):
    n = len(shape)
    return pl.BlockSpec(shape, lambda *_: (0,) * n)


LANES = 128
TOK_ROWS = D_MODEL // LANES


def _load_token_tiles(ref, n):
    return jnp.concatenate([ref[pl.ds(s, n, stride=TOK_ROWS), :] for s in range(TOK_ROWS)], axis=1)


def _store_token_tiles(ref, x):
    n = x.shape[0]
    for s in range(TOK_ROWS):
        ref[pl.ds(s, n, stride=TOK_ROWS), :] = x[:, s * LANES:(s + 1) * LANES]


def _conv_in_kernel(x_ref, g_ref, w_ref, b_ref, z_ref):
    u = _rms(x_ref[...], g_ref[...])
    a = jnp.dot(u.astype(BF16), w_ref[...], preferred_element_type=F32) + b_ref[...]
    z_ref[...] = a[:, :D_MODEL] * jax.nn.sigmoid(a[:, D_MODEL:])


def _conv_in(x, g, w_in, b_in, tm):
    t = x.shape[0]
    return pl.pallas_call(
        _conv_in_kernel,
        grid=(t // tm,),
        in_specs=[pl.BlockSpec((tm, D_MODEL), lambda i: (i, 0)),
                  _full((1, D_MODEL)), _full((D_MODEL, 2 * D_MODEL)), _full((1, 2 * D_MODEL))],
        out_specs=pl.BlockSpec((tm, D_MODEL), lambda i: (i, 0)),
        out_shape=jax.ShapeDtypeStruct((t, D_MODEL), F32),
        compiler_params=_cparams(("parallel",)),
        name="conv_in",
    )(x, g, w_in, b_in)


def _ln_silu_out(y, lng, lnb, wout, bout):
    mu = jnp.mean(y, axis=-1, keepdims=True)
    yc = y - mu
    var = jnp.mean(yc * yc, axis=-1, keepdims=True)
    n = yc * lax.rsqrt(var + EPS) * lng + lnb
    act = n * jax.nn.sigmoid(n)
    return jnp.dot(act.astype(BF16), wout, preferred_element_type=F32) + bout


CONV_TS = 256
CONV_RC = 128
CONV_LC = 256


def _conv_mix_kernel(z_ref, halo_ref, ctx_ref, x_ref, wdw_ref, bdw_ref, lng_ref, lnb_ref,
                     wout_ref, bout_ref, h_ref, buf_ref, y_ref):
    s = pl.program_id(1)

    @pl.when(s == 0)
    def _():
        buf_ref[0:HALO, :] = ctx_ref[...]

    @pl.when(s > 0)
    def _():
        buf_ref[0:HALO, :] = halo_ref[...]

    buf_ref[HALO:, :] = z_ref[...]
    first = HALO - (CONV_W - 1)
    for lc in range(D_MODEL // CONV_LC):
        lanes = slice(lc * CONV_LC, (lc + 1) * CONV_LC)
        for rc in range(CONV_TS // CONV_RC):
            r0 = rc * CONV_RC
            acc = jnp.broadcast_to(bdw_ref[:, lanes], (CONV_RC, CONV_LC))
            for k in range(CONV_W):
                acc = acc + wdw_ref[k:k + 1, lanes] * buf_ref[r0 + first + k:r0 + first + k + CONV_RC, lanes]
            y_ref[r0:r0 + CONV_RC, lanes] = acc
    out = _ln_silu_out(y_ref[...], lng_ref[...], lnb_ref[...], wout_ref[...], bout_ref[...])
    h_ref[...] = x_ref[...] + out


def _conv_mix(z3, ctx, x3, wdw, bdw, lng, lnb, wout, bout):
    b, s, d = z3.shape
    ts = CONV_TS
    hb = ts // HALO
    return pl.pallas_call(
        _conv_mix_kernel,
        grid=(b, s // ts),
        in_specs=[pl.BlockSpec((None, ts, d), lambda i, j: (i, j, 0)),
                  pl.BlockSpec((None, HALO, d), lambda i, j: (i, jnp.maximum(j * hb - 1, 0), 0)),
                  pl.BlockSpec((None, HALO, d), lambda i, j: (i, 0, 0)),
                  pl.BlockSpec((None, ts, d), lambda i, j: (i, j, 0)),
                  _full((HALO, d)), _full((1, d)), _full((1, d)), _full((1, d)),
                  _full((d, d)), _full((1, d))],
        out_specs=pl.BlockSpec((None, ts, d), lambda i, j: (i, j, 0)),
        out_shape=jax.ShapeDtypeStruct((b, s, d), F32),
        scratch_shapes=[pltpu.VMEM((ts + HALO, d), F32), pltpu.VMEM((ts, d), F32)],
        compiler_params=_cparams(("parallel", "arbitrary")),
        name="conv_mix",
    )(z3, z3, ctx, x3, wdw, bdw, lng, lnb, wout, bout)


def _conv_step_kernel(ctx_ref, w_ref, z_ref, x_ref, bdw_ref, lng_ref, lnb_ref, wout_ref, bout_ref,
                      h_ref, acc_ref):
    k = pl.program_id(0)

    @pl.when(k == 0)
    def _():
        acc_ref[...] = jnp.broadcast_to(bdw_ref[...], acc_ref.shape)

    @pl.when(k < CONV_W - 1)
    def _():
        acc_ref[...] += w_ref[...] * ctx_ref[...]

    @pl.when(k == CONV_W - 1)
    def _():
        y = acc_ref[...] + w_ref[...] * z_ref[...]
        out = _ln_silu_out(y, lng_ref[...], lnb_ref[...], wout_ref[...], bout_ref[...])
        h_ref[...] = x_ref[...] + out


def _conv_step(ctx_t, wdw3, z, x, bdw, lng, lnb, wout, bout):
    n, d = z.shape
    last = CONV_W - 2
    return pl.pallas_call(
        _conv_step_kernel,
        grid=(CONV_W,),
        in_specs=[pl.BlockSpec((None, n, d), lambda k: (jnp.minimum(k, last), 0, 0)),
                  pl.BlockSpec((None, 1, d), lambda k: (k, 0, 0)),
                  _full((n, d)), _full((n, d)), _full((1, d)), _full((1, d)), _full((1, d)),
                  _full((d, d)), _full((1, d))],
        out_specs=_full((n, d)),
        out_shape=jax.ShapeDtypeStruct((n, d), F32),
        scratch_shapes=[pltpu.VMEM((n, d), F32)],
        compiler_params=_cparams(("arbitrary",)),
        name="conv_step",
    )(ctx_t, wdw3, z, x, bdw, lng, lnb, wout, bout)


def _router_kernel(h_ref, g_ref, wr_ref, br_ref, u_ref, code_ref, gate_ref, cnt_ref, carry_ref):
    i = pl.program_id(0)
    tm = h_ref.shape[0]

    @pl.when(i == 0)
    def _():
        carry_ref[...] = jnp.zeros_like(carry_ref)

    u = _rms(h_ref[...], g_ref[...])
    _store_token_tiles(u_ref, u)
    logits = lax.dot_general(wr_ref[...], u, (((1,), (1,)), ((), ())),
                             precision=lax.Precision.HIGHEST, preferred_element_type=F32) + br_ref[...]
    gl = logits[0:N_GROUPS]
    il = logits[8:8 + N_EXPERTS]
    gmax = jnp.max(gl, axis=0, keepdims=True)
    giota = lax.broadcasted_iota(jnp.int32, gl.shape, 0)
    gsel = jnp.min(jnp.where(gl == gmax, giota, N_GROUPS), axis=0, keepdims=True)
    gw = 1.0 / jnp.sum(jnp.exp(gl - gmax), axis=0, keepdims=True)
    eiota = lax.broadcasted_iota(jnp.int32, il.shape, 0)
    ml = jnp.where((eiota >> 3) == gsel, il, -jnp.inf)
    v1 = jnp.max(ml, axis=0, keepdims=True)
    i1 = jnp.min(jnp.where(ml == v1, eiota, N_EXPERTS), axis=0, keepdims=True)
    ml2 = jnp.where(eiota == i1, -jnp.inf, ml)
    v2 = jnp.max(ml2, axis=0, keepdims=True)
    i2 = jnp.min(jnp.where(ml2 == v2, eiota, N_EXPERTS), axis=0, keepdims=True)
    e2 = jnp.exp(v2 - v1)
    p1 = 1.0 / (1.0 + e2)
    gate_ref[0:1, :] = p1 * gw
    gate_ref[1:2, :] = e2 * p1 * gw
    hit1 = eiota == i1
    hit2 = eiota == i2
    onehot = jnp.where(hit1 | hit2, 1.0, 0.0)
    r = lax.broadcasted_iota(jnp.int32, (tm, tm), 0)
    c = lax.broadcasted_iota(jnp.int32, (tm, tm), 1)
    before = jnp.where(r < c, 1.0, 0.0).astype(BF16)
    cum = jnp.dot(onehot.astype(BF16), before, preferred_element_type=F32) + carry_ref[:, 0:1]
    rank1 = jnp.sum(jnp.where(hit1, cum, 0.0), axis=0, keepdims=True).astype(jnp.int32)
    rank2 = jnp.sum(jnp.where(hit2, cum, 0.0), axis=0, keepdims=True).astype(jnp.int32)
    code_ref[0:1, :] = (i1 << RANK_BITS) | rank1
    code_ref[1:2, :] = (i2 << RANK_BITS) | rank2
    carry_ref[...] += jnp.sum(onehot, axis=1, keepdims=True)
    cnt_ref[...] = carry_ref[...]


def _router(h, g, wr, br, tm):
    t = h.shape[0]
    row2 = pl.BlockSpec((TOP_K, tm), lambda i: (0, i))
    return pl.pallas_call(
        _router_kernel,
        grid=(t // tm,),
        in_specs=[pl.BlockSpec((tm, D_MODEL), lambda i: (i, 0)),
                  _full((1, D_MODEL)), _full((ROUTER_ROWS, D_MODEL)), _full((ROUTER_ROWS, 1))],
        out_specs=[pl.BlockSpec((tm * TOK_ROWS, LANES), lambda i: (i, 0)), row2, row2,
                   _full((N_EXPERTS, 128))],
        out_shape=[jax.ShapeDtypeStruct((t * TOK_ROWS, LANES), F32),
                   jax.ShapeDtypeStruct((TOP_K, t), jnp.int32),
                   jax.ShapeDtypeStruct((TOP_K, t), F32),
                   jax.ShapeDtypeStruct((N_EXPERTS, 128), F32)],
        scratch_shapes=[pltpu.VMEM((N_EXPERTS, 128), F32)],
        compiler_params=_cparams(("arbitrary",)),
        name="router",
    )(h, g, wr, br)


def _expert_kernel(blk_ref, inv_ref, u_hbm, w1_ref, w3_ref, w2_ref, yp_hbm, xbuf, ybuf, w1b, w3b, w2b,
                   gsem, ssem, *, n_tok, n_steps):
    j = pl.program_id(0)
    trash0 = TOP_K * n_tok
    prime0 = n_steps * MOE_BM

    half_rows = MOE_HALF * TOK_ROWS

    def tile_of(t):
        return pl.ds(pl.multiple_of(t * TOK_ROWS, TOK_ROWS), TOK_ROWS)

    def gather_start(step, half):
        base = step * MOE_BM + half * MOE_HALF
        for r in range(MOE_HALF):
            p = inv_ref[base + r]
            tok = jnp.where(p >= 0, p & (n_tok - 1), 0)
            pltpu.make_async_copy(u_hbm.at[tile_of(tok)], xbuf.at[half, tile_of(r)], gsem.at[half]).start()

    def gather_wait(half):
        pltpu.make_async_copy(u_hbm.at[pl.ds(0, half_rows)], xbuf.at[half], gsem.at[half]).wait()

    def scatter_start(step, half):
        base = step * MOE_BM + half * MOE_HALF
        for r in range(MOE_HALF):
            p = inv_ref[base + r]
            dst = jnp.where(p >= 0, p, trash0 - 1 - p)
            pltpu.make_async_copy(ybuf.at[half, tile_of(r)], yp_hbm.at[tile_of(dst)], ssem.at[half]).start()

    def scatter_wait(half):
        pltpu.make_async_copy(ybuf.at[half], yp_hbm.at[pl.ds(0, half_rows)], ssem.at[half]).wait()

    @pl.when(j == 0)
    def _():
        ybuf[...] = jnp.zeros_like(ybuf)
        for half in range(2):
            pltpu.make_async_copy(ybuf.at[half],
                                  yp_hbm.at[pl.ds((prime0 + half * MOE_HALF) * TOK_ROWS, half_rows)],
                                  ssem.at[half]).start()
        gather_start(0, 0)

    e = blk_ref[j]
    prev = blk_ref[jnp.maximum(j - 1, 0)]

    @pl.when((j == 0) | (e != prev))
    def _():
        w1b[...] = w1_ref[...].astype(BF16)
        w3b[...] = w3_ref[...].astype(BF16)
        w2b[...] = w2_ref[...].astype(BF16)

    def compute(half):
        x = _load_token_tiles(xbuf.at[half], MOE_HALF).astype(BF16)
        a = jnp.dot(x, w1b[...], preferred_element_type=F32)
        b = jnp.dot(x, w3b[...], preferred_element_type=F32)
        hmid = (a * jax.nn.sigmoid(a)) * b
        _store_token_tiles(ybuf.at[half], jnp.dot(hmid.astype(BF16), w2b[...], preferred_element_type=F32))

    gather_wait(0)
    gather_start(j, 1)
    scatter_wait(0)
    compute(0)
    scatter_start(j, 0)
    gather_wait(1)
    gather_start(jnp.minimum(j + 1, n_steps - 1), 0)
    scatter_wait(1)
    compute(1)
    scatter_start(j, 1)

    @pl.when(j == n_steps - 1)
    def _():
        gather_wait(0)
        scatter_wait(0)
        scatter_wait(1)


def _experts(blk_e, inv, u, w1, w3, w2, layer, n_steps, n_out_rows):
    n_tok = u.shape[0] // TOK_ROWS
    assert n_tok & (n_tok - 1) == 0
    wspec = lambda s: pl.BlockSpec((None, None) + s, lambda j, b, i: (layer, b[j], 0, 0))
    return pl.pallas_call(
        functools.partial(_expert_kernel, n_tok=n_tok, n_steps=n_steps),
        grid_spec=pltpu.PrefetchScalarGridSpec(
            num_scalar_prefetch=2,
            grid=(n_steps,),
            in_specs=[pl.BlockSpec(memory_space=pl.ANY),
                      wspec((D_MODEL, EXPERT_FF)), wspec((D_MODEL, EXPERT_FF)), wspec((EXPERT_FF, D_MODEL))],
            out_specs=pl.BlockSpec(memory_space=pl.ANY),
            scratch_shapes=[pltpu.VMEM((2, MOE_HALF * TOK_ROWS, LANES), F32),
                            pltpu.VMEM((2, MOE_HALF * TOK_ROWS, LANES), F32),
                            pltpu.VMEM((D_MODEL, EXPERT_FF), BF16),
                            pltpu.VMEM((D_MODEL, EXPERT_FF), BF16),
                            pltpu.VMEM((EXPERT_FF, D_MODEL), BF16),
                            pltpu.SemaphoreType.DMA((2,)),
                            pltpu.SemaphoreType.DMA((2,))],
        ),
        out_shape=jax.ShapeDtypeStruct((n_out_rows * TOK_ROWS, LANES), F32),
        compiler_params=_cparams(("arbitrary",)),
        name="moe_experts",
    )(blk_e, inv, u, w1, w3, w2)


def _combine_kernel(y0_ref, y1_ref, h_ref, gate_ref, p_ref, g_ref, wg_ref, bg_ref, wp_ref, gf_ref,
                    o_ref, *, final):
    gate = gate_ref[...]
    tm = h_ref.shape[0]
    y0 = _load_token_tiles(y0_ref, tm)
    y1 = _load_token_tiles(y1_ref, tm)
    h = h_ref[...] + y0 * gate[:, 0:1] + y1 * gate[:, 1:2]
    u = _rms(h, g_ref[...])
    sg = jax.nn.sigmoid(jnp.dot(u.astype(BF16), wg_ref[...], preferred_element_type=F32) + bg_ref[...])
    pp = jnp.dot(p_ref[...].astype(BF16), wp_ref[...], preferred_element_type=F32)
    h = h + sg * pp
    if final:
        h = _rms(h, gf_ref[...])
    o_ref[...] = h


def _combine(yp, h, gate_t, p, g, wg, bg, wp, gf, tm, final):
    t = h.shape[0]
    nt = t // tm
    return pl.pallas_call(
        functools.partial(_combine_kernel, final=final),
        grid=(nt,),
        in_specs=[pl.BlockSpec((tm * TOK_ROWS, LANES), lambda i: (i, 0)),
                  pl.BlockSpec((tm * TOK_ROWS, LANES), lambda i: (nt + i, 0)),
                  pl.BlockSpec((tm, D_MODEL), lambda i: (i, 0)),
                  pl.BlockSpec((tm, TOP_K), lambda i: (i, 0)),
                  pl.BlockSpec((tm, PLE_DIM), lambda i: (i, 0)),
                  _full((1, D_MODEL)), _full((D_MODEL, D_MODEL)), _full((1, D_MODEL)),
                  _full((PLE_DIM, D_MODEL)), _full((1, D_MODEL))],
        out_specs=pl.BlockSpec((tm, D_MODEL), lambda i: (i, 0)),
        out_shape=jax.ShapeDtypeStruct((t, D_MODEL), F32),
        compiler_params=_cparams(("parallel",)),
        name="moe_combine_ple",
    )(yp, yp, h, gate_t, p, g, wg, bg, wp, gf)


def _rope_group(x, cos, sin, lane):
    half = QK_ROPE // 2
    rot = jnp.where(lane < ROPE_MID, -pltpu.roll(x, HEAD_PAD - half, 1), pltpu.roll(x, half, 1))
    return x * cos + rot * sin


def _kvq_kernel(h_ref, gkv_ref, gmix_ref, wdkv_ref, glat_ref, wkr_ref, wuk_ref, wuv_ref,
                wdq_ref, gq_ref, wuq_ref, cos_ref, sin_ref,
                lat_ref, kpe_ref, kcat_ref, v_ref, q_ref):
    h = h_ref[...]
    tm = h.shape[0]
    cos = cos_ref[...]
    sin = sin_ref[...]
    lane = lax.broadcasted_iota(jnp.int32, (tm, HEAD_PAD), 1)
    ukv = _rms(h, gkv_ref[...]).astype(BF16)
    c = _rms(jnp.dot(ukv, wdkv_ref[...], preferred_element_type=F32), glat_ref[...])
    lat_ref[...] = c
    kpe = _rope_group(jnp.dot(ukv, wkr_ref[...], preferred_element_type=F32), cos, sin, lane)
    kpe_ref[...] = kpe
    cb = c.astype(BF16)
    v_ref[...] = jnp.dot(cb, wuv_ref[...], preferred_element_type=F32).astype(BF16)
    kn = jnp.dot(cb, wuk_ref[...], preferred_element_type=F32)
    umix = _rms(h, gmix_ref[...]).astype(BF16)
    cq = _rms(jnp.dot(umix, wdq_ref[...], preferred_element_type=F32), gq_ref[...])
    q = jnp.dot(cq.astype(BF16), wuq_ref[...], preferred_element_type=F32)
    for hd in range(N_HEADS):
        lanes = slice(hd * HEAD_PAD, (hd + 1) * HEAD_PAD)
        kcat_ref[:, lanes] = (kn[:, lanes] + kpe).astype(BF16)
        q_ref[:, lanes] = (_rope_group(q[:, lanes], cos, sin, lane) * (SCALE * LOG2E)).astype(BF16)


def _kvq(h, gkv, gmix, wdkv, glat, wkr, wuk, wuv, wdq, gq, wuq, cos, sin, tm, n_pos_blocks):
    t = h.shape[0]
    row = lambda w: pl.BlockSpec((tm, w), lambda i: (i, 0))
    return pl.pallas_call(
        _kvq_kernel,
        grid=(t // tm,),
        in_specs=[row(D_MODEL), _full((1, D_MODEL)), _full((1, D_MODEL)),
                  _full((D_MODEL, KV_LORA)), _full((1, KV_LORA)), _full((D_MODEL, HEAD_PAD)),
                  _full((KV_LORA, N_HEADS * HEAD_PAD)), _full((KV_LORA, N_HEADS * V_HEAD)),
                  _full((D_MODEL, Q_LORA)), _full((1, Q_LORA)), _full((Q_LORA, N_HEADS * HEAD_PAD)),
                  pl.BlockSpec((tm, HEAD_PAD), lambda i: (i % n_pos_blocks, 0)),
                  pl.BlockSpec((tm, HEAD_PAD), lambda i: (i % n_pos_blocks, 0))],
        out_specs=[row(KV_LORA), row(HEAD_PAD), row(N_HEADS * HEAD_PAD), row(N_HEADS * V_HEAD),
                   row(N_HEADS * HEAD_PAD)],
        out_shape=[jax.ShapeDtypeStruct((t, KV_LORA), F32),
                   jax.ShapeDtypeStruct((t, HEAD_PAD), F32),
                   jax.ShapeDtypeStruct((t, N_HEADS * HEAD_PAD), BF16),
                   jax.ShapeDtypeStruct((t, N_HEADS * V_HEAD), BF16),
                   jax.ShapeDtypeStruct((t, N_HEADS * HEAD_PAD), BF16)],
        compiler_params=_cparams(("parallel",)),
        name="kvq_proj",
    )(h, gkv, gmix, wdkv, glat, wkr, wuk, wuv, wdq, gq, wuq, cos, sin)


ATT_TQ = 512
ATT_TK = 256


def _flash_kernel(q_ref, k_ref, v_ref, o_ref):
    s_len = q_ref.shape[0]
    tq, tk = ATT_TQ, ATT_TK
    row = lax.broadcasted_iota(jnp.int32, (tq, tk), 0)
    col = lax.broadcasted_iota(jnp.int32, (tq, tk), 1)
    lane = lax.broadcasted_iota(jnp.int32, (tq, 2 * V_HEAD), 1)
    nt = (((1,), (1,)), ((), ()))
    for qi in range(s_len // tq):
        q0 = qi * tq
        qs = [q_ref[q0:q0 + tq, hh * HEAD_PAD:(hh + 1) * HEAD_PAD] for hh in range(2)]
        m = [jnp.full((tq, 1), -jnp.inf, F32) for _ in range(2)]
        l = [jnp.zeros((tq, 1), F32) for _ in range(2)]
        acc = [jnp.zeros((tq, 2 * V_HEAD), F32) for _ in range(2)]
        for kj in range((q0 + tq) // tk):
            k0 = kj * tk
            v = v_ref[k0:k0 + tk, :]
            for hh in range(2):
                k = k_ref[k0:k0 + tk, hh * HEAD_PAD:(hh + 1) * HEAD_PAD]
                s = lax.dot_general(qs[hh], k, nt, preferred_element_type=F32)
                if k0 + tk - 1 > q0:
                    s = jnp.where(col + k0 <= row + q0, s, -jnp.inf)
                m_new = jnp.maximum(m[hh], jnp.max(s, axis=-1, keepdims=True))
                alpha = jnp.exp2(m[hh] - m_new)
                p = jnp.exp2(s - m_new)
                l[hh] = alpha * l[hh] + jnp.sum(p, axis=-1, keepdims=True)
                acc[hh] = alpha * acc[hh] + jnp.dot(p.astype(BF16), v, preferred_element_type=F32)
                m[hh] = m_new
        o = jnp.where(lane < V_HEAD, acc[0] / l[0], acc[1] / l[1])
        o_ref[q0:q0 + tq, :] = o.astype(BF16)


def _flash(q, kcat, v, b, s):
    return pl.pallas_call(
        _flash_kernel,
        grid=(b, N_HEADS // 2),
        in_specs=[pl.BlockSpec((s, 2 * HEAD_PAD), lambda i, h: (i, h)),
                  pl.BlockSpec((s, 2 * HEAD_PAD), lambda i, h: (i, h)),
                  pl.BlockSpec((s, 2 * V_HEAD), lambda i, h: (i, h))],
        out_specs=pl.BlockSpec((s, 2 * V_HEAD), lambda i, h: (i, h)),
        out_shape=jax.ShapeDtypeStruct((b * s, N_HEADS * V_HEAD), BF16),
        compiler_params=_cparams(("parallel", "parallel")),
        name="flash_attn",
    )(q, kcat, v)


def _proj_res_kernel(a_ref, w_ref, h_ref, o_ref):
    o_ref[...] = h_ref[...] + jnp.dot(a_ref[...], w_ref[...], preferred_element_type=F32)


def _proj_res(a, w, h, tm):
    t, k = a.shape
    return pl.pallas_call(
        _proj_res_kernel,
        grid=(t // tm,),
        in_specs=[pl.BlockSpec((tm, k), lambda i: (i, 0)), _full(w.shape),
                  pl.BlockSpec((tm, D_MODEL), lambda i: (i, 0))],
        out_specs=pl.BlockSpec((tm, D_MODEL), lambda i: (i, 0)),
        out_shape=jax.ShapeDtypeStruct((t, D_MODEL), F32),
        compiler_params=_cparams(("parallel",)),
        name="proj_residual",
    )(a, w, h)


def _qlat_kernel(q_ref, wukt_ref, o_ref):
    for hd in range(N_HEADS):
        qh = q_ref[:, hd * HEAD_PAD:(hd + 1) * HEAD_PAD]
        o_ref[:, hd * KV_LORA:(hd + 1) * KV_LORA] = jnp.dot(
            qh, wukt_ref[hd], preferred_element_type=F32).astype(BF16)


def _qlat(q, wukt):
    n = q.shape[0]
    return pl.pallas_call(
        _qlat_kernel,
        in_specs=[_full(q.shape), _full(wukt.shape)],
        out_specs=_full((n, N_HEADS * KV_LORA)),
        out_shape=jax.ShapeDtypeStruct((n, N_HEADS * KV_LORA), BF16),
        grid=(1,),
        compiler_params=_cparams(("arbitrary",)),
        name="q_absorb",
    )(q, wukt)


PAGES_PER_CHUNK = 32
CHUNK_SLOTS = 4
CHUNK_AHEAD = 2


def _paged_kernel(pt_ref, ql_ref, qp_ref, cn_ref, kn_ref, lat_hbm, kpt_hbm, o_ref, lbuf, kbuf, sems, *, nb):
    b = pl.program_id(0)
    n_pages = pt_ref.shape[0] // nb
    cp = PAGES_PER_CHUNK
    n_chunks = n_pages // cp
    assert n_chunks == CHUNK_SLOTS
    total = nb * n_chunks

    def page_copies(g, slot, j):
        gg = jnp.where(g < total, g, 0)
        page = pt_ref[gg * cp + j]
        return (pltpu.make_async_copy(lat_hbm.at[page], lbuf.at[slot, pl.ds(j * PAGE_SIZE, PAGE_SIZE)],
                                      sems.at[0, slot]),
                pltpu.make_async_copy(kpt_hbm.at[page], kbuf.at[slot, :, pl.ds(j * PAGE_SIZE, PAGE_SIZE)],
                                      sems.at[1, slot]))

    def start_chunk(g, slot):
        for j in range(cp):
            for c in page_copies(g, slot, j):
                c.start()

    def wait_chunk(g, slot):
        for j in range(cp):
            for c in page_copies(g, slot, j):
                c.wait()

    @pl.when(b == 0)
    def _():
        for a in range(CHUNK_AHEAD):
            start_chunk(a, a)

    ql = ql_ref[...]
    qp = qp_ref[...]
    nt = (((1,), (1,)), ((), ()))
    m = jnp.full((N_HEADS, 1), -jnp.inf, F32)
    l = jnp.zeros((N_HEADS, 1), F32)
    acc = jnp.zeros((N_HEADS, KV_LORA), F32)
    for c in range(n_chunks):
        g = b * n_chunks + c
        wait_chunk(g, c)
        start_chunk(g + CHUNK_AHEAD, (c + CHUNK_AHEAD) % CHUNK_SLOTS)
        kc = lbuf[c].astype(BF16)
        kp = kbuf[c].astype(BF16)
        s = (lax.dot_general(ql, kc, nt, preferred_element_type=F32)
             + jnp.dot(qp, kp, preferred_element_type=F32))
        m_new = jnp.maximum(m, jnp.max(s, axis=-1, keepdims=True))
        alpha = jnp.exp2(m - m_new)
        p = jnp.exp2(s - m_new)
        l = alpha * l + jnp.sum(p, axis=-1, keepdims=True)
        acc = alpha * acc + jnp.dot(p.astype(BF16), kc, preferred_element_type=F32)
        m = m_new
    cn = cn_ref[...].astype(BF16).astype(F32)
    kn = kn_ref[...].astype(BF16).astype(F32)
    s = (jnp.sum(ql.astype(F32) * cn, axis=-1, keepdims=True)
         + jnp.sum(qp.astype(F32) * kn, axis=-1, keepdims=True))
    m_new = jnp.maximum(m, s)
    alpha = jnp.exp2(m - m_new)
    p = jnp.exp2(s - m_new)
    l = alpha * l + p
    acc = alpha * acc + p.astype(BF16).astype(F32) * cn
    o_ref[...] = acc / l

    @pl.when(b == nb - 1)
    def _():
        for a in range(CHUNK_AHEAD):
            wait_chunk(total + a, a)


def _paged_attn(pt_flat, ql, qp, c_new, k_new, cache_latent, cache_kpt):
    n = ql.shape[0]
    rows = PAGES_PER_CHUNK * PAGE_SIZE
    per_seq = lambda w: pl.BlockSpec((None,) + w, lambda i, pt: (i, 0, 0))
    return pl.pallas_call(
        functools.partial(_paged_kernel, nb=n),
        grid_spec=pltpu.PrefetchScalarGridSpec(
            num_scalar_prefetch=1,
            grid=(n,),
            in_specs=[per_seq((N_HEADS, KV_LORA)), per_seq((N_HEADS, QK_ROPE)),
                      per_seq((1, KV_LORA)), per_seq((1, QK_ROPE)),
                      pl.BlockSpec(memory_space=pl.ANY), pl.BlockSpec(memory_space=pl.ANY)],
            out_specs=per_seq((N_HEADS, KV_LORA)),
            scratch_shapes=[pltpu.VMEM((CHUNK_SLOTS, rows, KV_LORA), F32),
                            pltpu.VMEM((CHUNK_SLOTS, QK_ROPE, rows), F32),
                            pltpu.SemaphoreType.DMA((2, CHUNK_SLOTS))],
        ),
        out_shape=jax.ShapeDtypeStruct((n, N_HEADS, KV_LORA), F32),
        compiler_params=_cparams(("arbitrary",)),
        name="paged_attn",
    )(pt_flat, ql, qp, c_new, k_new, cache_latent, cache_kpt)


def _attn_out_kernel(ol_ref, wuv_ref, wo_ref, h_ref, o_ref):
    acc = h_ref[...]
    for hd in range(N_HEADS):
        oh = ol_ref[:, hd * KV_LORA:(hd + 1) * KV_LORA].astype(BF16)
        th = jnp.dot(oh, wuv_ref[hd], preferred_element_type=F32).astype(BF16)
        acc = acc + jnp.dot(th, wo_ref[hd * V_HEAD:(hd + 1) * V_HEAD, :], preferred_element_type=F32)
    o_ref[...] = acc


def _attn_out(ol, wuv_h, wo, h):
    n = h.shape[0]
    return pl.pallas_call(
        _attn_out_kernel,
        grid=(1,),
        in_specs=[_full(ol.shape), _full(wuv_h.shape), _full(wo.shape), _full(h.shape)],
        out_specs=_full(h.shape),
        out_shape=jax.ShapeDtypeStruct((n, D_MODEL), F32),
        compiler_params=_cparams(("arbitrary",)),
        name="attn_out",
    )(ol, wuv_h, wo, h)


def _rope_tables(pos):
    half = QK_ROPE // 2
    inv = ROPE_THETA ** (-jnp.arange(half, dtype=F32) / half)
    ang = pos.astype(F32)[:, None] * inv[None, :]
    cos, sin = jnp.cos(ang), jnp.sin(ang)
    n = pos.shape[0]
    cos_t = jnp.concatenate([jnp.ones((n, QK_NOPE), F32), cos, cos,
                             jnp.ones((n, HEAD_PAD - ROPE_HI), F32)], axis=1)
    sin_t = jnp.concatenate([jnp.zeros((n, QK_NOPE), F32), sin, sin,
                             jnp.zeros((n, HEAD_PAD - ROPE_HI), F32)], axis=1)
    return cos_t, sin_t


def _moe(h, g_ffn, wr, br, w1, w3, w2, layer, tm):
    t = h.shape[0]
    u, code, gate, cnt = _router(h, g_ffn, wr, br, tm)
    counts = cnt[:, 0].astype(jnp.int32)
    padded = (counts + MOE_BM - 1) // MOE_BM * MOE_BM
    pends = jnp.cumsum(padded)
    pstarts = pends - padded
    n_steps = (t * TOP_K + N_EXPERTS * (MOE_BM - 1) + MOE_BM - 1) // MOE_BM
    n_rows = n_steps * MOE_BM
    eid = code >> RANK_BITS
    eiota = jnp.arange(N_EXPERTS, dtype=jnp.int32)
    start_of = jnp.sum(jnp.where(eid[..., None] == eiota, pstarts, 0), axis=-1)
    dest = (start_of + (code & ((1 << RANK_BITS) - 1))).reshape(-1)
    block_start = jnp.arange(n_steps, dtype=jnp.int32) * MOE_BM
    blk_e = jnp.minimum(jnp.sum((pends[None, :] <= block_start[:, None]).astype(jnp.int32), axis=1),
                        N_EXPERTS - 1)
    inv = jnp.full((n_rows,), -1, jnp.int32).at[dest].set(
        jnp.arange(TOP_K * t, dtype=jnp.int32), unique_indices=True)
    inv = jnp.where(inv >= 0, inv, -jnp.cumsum((inv < 0).astype(jnp.int32)))
    yp = _experts(blk_e, inv, u, w1, w3, w2, layer, n_steps, n_rows + MOE_BM)
    return yp, gate.T


def kernel(x_prompt, x_sample, cache_conv, cache_latent, cache_kpe, page_table, p_prompt, p_sample, g_mix, g_ffn, g_ple, g_final, conv_w_in, conv_b_in, conv_w_dw, conv_b_dw, conv_ln_g, conv_ln_b, conv_w_out, conv_b_out, g_kv_in, w_dkv, g_kv_lat, w_kr, w_uk, w_uv, w_dq, g_q_lat, w_uq, w_o, w_group, b_group, w_inner, b_inner, w_exp_gate, w_exp_up, w_exp_down, w_ple_gate, b_ple_gate, w_ple_proj):
    bp, sp, d = x_prompt.shape
    nd = x_sample.shape[0]
    tp = bp * sp
    row = lambda a: a.reshape(1, -1)

    w_in_b = conv_w_in[0].astype(BF16)
    w_out_b = conv_w_out[0].astype(BF16)
    wdw_pad = jnp.concatenate([conv_w_dw[0], jnp.zeros((HALO - CONV_W, d), F32)], axis=0)
    wr, br = [], []
    for i in range(2):
        wr.append(jnp.concatenate([w_group[i].T, jnp.zeros((8 - N_GROUPS, d), F32),
                                   w_inner[i].reshape(d, N_EXPERTS).T], axis=0))
        br.append(jnp.concatenate([b_group[i], jnp.zeros((8 - N_GROUPS,), F32),
                                   b_inner[i].reshape(N_EXPERTS)]).reshape(ROUTER_ROWS, 1))
    w_dkv_b = w_dkv.astype(BF16)
    w_kr_pad = jnp.zeros((d, HEAD_PAD), F32).at[:, ROPE_LO:ROPE_HI].set(w_kr).astype(BF16)
    w_uk_pad = jnp.pad(w_uk, ((0, 0), (0, 0), (0, HEAD_PAD - QK_NOPE))).reshape(KV_LORA, -1).astype(BF16)
    w_uv_b = w_uv.reshape(KV_LORA, -1).astype(BF16)
    w_dq_b = w_dq[0].astype(BF16)
    w_uq_pad = jnp.pad(w_uq[0].reshape(Q_LORA, N_HEADS, QK_NOPE + QK_ROPE),
                       ((0, 0), (0, 0), (0, HEAD_PAD - ROPE_HI))).reshape(Q_LORA, -1).astype(BF16)
    w_o_b = w_o[0].astype(BF16)
    w_ukt_pad = jnp.pad(jnp.transpose(w_uk, (1, 2, 0)),
                        ((0, 0), (0, HEAD_PAD - QK_NOPE), (0, 0))).astype(BF16)
    w_uv_h = jnp.transpose(w_uv, (1, 0, 2)).astype(BF16)
    wg_b = [w_ple_gate[i].astype(BF16) for i in range(2)]
    wp_b = [w_ple_proj[i].astype(BF16) for i in range(2)]

    def moe_ple(h, layer, p, tm, final):
        yp, gate_t = _moe(h, row(g_ffn[layer]), wr[layer], br[layer],
                          w_exp_gate, w_exp_up, w_exp_down, layer, tm)
        return _combine(yp, h, gate_t, p, row(g_ple[layer]), wg_b[layer], row(b_ple_gate[layer]),
                        wp_b[layer], row(g_final), tm, final)

    def kvq(h, cos_t, sin_t, tm, n_pos_blocks):
        return _kvq(h, row(g_kv_in), row(g_mix[1]), w_dkv_b, row(g_kv_lat), w_kr_pad, w_uk_pad, w_uv_b,
                    w_dq_b, row(g_q_lat[0]), w_uq_pad, cos_t, sin_t, tm, n_pos_blocks)

    xp = x_prompt.reshape(tp, d)
    z = _conv_in(xp, row(g_mix[0]), w_in_b, row(conv_b_in[0]), 512)
    z3 = z.reshape(bp, sp, d)
    conv_prompt = z3[:, sp - (CONV_W - 1):, :][None]
    ctx0 = jnp.zeros((bp, HALO, d), F32)
    h = _conv_mix(z3, ctx0, x_prompt, wdw_pad, row(conv_b_dw[0]), row(conv_ln_g[0]), row(conv_ln_b[0]),
                  w_out_b, row(conv_b_out[0])).reshape(tp, d)
    h = moe_ple(h, 0, p_prompt[0].reshape(tp, PLE_DIM), 512, False)
    cos_p, sin_p = _rope_tables(jnp.arange(sp, dtype=jnp.int32))
    lat_p, kpe_p, kcat, v, q = kvq(h, cos_p, sin_p, 512, sp // 512)
    o = _flash(q, kcat, v, bp, sp)
    h = _proj_res(o, w_o_b, h, 512)
    y_prompt = moe_ple(h, 1, p_prompt[1].reshape(tp, PLE_DIM), 512, True).reshape(bp, sp, d)
    latent_prompt = lat_p.reshape(bp, sp, KV_LORA)
    kpe_prompt = kpe_p[:, ROPE_LO:ROPE_HI].reshape(bp, sp, QK_ROPE)

    xs = x_sample.reshape(nd, d)
    zs = _conv_in(xs, row(g_mix[0]), w_in_b, row(conv_b_in[0]), nd)
    ctx_t = jnp.transpose(cache_conv[0], (1, 0, 2))
    conv_sample = jnp.transpose(jnp.concatenate([ctx_t[1:], zs[None]], axis=0), (1, 0, 2))[None]
    hs = _conv_step(ctx_t, conv_w_dw[0][:, None, :], zs, xs,
                    row(conv_b_dw[0]), row(conv_ln_g[0]), row(conv_ln_b[0]), w_out_b, row(conv_b_out[0]))
    hs = moe_ple(hs, 0, p_sample[0].reshape(nd, PLE_DIM), nd, False)
    n_pages = page_table.shape[1]
    pos_s = jnp.full((nd,), n_pages * PAGE_SIZE, jnp.int32)
    cos_s, sin_s = _rope_tables(pos_s)
    lat_s, kpe_s, _, _, qs = kvq(hs, cos_s, sin_s, nd, 1)
    ql = _qlat(qs, w_ukt_pad).reshape(nd, N_HEADS, KV_LORA)
    qp = qs.reshape(nd, N_HEADS, HEAD_PAD)[:, :, ROPE_LO:ROPE_HI]
    kpe_new = kpe_s[:, ROPE_LO:ROPE_HI]
    o_lat = _paged_attn(page_table.reshape(-1), ql, qp, lat_s.reshape(nd, 1, KV_LORA),
                        kpe_new.reshape(nd, 1, QK_ROPE), cache_latent, jnp.transpose(cache_kpe, (0, 2, 1)))
    hs = _attn_out(o_lat.reshape(nd, N_HEADS * KV_LORA), w_uv_h, w_o_b, hs)
    y_sample = moe_ple(hs, 1, p_sample[1].reshape(nd, PLE_DIM), nd, True).reshape(nd, 1, d)
    latent_sample = lat_s.reshape(nd, 1, KV_LORA)
    kpe_sample = kpe_new.reshape(nd, 1, QK_ROPE)

    return (y_prompt, y_sample, conv_prompt, conv_sample, latent_prompt, kpe_prompt, latent_sample, kpe_sample)
```

```python
import functools
import math

import jax
import jax.numpy as jnp
from jax import lax
from jax.experimental import pallas as pl
from jax.experimental.pallas import tpu as pltpu

F32 = jnp.float32
BF16 = jnp.bfloat16

D_MODEL = 1024
PLE_DIM = 256
CONV_W = 31
N_HEADS = 16
QK_NOPE = 64
QK_ROPE = 32
V_HEAD = 64
Q_LORA = 384
KV_LORA = 256
ROPE_THETA = 10000.0
SCALE = 1.0 / math.sqrt(QK_NOPE + QK_ROPE)
LOG2E = 1.4426950408889634
N_GROUPS = 4
EXPERTS_PER_GROUP = 8
N_EXPERTS = N_GROUPS * EXPERTS_PER_GROUP
TOP_K = 2
EXPERT_FF = 512
PAGE_SIZE = 128
EPS = 1e-6

HEAD_PAD = 128
ROPE_LO = QK_NOPE
ROPE_MID = QK_NOPE + QK_ROPE // 2
ROPE_HI = QK_NOPE + QK_ROPE
HALO = 32
MOE_HALF = 128
MOE_BM = 2 * MOE_HALF
ROUTER_ROWS = 40
RANK_BITS = 20
VMEM_LIMIT = 56 * 1024 * 1024


def _cparams(sem):
    return pltpu.CompilerParams(dimension_semantics=sem, vmem_limit_bytes=VMEM_LIMIT)


def _rms(x, g):
    return x * lax.rsqrt(jnp.mean(x * x, axis=-1, keepdims=True) + EPS) * g


def _full(shape):
    n = len(shape)
    return pl.BlockSpec(shape, lambda *_: (0,) * n)


LANES = 128
SUBLANES = 8
TOK_ROWS = D_MODEL // LANES


def _load_token_tiles(ref, n):
    return jnp.concatenate([ref[pl.ds(s, n, stride=TOK_ROWS), :] for s in range(TOK_ROWS)], axis=1)


def _store_token_tiles(ref, x):
    n = x.shape[0]
    for s in range(TOK_ROWS):
        ref[pl.ds(s, n, stride=TOK_ROWS), :] = x[:, s * LANES:(s + 1) * LANES]


def _conv_in_kernel(x_ref, g_ref, w_ref, b_ref, z_ref):
    u = _rms(x_ref[...], g_ref[...])
    a = jnp.dot(u.astype(BF16), w_ref[...], preferred_element_type=F32) + b_ref[...]
    z_ref[...] = a[:, :D_MODEL] * jax.nn.sigmoid(a[:, D_MODEL:])


def _conv_in(x, g, w_in, b_in, tm):
    t = x.shape[0]
    return pl.pallas_call(
        _conv_in_kernel,
        grid=(t // tm,),
        in_specs=[pl.BlockSpec((tm, D_MODEL), lambda i: (i, 0)),
                  _full((1, D_MODEL)), _full((D_MODEL, 2 * D_MODEL)), _full((1, 2 * D_MODEL))],
        out_specs=pl.BlockSpec((tm, D_MODEL), lambda i: (i, 0)),
        out_shape=jax.ShapeDtypeStruct((t, D_MODEL), F32),
        compiler_params=_cparams(("parallel",)),
        name="conv_in",
    )(x, g, w_in, b_in)


def _ln_silu_out(y, lng, lnb, wout, bout):
    mu = jnp.mean(y, axis=-1, keepdims=True)
    yc = y - mu
    var = jnp.mean(yc * yc, axis=-1, keepdims=True)
    n = yc * lax.rsqrt(var + EPS) * lng + lnb
    act = n * jax.nn.sigmoid(n)
    return jnp.dot(act.astype(BF16), wout, preferred_element_type=F32) + bout


CONV_TS = 256
CONV_RC = 128
CONV_LC = 256


def _conv_mix_kernel(z_ref, halo_ref, ctx_ref, x_ref, wdw_ref, bdw_ref, lng_ref, lnb_ref,
                     wout_ref, bout_ref, h_ref, buf_ref, y_ref):
    s = pl.program_id(1)

    @pl.when(s == 0)
    def _():
        buf_ref[0, 0:HALO, :] = ctx_ref[...]

    @pl.when(s > 0)
    def _():
        buf_ref[0, 0:HALO, :] = halo_ref[...]

    buf_ref[0, HALO:, :] = z_ref[...]
    shifted_rows = CONV_TS + HALO - SUBLANES
    for sh in range(1, SUBLANES):
        buf_ref[sh, 0:shifted_rows, :] = buf_ref[0, sh:sh + shifted_rows, :]
    first = HALO - (CONV_W - 1)
    for lc in range(D_MODEL // CONV_LC):
        lanes = slice(lc * CONV_LC, (lc + 1) * CONV_LC)
        for rc in range(CONV_TS // CONV_RC):
            r0 = rc * CONV_RC
            acc = jnp.broadcast_to(bdw_ref[:, lanes], (CONV_RC, CONV_LC))
            for k in range(CONV_W):
                sh = (first + k) % SUBLANES
                a0 = r0 + first + k - sh
                acc = acc + wdw_ref[k:k + 1, lanes] * buf_ref[sh, a0:a0 + CONV_RC, lanes]
            y_ref[r0:r0 + CONV_RC, lanes] = acc
    out = _ln_silu_out(y_ref[...], lng_ref[...], lnb_ref[...], wout_ref[...], bout_ref[...])
    h_ref[...] = x_ref[...] + out


def _conv_mix(z3, ctx, x3, wdw, bdw, lng, lnb, wout, bout):
    b, s, d = z3.shape
    ts = CONV_TS
    hb = ts // HALO
    return pl.pallas_call(
        _conv_mix_kernel,
        grid=(b, s // ts),
        in_specs=[pl.BlockSpec((None, ts, d), lambda i, j: (i, j, 0)),
                  pl.BlockSpec((None, HALO, d), lambda i, j: (i, jnp.maximum(j * hb - 1, 0), 0)),
                  pl.BlockSpec((None, HALO, d), lambda i, j: (i, 0, 0)),
                  pl.BlockSpec((None, ts, d), lambda i, j: (i, j, 0)),
                  _full((HALO, d)), _full((1, d)), _full((1, d)), _full((1, d)),
                  _full((d, d)), _full((1, d))],
        out_specs=pl.BlockSpec((None, ts, d), lambda i, j: (i, j, 0)),
        out_shape=jax.ShapeDtypeStruct((b, s, d), F32),
        scratch_shapes=[pltpu.VMEM((SUBLANES, ts + HALO, d), F32), pltpu.VMEM((ts, d), F32)],
        compiler_params=_cparams(("parallel", "arbitrary")),
        name="conv_mix",
    )(z3, z3, ctx, x3, wdw, bdw, lng, lnb, wout, bout)


def _conv_step_kernel(ctx_ref, w_ref, z_ref, x_ref, bdw_ref, lng_ref, lnb_ref, wout_ref, bout_ref,
                      h_ref, acc_ref):
    k = pl.program_id(0)

    @pl.when(k == 0)
    def _():
        acc_ref[...] = jnp.broadcast_to(bdw_ref[...], acc_ref.shape)

    @pl.when(k < CONV_W - 1)
    def _():
        acc_ref[...] += w_ref[...] * ctx_ref[...]

    @pl.when(k == CONV_W - 1)
    def _():
        y = acc_ref[...] + w_ref[...] * z_ref[...]
        out = _ln_silu_out(y, lng_ref[...], lnb_ref[...], wout_ref[...], bout_ref[...])
        h_ref[...] = x_ref[...] + out


def _conv_step(ctx_t, wdw3, z, x, bdw, lng, lnb, wout, bout):
    n, d = z.shape
    last = CONV_W - 2
    return pl.pallas_call(
        _conv_step_kernel,
        grid=(CONV_W,),
        in_specs=[pl.BlockSpec((None, n, d), lambda k: (jnp.minimum(k, last), 0, 0)),
                  pl.BlockSpec((None, 1, d), lambda k: (k, 0, 0)),
                  _full((n, d)), _full((n, d)), _full((1, d)), _full((1, d)), _full((1, d)),
                  _full((d, d)), _full((1, d))],
        out_specs=_full((n, d)),
        out_shape=jax.ShapeDtypeStruct((n, d), F32),
        scratch_shapes=[pltpu.VMEM((n, d), F32)],
        compiler_params=_cparams(("arbitrary",)),
        name="conv_step",
    )(ctx_t, wdw3, z, x, bdw, lng, lnb, wout, bout)


def _router_kernel(h_ref, g_ref, wr_ref, br_ref, u_ref, code_ref, gate_ref, cnt_ref, carry_ref):
    i = pl.program_id(0)
    tm = h_ref.shape[0]

    @pl.when(i == 0)
    def _():
        carry_ref[...] = jnp.zeros_like(carry_ref)

    u = _rms(h_ref[...], g_ref[...])
    _store_token_tiles(u_ref, u)
    logits = lax.dot_general(wr_ref[...], u, (((1,), (1,)), ((), ())),
                             precision=lax.Precision.HIGHEST, preferred_element_type=F32) + br_ref[...]
    gl = logits[0:N_GROUPS]
    il = logits[8:8 + N_EXPERTS]
    gmax = jnp.max(gl, axis=0, keepdims=True)
    giota = lax.broadcasted_iota(jnp.int32, gl.shape, 0)
    gsel = jnp.min(jnp.where(gl == gmax, giota, N_GROUPS), axis=0, keepdims=True)
    gw = 1.0 / jnp.sum(jnp.exp(gl - gmax), axis=0, keepdims=True)
    eiota = lax.broadcasted_iota(jnp.int32, il.shape, 0)
    ml = jnp.where((eiota >> 3) == gsel, il, -jnp.inf)
    v1 = jnp.max(ml, axis=0, keepdims=True)
    i1 = jnp.min(jnp.where(ml == v1, eiota, N_EXPERTS), axis=0, keepdims=True)
    ml2 = jnp.where(eiota == i1, -jnp.inf, ml)
    v2 = jnp.max(ml2, axis=0, keepdims=True)
    i2 = jnp.min(jnp.where(ml2 == v2, eiota, N_EXPERTS), axis=0, keepdims=True)
    e2 = jnp.exp(v2 - v1)
    p1 = 1.0 / (1.0 + e2)
    gate_ref[0:1, :] = p1 * gw
    gate_ref[1:2, :] = e2 * p1 * gw
    hit1 = eiota == i1
    hit2 = eiota == i2
    onehot = jnp.where(hit1 | hit2, 1.0, 0.0)
    r = lax.broadcasted_iota(jnp.int32, (tm, tm), 0)
    c = lax.broadcasted_iota(jnp.int32, (tm, tm), 1)
    before = jnp.where(r < c, 1.0, 0.0).astype(BF16)
    cum = jnp.dot(onehot.astype(BF16), before, preferred_element_type=F32) + carry_ref[:, 0:1]
    rank1 = jnp.sum(jnp.where(hit1, cum, 0.0), axis=0, keepdims=True).astype(jnp.int32)
    rank2 = jnp.sum(jnp.where(hit2, cum, 0.0), axis=0, keepdims=True).astype(jnp.int32)
    code_ref[0:1, :] = (i1 << RANK_BITS) | rank1
    code_ref[1:2, :] = (i2 << RANK_BITS) | rank2
    carry_ref[...] += jnp.sum(onehot, axis=1, keepdims=True)
    cnt_ref[...] = carry_ref[...]


def _router(h, g, wr, br, tm):
    t = h.shape[0]
    row2 = pl.BlockSpec((TOP_K, tm), lambda i: (0, i))
    return pl.pallas_call(
        _router_kernel,
        grid=(t // tm,),
        in_specs=[pl.BlockSpec((tm, D_MODEL), lambda i: (i, 0)),
                  _full((1, D_MODEL)), _full((ROUTER_ROWS, D_MODEL)), _full((ROUTER_ROWS, 1))],
        out_specs=[pl.BlockSpec((tm * TOK_ROWS, LANES), lambda i: (i, 0)), row2, row2,
                   _full((N_EXPERTS, 128))],
        out_shape=[jax.ShapeDtypeStruct((t * TOK_ROWS, LANES), F32),
                   jax.ShapeDtypeStruct((TOP_K, t), jnp.int32),
                   jax.ShapeDtypeStruct((TOP_K, t), F32),
                   jax.ShapeDtypeStruct((N_EXPERTS, 128), F32)],
        scratch_shapes=[pltpu.VMEM((N_EXPERTS, 128), F32)],
        compiler_params=_cparams(("arbitrary",)),
        name="router",
    )(h, g, wr, br)


def _expert_kernel(blk_ref, inv_ref, u_hbm, w1_ref, w3_ref, w2_ref, yp_hbm, xbuf, ybuf, w1b, w3b, w2b,
                   gsem, ssem, *, n_tok, n_steps):
    j = pl.program_id(0)
    trash0 = TOP_K * n_tok
    prime0 = n_steps * MOE_BM

    half_rows = MOE_HALF * TOK_ROWS

    def tile_of(t):
        return pl.ds(pl.multiple_of(t * TOK_ROWS, TOK_ROWS), TOK_ROWS)

    def gather_start(step, half):
        base = step * MOE_BM + half * MOE_HALF
        for r in range(MOE_HALF):
            p = inv_ref[base + r]
            tok = jnp.where(p >= 0, p, base + r) & (n_tok - 1)
            pltpu.make_async_copy(u_hbm.at[tile_of(tok)], xbuf.at[half, tile_of(r)], gsem.at[half]).start()

    def gather_wait(half):
        pltpu.make_async_copy(u_hbm.at[pl.ds(0, half_rows)], xbuf.at[half], gsem.at[half]).wait()

    def scatter_start(step, half, to_prime=None):
        base = step * MOE_BM + half * MOE_HALF
        for r in range(MOE_HALF):
            p = inv_ref[base + r]
            dst = jnp.where(p >= 0, p, trash0 - 1 - p)
            if to_prime is not None:
                dst = jnp.where(to_prime, prime0 + half * MOE_HALF + r, dst)
            pltpu.make_async_copy(ybuf.at[half, tile_of(r)], yp_hbm.at[tile_of(dst)], ssem.at[half]).start()

    def scatter_wait(half):
        pltpu.make_async_copy(ybuf.at[half], yp_hbm.at[pl.ds(0, half_rows)], ssem.at[half]).wait()

    @pl.when(j == 0)
    def _():
        ybuf[...] = jnp.zeros_like(ybuf)
        pltpu.make_async_copy(ybuf.at[0], yp_hbm.at[pl.ds(prime0 * TOK_ROWS, half_rows)], ssem.at[0]).start()
        gather_start(0, 0)

    e = blk_ref[j]
    prev = blk_ref[jnp.maximum(j - 1, 0)]

    @pl.when((j == 0) | (e != prev))
    def _():
        w1b[...] = w1_ref[...].astype(BF16)
        w3b[...] = w3_ref[...].astype(BF16)
        w2b[...] = w2_ref[...].astype(BF16)

    def compute(half):
        x = _load_token_tiles(xbuf.at[half], MOE_HALF).astype(BF16)
        a = jnp.dot(x, w1b[...], preferred_element_type=F32)
        b = jnp.dot(x, w3b[...], preferred_element_type=F32)
        hmid = (a * jax.nn.sigmoid(a)) * b
        _store_token_tiles(ybuf.at[half], jnp.dot(hmid.astype(BF16), w2b[...], preferred_element_type=F32))

    gather_wait(0)
    scatter_wait(0)
    gather_start(j, 1)
    scatter_start(jnp.maximum(j - 1, 0), 1, to_prime=(j == 0))
    compute(0)
    gather_wait(1)
    scatter_wait(1)
    scatter_start(j, 0)
    gather_start(jnp.minimum(j + 1, n_steps - 1), 0)
    compute(1)

    @pl.when(j == n_steps - 1)
    def _():
        scatter_start(j, 1)
        gather_wait(0)
        scatter_wait(0)
        scatter_wait(1)


def _experts(blk_e, inv, u, w1, w3, w2, layer, n_steps, n_out_rows):
    n_tok = u.shape[0] // TOK_ROWS
    assert n_tok & (n_tok - 1) == 0
    wspec = lambda s: pl.BlockSpec((None, None) + s, lambda j, b, i: (layer, b[j], 0, 0))
    return pl.pallas_call(
        functools.partial(_expert_kernel, n_tok=n_tok, n_steps=n_steps),
        grid_spec=pltpu.PrefetchScalarGridSpec(
            num_scalar_prefetch=2,
            grid=(n_steps,),
            in_specs=[pl.BlockSpec(memory_space=pl.ANY),
                      wspec((D_MODEL, EXPERT_FF)), wspec((D_MODEL, EXPERT_FF)), wspec((EXPERT_FF, D_MODEL))],
            out_specs=pl.BlockSpec(memory_space=pl.ANY),
            scratch_shapes=[pltpu.VMEM((2, MOE_HALF * TOK_ROWS, LANES), F32),
                            pltpu.VMEM((2, MOE_HALF * TOK_ROWS, LANES), F32),
                            pltpu.VMEM((D_MODEL, EXPERT_FF), BF16),
                            pltpu.VMEM((D_MODEL, EXPERT_FF), BF16),
                            pltpu.VMEM((EXPERT_FF, D_MODEL), BF16),
                            pltpu.SemaphoreType.DMA((2,)),
                            pltpu.SemaphoreType.DMA((2,))],
        ),
        out_shape=jax.ShapeDtypeStruct((n_out_rows * TOK_ROWS, LANES), F32),
        compiler_params=_cparams(("arbitrary",)),
        name="moe_experts",
    )(blk_e, inv, u, w1, w3, w2)


def _combine_kernel(y0_ref, y1_ref, h_ref, gate_ref, p_ref, g_ref, wg_ref, bg_ref, wp_ref, gf_ref,
                    o_ref, *, final):
    gate = gate_ref[...]
    tm = h_ref.shape[0]
    y0 = _load_token_tiles(y0_ref, tm)
    y1 = _load_token_tiles(y1_ref, tm)
    h = h_ref[...] + y0 * gate[:, 0:1] + y1 * gate[:, 1:2]
    u = _rms(h, g_ref[...])
    sg = jax.nn.sigmoid(jnp.dot(u.astype(BF16), wg_ref[...], preferred_element_type=F32) + bg_ref[...])
    pp = jnp.dot(p_ref[...].astype(BF16), wp_ref[...], preferred_element_type=F32)
    h = h + sg * pp
    if final:
        h = _rms(h, gf_ref[...])
    o_ref[...] = h


def _combine(yp, h, gate_t, p, g, wg, bg, wp, gf, tm, final):
    t = h.shape[0]
    nt = t // tm
    return pl.pallas_call(
        functools.partial(_combine_kernel, final=final),
        grid=(nt,),
        in_specs=[pl.BlockSpec((tm * TOK_ROWS, LANES), lambda i: (i, 0)),
                  pl.BlockSpec((tm * TOK_ROWS, LANES), lambda i: (nt + i, 0)),
                  pl.BlockSpec((tm, D_MODEL), lambda i: (i, 0)),
                  pl.BlockSpec((tm, TOP_K), lambda i: (i, 0)),
                  pl.BlockSpec((tm, PLE_DIM), lambda i: (i, 0)),
                  _full((1, D_MODEL)), _full((D_MODEL, D_MODEL)), _full((1, D_MODEL)),
                  _full((PLE_DIM, D_MODEL)), _full((1, D_MODEL))],
        out_specs=pl.BlockSpec((tm, D_MODEL), lambda i: (i, 0)),
        out_shape=jax.ShapeDtypeStruct((t, D_MODEL), F32),
        compiler_params=_cparams(("parallel",)),
        name="moe_combine_ple",
    )(yp, yp, h, gate_t, p, g, wg, bg, wp, gf)


def _rope_group(x, cos, sin, lane):
    half = QK_ROPE // 2
    rot = jnp.where(lane < ROPE_MID, -pltpu.roll(x, HEAD_PAD - half, 1), pltpu.roll(x, half, 1))
    return x * cos + rot * sin


def _kvq_kernel(h_ref, gkv_ref, gmix_ref, wdkv_ref, glat_ref, wkr_ref, wuk_ref, wuv_ref,
                wdq_ref, gq_ref, wuq_ref, cos_ref, sin_ref,
                lat_ref, kpe_ref, kcat_ref, v_ref, q_ref):
    h = h_ref[...]
    tm = h.shape[0]
    cos = cos_ref[...]
    sin = sin_ref[...]
    lane = lax.broadcasted_iota(jnp.int32, (tm, HEAD_PAD), 1)
    ukv = _rms(h, gkv_ref[...]).astype(BF16)
    c = _rms(jnp.dot(ukv, wdkv_ref[...], preferred_element_type=F32), glat_ref[...])
    lat_ref[...] = c
    kpe = _rope_group(jnp.dot(ukv, wkr_ref[...], preferred_element_type=F32), cos, sin, lane)
    kpe_ref[...] = kpe
    cb = c.astype(BF16)
    v_ref[...] = jnp.dot(cb, wuv_ref[...], preferred_element_type=F32).astype(BF16)
    kn = jnp.dot(cb, wuk_ref[...], preferred_element_type=F32)
    umix = _rms(h, gmix_ref[...]).astype(BF16)
    cq = _rms(jnp.dot(umix, wdq_ref[...], preferred_element_type=F32), gq_ref[...])
    q = jnp.dot(cq.astype(BF16), wuq_ref[...], preferred_element_type=F32)
    for hd in range(N_HEADS):
        lanes = slice(hd * HEAD_PAD, (hd + 1) * HEAD_PAD)
        kcat_ref[:, lanes] = (kn[:, lanes] + kpe).astype(BF16)
        q_ref[:, lanes] = (_rope_group(q[:, lanes], cos, sin, lane) * (SCALE * LOG2E)).astype(BF16)


def _kvq(h, gkv, gmix, wdkv, glat, wkr, wuk, wuv, wdq, gq, wuq, cos, sin, tm, n_pos_blocks):
    t = h.shape[0]
    row = lambda w: pl.BlockSpec((tm, w), lambda i: (i, 0))
    return pl.pallas_call(
        _kvq_kernel,
        grid=(t // tm,),
        in_specs=[row(D_MODEL), _full((1, D_MODEL)), _full((1, D_MODEL)),
                  _full((D_MODEL, KV_LORA)), _full((1, KV_LORA)), _full((D_MODEL, HEAD_PAD)),
                  _full((KV_LORA, N_HEADS * HEAD_PAD)), _full((KV_LORA, N_HEADS * V_HEAD)),
                  _full((D_MODEL, Q_LORA)), _full((1, Q_LORA)), _full((Q_LORA, N_HEADS * HEAD_PAD)),
                  pl.BlockSpec((tm, HEAD_PAD), lambda i: (i % n_pos_blocks, 0)),
                  pl.BlockSpec((tm, HEAD_PAD), lambda i: (i % n_pos_blocks, 0))],
        out_specs=[row(KV_LORA), row(HEAD_PAD), row(N_HEADS * HEAD_PAD), row(N_HEADS * V_HEAD),
                   row(N_HEADS * HEAD_PAD)],
        out_shape=[jax.ShapeDtypeStruct((t, KV_LORA), F32),
                   jax.ShapeDtypeStruct((t, HEAD_PAD), F32),
                   jax.ShapeDtypeStruct((t, N_HEADS * HEAD_PAD), BF16),
                   jax.ShapeDtypeStruct((t, N_HEADS * V_HEAD), BF16),
                   jax.ShapeDtypeStruct((t, N_HEADS * HEAD_PAD), BF16)],
        compiler_params=_cparams(("parallel",)),
        name="kvq_proj",
    )(h, gkv, gmix, wdkv, glat, wkr, wuk, wuv, wdq, gq, wuq, cos, sin)


ATT_TQ = 512
ATT_TK = 256


def _flash_kernel(q_ref, k_ref, v_ref, o_ref):
    s_len = q_ref.shape[0]
    tq, tk = ATT_TQ, ATT_TK
    row = lax.broadcasted_iota(jnp.int32, (tq, tk), 0)
    col = lax.broadcasted_iota(jnp.int32, (tq, tk), 1)
    lane = lax.broadcasted_iota(jnp.int32, (tq, 2 * V_HEAD), 1)
    nt = (((1,), (1,)), ((), ()))
    for qi in range(s_len // tq):
        q0 = qi * tq
        qs = [q_ref[q0:q0 + tq, hh * HEAD_PAD:(hh + 1) * HEAD_PAD] for hh in range(2)]
        m = [jnp.full((tq, 1), -jnp.inf, F32) for _ in range(2)]
        l = [jnp.zeros((tq, 1), F32) for _ in range(2)]
        acc = [jnp.zeros((tq, 2 * V_HEAD), F32) for _ in range(2)]
        for kj in range((q0 + tq) // tk):
            k0 = kj * tk
            v = v_ref[k0:k0 + tk, :]
            for hh in range(2):
                k = k_ref[k0:k0 + tk, hh * HEAD_PAD:(hh + 1) * HEAD_PAD]
                s = lax.dot_general(qs[hh], k, nt, preferred_element_type=F32)
                if k0 + tk - 1 > q0:
                    s = jnp.where(col + k0 <= row + q0, s, -jnp.inf)
                m_new = jnp.maximum(m[hh], jnp.max(s, axis=-1, keepdims=True))
                alpha = jnp.exp2(m[hh] - m_new)
                p = jnp.exp2(s - m_new)
                l[hh] = alpha * l[hh] + jnp.sum(p, axis=-1, keepdims=True)
                acc[hh] = alpha * acc[hh] + jnp.dot(p.astype(BF16), v, preferred_element_type=F32)
                m[hh] = m_new
        o = jnp.where(lane < V_HEAD, acc[0] / l[0], acc[1] / l[1])
        o_ref[q0:q0 + tq, :] = o.astype(BF16)


def _flash(q, kcat, v, b, s):
    return pl.pallas_call(
        _flash_kernel,
        grid=(b, N_HEADS // 2),
        in_specs=[pl.BlockSpec((s, 2 * HEAD_PAD), lambda i, h: (i, h)),
                  pl.BlockSpec((s, 2 * HEAD_PAD), lambda i, h: (i, h)),
                  pl.BlockSpec((s, 2 * V_HEAD), lambda i, h: (i, h))],
        out_specs=pl.BlockSpec((s, 2 * V_HEAD), lambda i, h: (i, h)),
        out_shape=jax.ShapeDtypeStruct((b * s, N_HEADS * V_HEAD), BF16),
        compiler_params=_cparams(("parallel", "parallel")),
        name="flash_attn",
    )(q, kcat, v)


def _proj_res_kernel(a_ref, w_ref, h_ref, o_ref):
    o_ref[...] = h_ref[...] + jnp.dot(a_ref[...], w_ref[...], preferred_element_type=F32)


def _proj_res(a, w, h, tm):
    t, k = a.shape
    return pl.pallas_call(
        _proj_res_kernel,
        grid=(t // tm,),
        in_specs=[pl.BlockSpec((tm, k), lambda i: (i, 0)), _full(w.shape),
                  pl.BlockSpec((tm, D_MODEL), lambda i: (i, 0))],
        out_specs=pl.BlockSpec((tm, D_MODEL), lambda i: (i, 0)),
        out_shape=jax.ShapeDtypeStruct((t, D_MODEL), F32),
        compiler_params=_cparams(("parallel",)),
        name="proj_residual",
    )(a, w, h)


def _qlat_kernel(q_ref, wukt_ref, o_ref):
    for hd in range(N_HEADS):
        qh = q_ref[:, hd * HEAD_PAD:(hd + 1) * HEAD_PAD]
        o_ref[:, hd * KV_LORA:(hd + 1) * KV_LORA] = jnp.dot(
            qh, wukt_ref[hd], preferred_element_type=F32).astype(BF16)


def _qlat(q, wukt):
    n = q.shape[0]
    return pl.pallas_call(
        _qlat_kernel,
        in_specs=[_full(q.shape), _full(wukt.shape)],
        out_specs=_full((n, N_HEADS * KV_LORA)),
        out_shape=jax.ShapeDtypeStruct((n, N_HEADS * KV_LORA), BF16),
        grid=(1,),
        compiler_params=_cparams(("arbitrary",)),
        name="q_absorb",
    )(q, wukt)


PAGES_PER_CHUNK = 32
CHUNK_SLOTS = 4
CHUNK_AHEAD = 2


def _paged_kernel(pt_ref, ql_ref, qp_ref, cn_ref, kn_ref, lat_hbm, kpt_hbm, o_ref, lbuf, kbuf, sems, *, nb):
    b = pl.program_id(0)
    n_pages = pt_ref.shape[0] // nb
    cp = PAGES_PER_CHUNK
    n_chunks = n_pages // cp
    assert n_chunks == CHUNK_SLOTS
    total = nb * n_chunks

    def page_copies(g, slot, j):
        gg = jnp.where(g < total, g, 0)
        page = pt_ref[gg * cp + j]
        return (pltpu.make_async_copy(lat_hbm.at[page], lbuf.at[slot, pl.ds(j * PAGE_SIZE, PAGE_SIZE)],
                                      sems.at[0, slot]),
                pltpu.make_async_copy(kpt_hbm.at[page], kbuf.at[slot, :, pl.ds(j * PAGE_SIZE, PAGE_SIZE)],
                                      sems.at[1, slot]))

    def start_chunk(g, slot):
        for j in range(cp):
            for c in page_copies(g, slot, j):
                c.start()

    def wait_chunk(g, slot):
        for j in range(cp):
            for c in page_copies(g, slot, j):
                c.wait()

    @pl.when(b == 0)
    def _():
        for a in range(CHUNK_AHEAD):
            start_chunk(a, a)

    ql = ql_ref[...]
    qp = qp_ref[...]
    nt = (((1,), (1,)), ((), ()))
    m = jnp.full((N_HEADS, 1), -jnp.inf, F32)
    l = jnp.zeros((N_HEADS, 1), F32)
    acc = jnp.zeros((N_HEADS, KV_LORA), F32)
    for c in range(n_chunks):
        g = b * n_chunks + c
        if c % CHUNK_AHEAD == 0:
            for a in range(CHUNK_AHEAD):
                start_chunk(g + CHUNK_AHEAD + a, (c + CHUNK_AHEAD + a) % CHUNK_SLOTS)
            for a in range(CHUNK_AHEAD):
                wait_chunk(g + a, c + a)
        kc = lbuf[c].astype(BF16)
        kp = kbuf[c].astype(BF16)
        s = (lax.dot_general(ql, kc, nt, preferred_element_type=F32)
             + jnp.dot(qp, kp, preferred_element_type=F32))
        m_new = jnp.maximum(m, jnp.max(s, axis=-1, keepdims=True))
        alpha = jnp.exp2(m - m_new)
        p = jnp.exp2(s - m_new)
        l = alpha * l + jnp.sum(p, axis=-1, keepdims=True)
        acc = alpha * acc + jnp.dot(p.astype(BF16), kc, preferred_element_type=F32)
        m = m_new
    cn = cn_ref[...].astype(BF16).astype(F32)
    kn = kn_ref[...].astype(BF16).astype(F32)
    s = (jnp.sum(ql.astype(F32) * cn, axis=-1, keepdims=True)
         + jnp.sum(qp.astype(F32) * kn, axis=-1, keepdims=True))
    m_new = jnp.maximum(m, s)
    alpha = jnp.exp2(m - m_new)
    p = jnp.exp2(s - m_new)
    l = alpha * l + p
    acc = alpha * acc + p.astype(BF16).astype(F32) * cn
    o_ref[...] = acc / l

    @pl.when(b == nb - 1)
    def _():
        for a in range(CHUNK_AHEAD):
            wait_chunk(total + a, a)


def _paged_attn(pt_flat, ql, qp, c_new, k_new, cache_latent, cache_kpt):
    n = ql.shape[0]
    rows = PAGES_PER_CHUNK * PAGE_SIZE
    per_seq = lambda w: pl.BlockSpec((None,) + w, lambda i, pt: (i, 0, 0))
    return pl.pallas_call(
        functools.partial(_paged_kernel, nb=n),
        grid_spec=pltpu.PrefetchScalarGridSpec(
            num_scalar_prefetch=1,
            grid=(n,),
            in_specs=[per_seq((N_HEADS, KV_LORA)), per_seq((N_HEADS, QK_ROPE)),
                      per_seq((1, KV_LORA)), per_seq((1, QK_ROPE)),
                      pl.BlockSpec(memory_space=pl.ANY), pl.BlockSpec(memory_space=pl.ANY)],
            out_specs=per_seq((N_HEADS, KV_LORA)),
            scratch_shapes=[pltpu.VMEM((CHUNK_SLOTS, rows, KV_LORA), F32),
                            pltpu.VMEM((CHUNK_SLOTS, QK_ROPE, rows), F32),
                            pltpu.SemaphoreType.DMA((2, CHUNK_SLOTS))],
        ),
        out_shape=jax.ShapeDtypeStruct((n, N_HEADS, KV_LORA), F32),
        compiler_params=_cparams(("arbitrary",)),
        name="paged_attn",
    )(pt_flat, ql, qp, c_new, k_new, cache_latent, cache_kpt)


def _attn_out_kernel(ol_ref, wuv_ref, wo_ref, h_ref, o_ref):
    acc = h_ref[...]
    for hd in range(N_HEADS):
        oh = ol_ref[:, hd * KV_LORA:(hd + 1) * KV_LORA].astype(BF16)
        th = jnp.dot(oh, wuv_ref[hd], preferred_element_type=F32).astype(BF16)
        acc = acc + jnp.dot(th, wo_ref[hd * V_HEAD:(hd + 1) * V_HEAD, :], preferred_element_type=F32)
    o_ref[...] = acc


def _attn_out(ol, wuv_h, wo, h):
    n = h.shape[0]
    return pl.pallas_call(
        _attn_out_kernel,
        grid=(1,),
        in_specs=[_full(ol.shape), _full(wuv_h.shape), _full(wo.shape), _full(h.shape)],
        out_specs=_full(h.shape),
        out_shape=jax.ShapeDtypeStruct((n, D_MODEL), F32),
        compiler_params=_cparams(("arbitrary",)),
        name="attn_out",
    )(ol, wuv_h, wo, h)


def _rope_tables(pos):
    half = QK_ROPE // 2
    inv = ROPE_THETA ** (-jnp.arange(half, dtype=F32) / half)
    ang = pos.astype(F32)[:, None] * inv[None, :]
    cos, sin = jnp.cos(ang), jnp.sin(ang)
    n = pos.shape[0]
    cos_t = jnp.concatenate([jnp.ones((n, QK_NOPE), F32), cos, cos,
                             jnp.ones((n, HEAD_PAD - ROPE_HI), F32)], axis=1)
    sin_t = jnp.concatenate([jnp.zeros((n, QK_NOPE), F32), sin, sin,
                             jnp.zeros((n, HEAD_PAD - ROPE_HI), F32)], axis=1)
    return cos_t, sin_t


def _moe(h, g_ffn, wr, br, w1, w3, w2, layer, tm):
    t = h.shape[0]
    u, code, gate, cnt = _router(h, g_ffn, wr, br, tm)
    counts = cnt[:, 0].astype(jnp.int32)
    padded = (counts + MOE_BM - 1) // MOE_BM * MOE_BM
    pends = jnp.cumsum(padded)
    pstarts = pends - padded
    n_steps = (t * TOP_K + N_EXPERTS * (MOE_BM - 1) + MOE_BM - 1) // MOE_BM
    n_rows = n_steps * MOE_BM
    eid = code >> RANK_BITS
    eiota = jnp.arange(N_EXPERTS, dtype=jnp.int32)
    start_of = jnp.sum(jnp.where(eid[..., None] == eiota, pstarts, 0), axis=-1)
    dest = (start_of + (code & ((1 << RANK_BITS) - 1))).reshape(-1)
    block_start = jnp.arange(n_steps, dtype=jnp.int32) * MOE_BM
    blk_e = jnp.minimum(jnp.sum((pends[None, :] <= block_start[:, None]).astype(jnp.int32), axis=1),
                        N_EXPERTS - 1)
    inv = jnp.full((n_rows,), -1, jnp.int32).at[dest].set(
        jnp.arange(TOP_K * t, dtype=jnp.int32), unique_indices=True)
    inv = jnp.where(inv >= 0, inv, -jnp.cumsum((inv < 0).astype(jnp.int32)))
    yp = _experts(blk_e, inv, u, w1, w3, w2, layer, n_steps, n_rows + MOE_BM)
    return yp, gate.T


def kernel(x_prompt, x_sample, cache_conv, cache_latent, cache_kpe, page_table, p_prompt, p_sample, g_mix, g_ffn, g_ple, g_final, conv_w_in, conv_b_in, conv_w_dw, conv_b_dw, conv_ln_g, conv_ln_b, conv_w_out, conv_b_out, g_kv_in, w_dkv, g_kv_lat, w_kr, w_uk, w_uv, w_dq, g_q_lat, w_uq, w_o, w_group, b_group, w_inner, b_inner, w_exp_gate, w_exp_up, w_exp_down, w_ple_gate, b_ple_gate, w_ple_proj):
    bp, sp, d = x_prompt.shape
    nd = x_sample.shape[0]
    tp = bp * sp
    row = lambda a: a.reshape(1, -1)

    w_in_b = conv_w_in[0].astype(BF16)
    w_out_b = conv_w_out[0].astype(BF16)
    wdw_pad = jnp.concatenate([conv_w_dw[0], jnp.zeros((HALO - CONV_W, d), F32)], axis=0)
    wr, br = [], []
    for i in range(2):
        wr.append(jnp.concatenate([w_group[i].T, jnp.zeros((8 - N_GROUPS, d), F32),
                                   w_inner[i].reshape(d, N_EXPERTS).T], axis=0))
        br.append(jnp.concatenate([b_group[i], jnp.zeros((8 - N_GROUPS,), F32),
                                   b_inner[i].reshape(N_EXPERTS)]).reshape(ROUTER_ROWS, 1))
    w_dkv_b = w_dkv.astype(BF16)
    w_kr_pad = jnp.zeros((d, HEAD_PAD), F32).at[:, ROPE_LO:ROPE_HI].set(w_kr).astype(BF16)
    w_uk_pad = jnp.pad(w_uk, ((0, 0), (0, 0), (0, HEAD_PAD - QK_NOPE))).reshape(KV_LORA, -1).astype(BF16)
    w_uv_b = w_uv.reshape(KV_LORA, -1).astype(BF16)
    w_dq_b = w_dq[0].astype(BF16)
    w_uq_pad = jnp.pad(w_uq[0].reshape(Q_LORA, N_HEADS, QK_NOPE + QK_ROPE),
                       ((0, 0), (0, 0), (0, HEAD_PAD - ROPE_HI))).reshape(Q_LORA, -1).astype(BF16)
    w_o_b = w_o[0].astype(BF16)
    w_ukt_pad = jnp.pad(jnp.transpose(w_uk, (1, 2, 0)),
                        ((0, 0), (0, HEAD_PAD - QK_NOPE), (0, 0))).astype(BF16)
    w_uv_h = jnp.transpose(w_uv, (1, 0, 2)).astype(BF16)
    wg_b = [w_ple_gate[i].astype(BF16) for i in range(2)]
    wp_b = [w_ple_proj[i].astype(BF16) for i in range(2)]

    def moe_ple(h, layer, p, tm, final):
        yp, gate_t = _moe(h, row(g_ffn[layer]), wr[layer], br[layer],
                          w_exp_gate, w_exp_up, w_exp_down, layer, tm)
        return _combine(yp, h, gate_t, p, row(g_ple[layer]), wg_b[layer], row(b_ple_gate[layer]),
                        wp_b[layer], row(g_final), tm, final)

    def kvq(h, cos_t, sin_t, tm, n_pos_blocks):
        return _kvq(h, row(g_kv_in), row(g_mix[1]), w_dkv_b, row(g_kv_lat), w_kr_pad, w_uk_pad, w_uv_b,
                    w_dq_b, row(g_q_lat[0]), w_uq_pad, cos_t, sin_t, tm, n_pos_blocks)

    xp = x_prompt.reshape(tp, d)
    z = _conv_in(xp, row(g_mix[0]), w_in_b, row(conv_b_in[0]), 512)
    z3 = z.reshape(bp, sp, d)
    conv_prompt = z3[:, sp - (CONV_W - 1):, :][None]
    ctx0 = jnp.zeros((bp, HALO, d), F32)
    h = _conv_mix(z3, ctx0, x_prompt, wdw_pad, row(conv_b_dw[0]), row(conv_ln_g[0]), row(conv_ln_b[0]),
                  w_out_b, row(conv_b_out[0])).reshape(tp, d)
    h = moe_ple(h, 0, p_prompt[0].reshape(tp, PLE_DIM), 512, False)
    cos_p, sin_p = _rope_tables(jnp.arange(sp, dtype=jnp.int32))
    lat_p, kpe_p, kcat, v, q = kvq(h, cos_p, sin_p, 512, sp // 512)
    o = _flash(q, kcat, v, bp, sp)
    h = _proj_res(o, w_o_b, h, 512)
    y_prompt = moe_ple(h, 1, p_prompt[1].reshape(tp, PLE_DIM), 512, True).reshape(bp, sp, d)
    latent_prompt = lat_p.reshape(bp, sp, KV_LORA)
    kpe_prompt = kpe_p[:, ROPE_LO:ROPE_HI].reshape(bp, sp, QK_ROPE)

    xs = x_sample.reshape(nd, d)
    zs = _conv_in(xs, row(g_mix[0]), w_in_b, row(conv_b_in[0]), nd)
    ctx_t = jnp.transpose(cache_conv[0], (1, 0, 2))
    conv_sample = jnp.transpose(jnp.concatenate([ctx_t[1:], zs[None]], axis=0), (1, 0, 2))[None]
    hs = _conv_step(ctx_t, conv_w_dw[0][:, None, :], zs, xs,
                    row(conv_b_dw[0]), row(conv_ln_g[0]), row(conv_ln_b[0]), w_out_b, row(conv_b_out[0]))
    hs = moe_ple(hs, 0, p_sample[0].reshape(nd, PLE_DIM), nd, False)
    n_pages = page_table.shape[1]
    pos_s = jnp.full((nd,), n_pages * PAGE_SIZE, jnp.int32)
    cos_s, sin_s = _rope_tables(pos_s)
    lat_s, kpe_s, _, _, qs = kvq(hs, cos_s, sin_s, nd, 1)
    ql = _qlat(qs, w_ukt_pad).reshape(nd, N_HEADS, KV_LORA)
    qp = qs.reshape(nd, N_HEADS, HEAD_PAD)[:, :, ROPE_LO:ROPE_HI]
    kpe_new = kpe_s[:, ROPE_LO:ROPE_HI]
    o_lat = _paged_attn(page_table.reshape(-1), ql, qp, lat_s.reshape(nd, 1, KV_LORA),
                        kpe_new.reshape(nd, 1, QK_ROPE), cache_latent, jnp.transpose(cache_kpe, (0, 2, 1)))
    hs = _attn_out(o_lat.reshape(nd, N_HEADS * KV_LORA), w_uv_h, w_o_b, hs)
    y_sample = moe_ple(hs, 1, p_sample[1].reshape(nd, PLE_DIM), nd, True).reshape(nd, 1, d)
    latent_sample = lat_s.reshape(nd, 1, KV_LORA)
    kpe_sample = kpe_new.reshape(nd, 1, QK_ROPE)

    return (y_prompt, y_sample, conv_prompt, conv_sample, latent_prompt, kpe_prompt, latent_sample, kpe_sample)
```

```python
import functools
import math

import jax
import jax.numpy as jnp
from jax import lax
from jax.experimental import pallas as pl
from jax.experimental.pallas import tpu as pltpu

F32 = jnp.float32
BF16 = jnp.bfloat16

D_MODEL = 1024
PLE_DIM = 256
CONV_W = 31
N_HEADS = 16
QK_NOPE = 64
QK_ROPE = 32
V_HEAD = 64
Q_LORA = 384
KV_LORA = 256
ROPE_THETA = 10000.0
SCALE = 1.0 / math.sqrt(QK_NOPE + QK_ROPE)
LOG2E = 1.4426950408889634
N_GROUPS = 4
EXPERTS_PER_GROUP = 8
N_EXPERTS = N_GROUPS * EXPERTS_PER_GROUP
TOP_K = 2
EXPERT_FF = 512
PAGE_SIZE = 128
EPS = 1e-6

HEAD_PAD = 128
ROPE_LO = QK_NOPE
ROPE_MID = QK_NOPE + QK_ROPE // 2
ROPE_HI = QK_NOPE + QK_ROPE
HALO = 32
MOE_HALF = 128
MOE_BM = 2 * MOE_HALF
ROUTER_ROWS = 40
RANK_BITS = 20
TOK_BITS = 15
VMEM_LIMIT = 56 * 1024 * 1024


def _cparams(sem):
    return pltpu.CompilerParams(dimension_semantics=sem, vmem_limit_bytes=VMEM_LIMIT)


def _rms(x, g):
    return x * lax.rsqrt(jnp.mean(x * x, axis=-1, keepdims=True) + EPS) * g


def _full(shape):
    n = len(shape)
    return pl.BlockSpec(shape, lambda *_: (0,) * n)


LANES = 128
SUBLANES = 8
TOK_ROWS = D_MODEL // LANES


def _load_token_tiles(ref, n):
    return jnp.concatenate([ref[pl.ds(s, n, stride=TOK_ROWS), :] for s in range(TOK_ROWS)], axis=1)


def _store_token_tiles(ref, x):
    n = x.shape[0]
    for s in range(TOK_ROWS):
        ref[pl.ds(s, n, stride=TOK_ROWS), :] = x[:, s * LANES:(s + 1) * LANES]


def _conv_in_kernel(x_ref, g_ref, w_ref, b_ref, z_ref):
    u = _rms(x_ref[...], g_ref[...])
    a = jnp.dot(u.astype(BF16), w_ref[...], preferred_element_type=F32) + b_ref[...]
    z_ref[...] = a[:, :D_MODEL] * jax.nn.sigmoid(a[:, D_MODEL:])


def _conv_in(x, g, w_in, b_in, tm):
    t = x.shape[0]
    return pl.pallas_call(
        _conv_in_kernel,
        grid=(t // tm,),
        in_specs=[pl.BlockSpec((tm, D_MODEL), lambda i: (i, 0)),
                  _full((1, D_MODEL)), _full((D_MODEL, 2 * D_MODEL)), _full((1, 2 * D_MODEL))],
        out_specs=pl.BlockSpec((tm, D_MODEL), lambda i: (i, 0)),
        out_shape=jax.ShapeDtypeStruct((t, D_MODEL), F32),
        compiler_params=_cparams(("parallel",)),
        name="conv_in",
    )(x, g, w_in, b_in)


def _ln_silu_out(y, lng, lnb, wout, bout):
    mu = jnp.mean(y, axis=-1, keepdims=True)
    yc = y - mu
    var = jnp.mean(yc * yc, axis=-1, keepdims=True)
    n = yc * lax.rsqrt(var + EPS) * lng + lnb
    act = n * jax.nn.sigmoid(n)
    return jnp.dot(act.astype(BF16), wout, preferred_element_type=F32) + bout


CONV_TS = 256
CONV_RC = 128
CONV_LC = 256


def _conv_mix_kernel(z_ref, halo_ref, ctx_ref, x_ref, wdw_ref, bdw_ref, lng_ref, lnb_ref,
                     wout_ref, bout_ref, h_ref, buf_ref, y_ref):
    s = pl.program_id(1)

    @pl.when(s == 0)
    def _():
        buf_ref[0, 0:HALO, :] = ctx_ref[...]

    @pl.when(s > 0)
    def _():
        buf_ref[0, 0:HALO, :] = halo_ref[...]

    buf_ref[0, HALO:, :] = z_ref[...]
    shifted_rows = CONV_TS + HALO - SUBLANES
    for sh in range(1, SUBLANES):
        buf_ref[sh, 0:shifted_rows, :] = buf_ref[0, sh:sh + shifted_rows, :]
    first = HALO - (CONV_W - 1)
    for lc in range(D_MODEL // CONV_LC):
        lanes = slice(lc * CONV_LC, (lc + 1) * CONV_LC)
        for rc in range(CONV_TS // CONV_RC):
            r0 = rc * CONV_RC
            acc = jnp.broadcast_to(bdw_ref[:, lanes], (CONV_RC, CONV_LC))
            for k in range(CONV_W):
                sh = (first + k) % SUBLANES
                a0 = r0 + first + k - sh
                acc = acc + wdw_ref[k:k + 1, lanes] * buf_ref[sh, a0:a0 + CONV_RC, lanes]
            y_ref[r0:r0 + CONV_RC, lanes] = acc
    out = _ln_silu_out(y_ref[...], lng_ref[...], lnb_ref[...], wout_ref[...], bout_ref[...])
    h_ref[...] = x_ref[...] + out


def _conv_mix(z3, ctx, x3, wdw, bdw, lng, lnb, wout, bout):
    b, s, d = z3.shape
    ts = CONV_TS
    hb = ts // HALO
    return pl.pallas_call(
        _conv_mix_kernel,
        grid=(b, s // ts),
        in_specs=[pl.BlockSpec((None, ts, d), lambda i, j: (i, j, 0)),
                  pl.BlockSpec((None, HALO, d), lambda i, j: (i, jnp.maximum(j * hb - 1, 0), 0)),
                  pl.BlockSpec((None, HALO, d), lambda i, j: (i, 0, 0)),
                  pl.BlockSpec((None, ts, d), lambda i, j: (i, j, 0)),
                  _full((HALO, d)), _full((1, d)), _full((1, d)), _full((1, d)),
                  _full((d, d)), _full((1, d))],
        out_specs=pl.BlockSpec((None, ts, d), lambda i, j: (i, j, 0)),
        out_shape=jax.ShapeDtypeStruct((b, s, d), F32),
        scratch_shapes=[pltpu.VMEM((SUBLANES, ts + HALO, d), F32), pltpu.VMEM((ts, d), F32)],
        compiler_params=_cparams(("parallel", "arbitrary")),
        name="conv_mix",
    )(z3, z3, ctx, x3, wdw, bdw, lng, lnb, wout, bout)


def _conv_step_kernel(ctx_ref, w_ref, z_ref, x_ref, bdw_ref, lng_ref, lnb_ref, wout_ref, bout_ref,
                      h_ref, acc_ref):
    k = pl.program_id(0)

    @pl.when(k == 0)
    def _():
        acc_ref[...] = jnp.broadcast_to(bdw_ref[...], acc_ref.shape)

    @pl.when(k < CONV_W - 1)
    def _():
        acc_ref[...] += w_ref[...] * ctx_ref[...]

    @pl.when(k == CONV_W - 1)
    def _():
        y = acc_ref[...] + w_ref[...] * z_ref[...]
        out = _ln_silu_out(y, lng_ref[...], lnb_ref[...], wout_ref[...], bout_ref[...])
        h_ref[...] = x_ref[...] + out


def _conv_step(ctx_t, wdw3, z, x, bdw, lng, lnb, wout, bout):
    n, d = z.shape
    last = CONV_W - 2
    return pl.pallas_call(
        _conv_step_kernel,
        grid=(CONV_W,),
        in_specs=[pl.BlockSpec((None, n, d), lambda k: (jnp.minimum(k, last), 0, 0)),
                  pl.BlockSpec((None, 1, d), lambda k: (k, 0, 0)),
                  _full((n, d)), _full((n, d)), _full((1, d)), _full((1, d)), _full((1, d)),
                  _full((d, d)), _full((1, d))],
        out_specs=_full((n, d)),
        out_shape=jax.ShapeDtypeStruct((n, d), F32),
        scratch_shapes=[pltpu.VMEM((n, d), F32)],
        compiler_params=_cparams(("arbitrary",)),
        name="conv_step",
    )(ctx_t, wdw3, z, x, bdw, lng, lnb, wout, bout)


def _router_kernel(h_ref, g_ref, wr_ref, br_ref, u_ref, code_ref, gate_ref, cnt_ref, carry_ref):
    i = pl.program_id(0)
    tm = h_ref.shape[0]

    @pl.when(i == 0)
    def _():
        carry_ref[...] = jnp.zeros_like(carry_ref)

    u = _rms(h_ref[...], g_ref[...])
    _store_token_tiles(u_ref, u)
    logits = lax.dot_general(wr_ref[...], u, (((1,), (1,)), ((), ())),
                             precision=lax.Precision.HIGHEST, preferred_element_type=F32) + br_ref[...]
    gl = logits[0:N_GROUPS]
    il = logits[8:8 + N_EXPERTS]
    gmax = jnp.max(gl, axis=0, keepdims=True)
    giota = lax.broadcasted_iota(jnp.int32, gl.shape, 0)
    gsel = jnp.min(jnp.where(gl == gmax, giota, N_GROUPS), axis=0, keepdims=True)
    gw = 1.0 / jnp.sum(jnp.exp(gl - gmax), axis=0, keepdims=True)
    eiota = lax.broadcasted_iota(jnp.int32, il.shape, 0)
    ml = jnp.where((eiota >> 3) == gsel, il, -jnp.inf)
    v1 = jnp.max(ml, axis=0, keepdims=True)
    i1 = jnp.min(jnp.where(ml == v1, eiota, N_EXPERTS), axis=0, keepdims=True)
    ml2 = jnp.where(eiota == i1, -jnp.inf, ml)
    v2 = jnp.max(ml2, axis=0, keepdims=True)
    i2 = jnp.min(jnp.where(ml2 == v2, eiota, N_EXPERTS), axis=0, keepdims=True)
    e2 = jnp.exp(v2 - v1)
    p1 = 1.0 / (1.0 + e2)
    gate_ref[0:1, :] = p1 * gw
    gate_ref[1:2, :] = e2 * p1 * gw
    hit1 = eiota == i1
    hit2 = eiota == i2
    onehot = jnp.where(hit1 | hit2, 1.0, 0.0)
    r = lax.broadcasted_iota(jnp.int32, (tm, tm), 0)
    c = lax.broadcasted_iota(jnp.int32, (tm, tm), 1)
    before = jnp.where(r < c, 1.0, 0.0).astype(BF16)
    cum = jnp.dot(onehot.astype(BF16), before, preferred_element_type=F32) + carry_ref[:, 0:1]
    rank1 = jnp.sum(jnp.where(hit1, cum, 0.0), axis=0, keepdims=True).astype(jnp.int32)
    rank2 = jnp.sum(jnp.where(hit2, cum, 0.0), axis=0, keepdims=True).astype(jnp.int32)
    code_ref[0:1, :] = (i1 << RANK_BITS) | rank1
    code_ref[1:2, :] = (i2 << RANK_BITS) | rank2
    carry_ref[...] += jnp.sum(onehot, axis=1, keepdims=True)
    cnt_ref[...] = carry_ref[...]


def _router(h, g, wr, br, tm):
    t = h.shape[0]
    row2 = pl.BlockSpec((TOP_K, tm), lambda i: (0, i))
    return pl.pallas_call(
        _router_kernel,
        grid=(t // tm,),
        in_specs=[pl.BlockSpec((tm, D_MODEL), lambda i: (i, 0)),
                  _full((1, D_MODEL)), _full((ROUTER_ROWS, D_MODEL)), _full((ROUTER_ROWS, 1))],
        out_specs=[pl.BlockSpec((tm * TOK_ROWS, LANES), lambda i: (i, 0)), row2, row2,
                   _full((N_EXPERTS, 128))],
        out_shape=[jax.ShapeDtypeStruct((t * TOK_ROWS, LANES), F32),
                   jax.ShapeDtypeStruct((TOP_K, t), jnp.int32),
                   jax.ShapeDtypeStruct((TOP_K, t), F32),
                   jax.ShapeDtypeStruct((N_EXPERTS, 128), F32)],
        scratch_shapes=[pltpu.VMEM((N_EXPERTS, 128), F32)],
        compiler_params=_cparams(("arbitrary",)),
        name="router",
    )(h, g, wr, br)


def _expert_kernel(blk_ref, idx_ref, u_hbm, w1_ref, w3_ref, w2_ref, yp_hbm, xbuf, ybuf, w1b, w3b, w2b,
                   gsem, ssem, *, n_steps):
    j = pl.program_id(0)
    prime0 = n_steps * MOE_BM
    half_rows = MOE_HALF * TOK_ROWS

    def tile_of(t):
        return pl.ds(pl.multiple_of(t * TOK_ROWS, TOK_ROWS), TOK_ROWS)

    def gather_start(step, half, par):
        base = step * MOE_BM + half * MOE_HALF
        for r in range(MOE_HALF):
            tok = idx_ref[base + r] & ((1 << TOK_BITS) - 1)
            pltpu.make_async_copy(u_hbm.at[tile_of(tok)], xbuf.at[par, half, tile_of(r)],
                                  gsem.at[par, half]).start()

    def gather_wait(half, par):
        pltpu.make_async_copy(u_hbm.at[pl.ds(0, half_rows)], xbuf.at[par, half], gsem.at[par, half]).wait()

    def scatter_start(step, half, to_prime=None):
        base = step * MOE_BM + half * MOE_HALF
        for r in range(MOE_HALF):
            dst = lax.shift_right_logical(idx_ref[base + r], jnp.int32(TOK_BITS))
            if to_prime is not None:
                dst = jnp.where(to_prime, prime0 + half * MOE_HALF + r, dst)
            pltpu.make_async_copy(ybuf.at[half, tile_of(r)], yp_hbm.at[tile_of(dst)], ssem.at[half]).start()

    def scatter_wait(half):
        pltpu.make_async_copy(ybuf.at[half], yp_hbm.at[pl.ds(0, half_rows)], ssem.at[half]).wait()

    @pl.when(j == 0)
    def _():
        ybuf[...] = jnp.zeros_like(ybuf)
        pltpu.make_async_copy(ybuf.at[0], yp_hbm.at[pl.ds(prime0 * TOK_ROWS, half_rows)], ssem.at[0]).start()
        gather_start(0, 0, 0)
        gather_start(0, 1, 0)

    e = blk_ref[j]
    prev = blk_ref[jnp.maximum(j - 1, 0)]

    @pl.when((j == 0) | (e != prev))
    def _():
        w1b[...] = w1_ref[...].astype(BF16)
        w3b[...] = w3_ref[...].astype(BF16)
        w2b[...] = w2_ref[...].astype(BF16)

    def compute(half, par):
        x = _load_token_tiles(xbuf.at[par, half], MOE_HALF).astype(BF16)
        a = jnp.dot(x, w1b[...], preferred_element_type=F32)
        b = jnp.dot(x, w3b[...], preferred_element_type=F32)
        hmid = (a * jax.nn.sigmoid(a)) * b
        _store_token_tiles(ybuf.at[half], jnp.dot(hmid.astype(BF16), w2b[...], preferred_element_type=F32))

    nxt = jnp.minimum(j + 1, n_steps - 1)

    def step_body(par):
        gather_wait(0, par)
        scatter_wait(0)
        gather_start(nxt, 0, 1 - par)
        scatter_start(jnp.maximum(j - 1, 0), 1, to_prime=(j == 0))
        compute(0, par)
        gather_wait(1, par)
        scatter_wait(1)
        gather_start(nxt, 1, 1 - par)
        scatter_start(j, 0)
        compute(1, par)

    for par in range(2):
        pl.when(j % 2 == par)(functools.partial(step_body, par))

    @pl.when(j == n_steps - 1)
    def _():
        spare = n_steps % 2
        scatter_start(j, 1)
        gather_wait(0, spare)
        gather_wait(1, spare)
        scatter_wait(0)
        scatter_wait(1)


def _experts(blk_e, idx, u, w1, w3, w2, layer, n_steps, n_out_rows):
    wspec = lambda s: pl.BlockSpec((None, None) + s, lambda j, b, i: (layer, b[j], 0, 0))
    return pl.pallas_call(
        functools.partial(_expert_kernel, n_steps=n_steps),
        grid_spec=pltpu.PrefetchScalarGridSpec(
            num_scalar_prefetch=2,
            grid=(n_steps,),
            in_specs=[pl.BlockSpec(memory_space=pl.ANY),
                      wspec((D_MODEL, EXPERT_FF)), wspec((D_MODEL, EXPERT_FF)), wspec((EXPERT_FF, D_MODEL))],
            out_specs=pl.BlockSpec(memory_space=pl.ANY),
            scratch_shapes=[pltpu.VMEM((2, 2, MOE_HALF * TOK_ROWS, LANES), F32),
                            pltpu.VMEM((2, MOE_HALF * TOK_ROWS, LANES), F32),
                            pltpu.VMEM((D_MODEL, EXPERT_FF), BF16),
                            pltpu.VMEM((D_MODEL, EXPERT_FF), BF16),
                            pltpu.VMEM((EXPERT_FF, D_MODEL), BF16),
                            pltpu.SemaphoreType.DMA((2, 2)),
                            pltpu.SemaphoreType.DMA((2,))],
        ),
        out_shape=jax.ShapeDtypeStruct((n_out_rows * TOK_ROWS, LANES), F32),
        compiler_params=_cparams(("arbitrary",)),
        name="moe_experts",
    )(blk_e, idx, u, w1, w3, w2)


def _combine_kernel(y0_ref, y1_ref, h_ref, gate_ref, p_ref, g_ref, wg_ref, bg_ref, wp_ref, gf_ref,
                    o_ref, *, final):
    gate = gate_ref[...]
    tm = h_ref.shape[0]
    y0 = _load_token_tiles(y0_ref, tm)
    y1 = _load_token_tiles(y1_ref, tm)
    h = h_ref[...] + y0 * gate[:, 0:1] + y1 * gate[:, 1:2]
    u = _rms(h, g_ref[...])
    sg = jax.nn.sigmoid(jnp.dot(u.astype(BF16), wg_ref[...], preferred_element_type=F32) + bg_ref[...])
    pp = jnp.dot(p_ref[...].astype(BF16), wp_ref[...], preferred_element_type=F32)
    h = h + sg * pp
    if final:
        h = _rms(h, gf_ref[...])
    o_ref[...] = h


def _combine(yp, h, gate_t, p, g, wg, bg, wp, gf, tm, final):
    t = h.shape[0]
    nt = t // tm
    return pl.pallas_call(
        functools.partial(_combine_kernel, final=final),
        grid=(nt,),
        in_specs=[pl.BlockSpec((tm * TOK_ROWS, LANES), lambda i: (i, 0)),
                  pl.BlockSpec((tm * TOK_ROWS, LANES), lambda i: (nt + i, 0)),
                  pl.BlockSpec((tm, D_MODEL), lambda i: (i, 0)),
                  pl.BlockSpec((tm, TOP_K), lambda i: (i, 0)),
                  pl.BlockSpec((tm, PLE_DIM), lambda i: (i, 0)),
                  _full((1, D_MODEL)), _full((D_MODEL, D_MODEL)), _full((1, D_MODEL)),
                  _full((PLE_DIM, D_MODEL)), _full((1, D_MODEL))],
        out_specs=pl.BlockSpec((tm, D_MODEL), lambda i: (i, 0)),
        out_shape=jax.ShapeDtypeStruct((t, D_MODEL), F32),
        compiler_params=_cparams(("parallel",)),
        name="moe_combine_ple",
    )(yp, yp, h, gate_t, p, g, wg, bg, wp, gf)


def _rope_group(x, cos, sin, lane):
    half = QK_ROPE // 2
    rot = jnp.where(lane < ROPE_MID, -pltpu.roll(x, HEAD_PAD - half, 1), pltpu.roll(x, half, 1))
    return x * cos + rot * sin


def _kvq_kernel(h_ref, gkv_ref, gmix_ref, wdkv_ref, glat_ref, wkr_ref, wuk_ref, wuv_ref,
                wdq_ref, gq_ref, wuq_ref, cos_ref, sin_ref,
                lat_ref, kpe_ref, kcat_ref, v_ref, q_ref):
    h = h_ref[...]
    tm = h.shape[0]
    cos = cos_ref[...]
    sin = sin_ref[...]
    lane = lax.broadcasted_iota(jnp.int32, (tm, HEAD_PAD), 1)
    ukv = _rms(h, gkv_ref[...]).astype(BF16)
    c = _rms(jnp.dot(ukv, wdkv_ref[...], preferred_element_type=F32), glat_ref[...])
    lat_ref[...] = c
    kpe = _rope_group(jnp.dot(ukv, wkr_ref[...], preferred_element_type=F32), cos, sin, lane)
    kpe_ref[...] = kpe
    cb = c.astype(BF16)
    v_ref[...] = jnp.dot(cb, wuv_ref[...], preferred_element_type=F32).astype(BF16)
    kn = jnp.dot(cb, wuk_ref[...], preferred_element_type=F32)
    umix = _rms(h, gmix_ref[...]).astype(BF16)
    cq = _rms(jnp.dot(umix, wdq_ref[...], preferred_element_type=F32), gq_ref[...])
    q = jnp.dot(cq.astype(BF16), wuq_ref[...], preferred_element_type=F32)
    for hd in range(N_HEADS):
        lanes = slice(hd * HEAD_PAD, (hd + 1) * HEAD_PAD)
        kcat_ref[:, lanes] = (kn[:, lanes] + kpe).astype(BF16)
        q_ref[:, lanes] = (_rope_group(q[:, lanes], cos, sin, lane) * (SCALE * LOG2E)).astype(BF16)


def _kvq(h, gkv, gmix, wdkv, glat, wkr, wuk, wuv, wdq, gq, wuq, cos, sin, tm, n_pos_blocks):
    t = h.shape[0]
    row = lambda w: pl.BlockSpec((tm, w), lambda i: (i, 0))
    return pl.pallas_call(
        _kvq_kernel,
        grid=(t // tm,),
        in_specs=[row(D_MODEL), _full((1, D_MODEL)), _full((1, D_MODEL)),
                  _full((D_MODEL, KV_LORA)), _full((1, KV_LORA)), _full((D_MODEL, HEAD_PAD)),
                  _full((KV_LORA, N_HEADS * HEAD_PAD)), _full((KV_LORA, N_HEADS * V_HEAD)),
                  _full((D_MODEL, Q_LORA)), _full((1, Q_LORA)), _full((Q_LORA, N_HEADS * HEAD_PAD)),
                  pl.BlockSpec((tm, HEAD_PAD), lambda i: (i % n_pos_blocks, 0)),
                  pl.BlockSpec((tm, HEAD_PAD), lambda i: (i % n_pos_blocks, 0))],
        out_specs=[row(KV_LORA), row(HEAD_PAD), row(N_HEADS * HEAD_PAD), row(N_HEADS * V_HEAD),
                   row(N_HEADS * HEAD_PAD)],
        out_shape=[jax.ShapeDtypeStruct((t, KV_LORA), F32),
                   jax.ShapeDtypeStruct((t, HEAD_PAD), F32),
                   jax.ShapeDtypeStruct((t, N_HEADS * HEAD_PAD), BF16),
                   jax.ShapeDtypeStruct((t, N_HEADS * V_HEAD), BF16),
                   jax.ShapeDtypeStruct((t, N_HEADS * HEAD_PAD), BF16)],
        compiler_params=_cparams(("parallel",)),
        name="kvq_proj",
    )(h, gkv, gmix, wdkv, glat, wkr, wuk, wuv, wdq, gq, wuq, cos, sin)


ATT_TQ = 512
ATT_TK = 256


def _flash_kernel(q_ref, k_ref, v_ref, o_ref):
    s_len = q_ref.shape[0]
    tq, tk = ATT_TQ, ATT_TK
    row = lax.broadcasted_iota(jnp.int32, (tq, tk), 0)
    col = lax.broadcasted_iota(jnp.int32, (tq, tk), 1)
    lane = lax.broadcasted_iota(jnp.int32, (tq, 2 * V_HEAD), 1)
    nt = (((1,), (1,)), ((), ()))
    for qi in range(s_len // tq):
        q0 = qi * tq
        qs = [q_ref[q0:q0 + tq, hh * HEAD_PAD:(hh + 1) * HEAD_PAD] for hh in range(2)]
        m = [jnp.full((tq, 1), -jnp.inf, F32) for _ in range(2)]
        l = [jnp.zeros((tq, 1), F32) for _ in range(2)]
        acc = [jnp.zeros((tq, 2 * V_HEAD), F32) for _ in range(2)]
        for kj in range((q0 + tq) // tk):
            k0 = kj * tk
            v = v_ref[k0:k0 + tk, :]
            for hh in range(2):
                k = k_ref[k0:k0 + tk, hh * HEAD_PAD:(hh + 1) * HEAD_PAD]
                s = lax.dot_general(qs[hh], k, nt, preferred_element_type=F32)
                if k0 + tk - 1 > q0:
                    s = jnp.where(col + k0 <= row + q0, s, -jnp.inf)
                m_new = jnp.maximum(m[hh], jnp.max(s, axis=-1, keepdims=True))
                alpha = jnp.exp2(m[hh] - m_new)
                p = jnp.exp2(s - m_new)
                l[hh] = alpha * l[hh] + jnp.sum(p, axis=-1, keepdims=True)
                acc[hh] = alpha * acc[hh] + jnp.dot(p.astype(BF16), v, preferred_element_type=F32)
                m[hh] = m_new
        o = jnp.where(lane < V_HEAD, acc[0] / l[0], acc[1] / l[1])
        o_ref[q0:q0 + tq, :] = o.astype(BF16)


def _flash(q, kcat, v, b, s):
    return pl.pallas_call(
        _flash_kernel,
        grid=(b, N_HEADS // 2),
        in_specs=[pl.BlockSpec((s, 2 * HEAD_PAD), lambda i, h: (i, h)),
                  pl.BlockSpec((s, 2 * HEAD_PAD), lambda i, h: (i, h)),
                  pl.BlockSpec((s, 2 * V_HEAD), lambda i, h: (i, h))],
        out_specs=pl.BlockSpec((s, 2 * V_HEAD), lambda i, h: (i, h)),
        out_shape=jax.ShapeDtypeStruct((b * s, N_HEADS * V_HEAD), BF16),
        compiler_params=_cparams(("parallel", "parallel")),
        name="flash_attn",
    )(q, kcat, v)


def _proj_res_kernel(a_ref, w_ref, h_ref, o_ref):
    o_ref[...] = h_ref[...] + jnp.dot(a_ref[...], w_ref[...], preferred_element_type=F32)


def _proj_res(a, w, h, tm):
    t, k = a.shape
    return pl.pallas_call(
        _proj_res_kernel,
        grid=(t // tm,),
        in_specs=[pl.BlockSpec((tm, k), lambda i: (i, 0)), _full(w.shape),
                  pl.BlockSpec((tm, D_MODEL), lambda i: (i, 0))],
        out_specs=pl.BlockSpec((tm, D_MODEL), lambda i: (i, 0)),
        out_shape=jax.ShapeDtypeStruct((t, D_MODEL), F32),
        compiler_params=_cparams(("parallel",)),
        name="proj_residual",
    )(a, w, h)


def _qlat_kernel(q_ref, wukt_ref, o_ref):
    for hd in range(N_HEADS):
        qh = q_ref[:, hd * HEAD_PAD:(hd + 1) * HEAD_PAD]
        o_ref[:, hd * KV_LORA:(hd + 1) * KV_LORA] = jnp.dot(
            qh, wukt_ref[hd], preferred_element_type=F32).astype(BF16)


def _qlat(q, wukt):
    n = q.shape[0]
    return pl.pallas_call(
        _qlat_kernel,
        in_specs=[_full(q.shape), _full(wukt.shape)],
        out_specs=_full((n, N_HEADS * KV_LORA)),
        out_shape=jax.ShapeDtypeStruct((n, N_HEADS * KV_LORA), BF16),
        grid=(1,),
        compiler_params=_cparams(("arbitrary",)),
        name="q_absorb",
    )(q, wukt)


PAGES_PER_CHUNK = 32
CHUNK_SLOTS = 4
CHUNK_AHEAD = 2


def _paged_kernel(pt_ref, ql_ref, qp_ref, cn_ref, kn_ref, lat_hbm, kpt_hbm, o_ref, lbuf, kbuf, sems, *, nb):
    b = pl.program_id(0)
    n_pages = pt_ref.shape[0] // nb
    cp = PAGES_PER_CHUNK
    n_chunks = n_pages // cp
    assert n_chunks == CHUNK_SLOTS
    total = nb * n_chunks

    def start_chunk(g, slot):
        gg = jnp.where(g < total, g, 0)
        for j in range(cp):
            page = pt_ref[gg * cp + j]
            src = pl.ds(pl.multiple_of(page * PAGE_SIZE, PAGE_SIZE), PAGE_SIZE)
            pltpu.make_async_copy(lat_hbm.at[src], lbuf.at[slot, pl.ds(j * PAGE_SIZE, PAGE_SIZE)],
                                  sems.at[0, slot]).start()
            pltpu.make_async_copy(kpt_hbm.at[page], kbuf.at[slot, j], sems.at[1, slot]).start()

    def wait_chunk(g, slot):
        del g
        pltpu.make_async_copy(lat_hbm.at[pl.ds(0, cp * PAGE_SIZE)], lbuf.at[slot], sems.at[0, slot]).wait()
        pltpu.make_async_copy(kpt_hbm.at[pl.ds(0, cp)], kbuf.at[slot], sems.at[1, slot]).wait()

    @pl.when(b == 0)
    def _():
        for a in range(CHUNK_AHEAD):
            start_chunk(a, a)

    ql = ql_ref[...]
    qp = qp_ref[...]
    nt = (((1,), (1,)), ((), ()))
    m = jnp.full((N_HEADS, 1), -jnp.inf, F32)
    l = jnp.zeros((N_HEADS, 1), F32)
    acc = jnp.zeros((N_HEADS, KV_LORA), F32)
    for c in range(n_chunks):
        g = b * n_chunks + c
        if c % CHUNK_AHEAD == 0:
            for a in range(CHUNK_AHEAD):
                start_chunk(g + CHUNK_AHEAD + a, (c + CHUNK_AHEAD + a) % CHUNK_SLOTS)
            for a in range(CHUNK_AHEAD):
                wait_chunk(g + a, c + a)
        kc = lbuf[c].astype(BF16)
        kp = jnp.concatenate([kbuf[c, jj] for jj in range(cp)], axis=1).astype(BF16)
        s = (lax.dot_general(ql, kc, nt, preferred_element_type=F32)
             + jnp.dot(qp, kp, preferred_element_type=F32))
        m_new = jnp.maximum(m, jnp.max(s, axis=-1, keepdims=True))
        alpha = jnp.exp2(m - m_new)
        p = jnp.exp2(s - m_new)
        l = alpha * l + jnp.sum(p, axis=-1, keepdims=True)
        acc = alpha * acc + jnp.dot(p.astype(BF16), kc, preferred_element_type=F32)
        m = m_new
    cn = cn_ref[...].astype(BF16).astype(F32)
    kn = kn_ref[...].astype(BF16).astype(F32)
    s = (jnp.sum(ql.astype(F32) * cn, axis=-1, keepdims=True)
         + jnp.sum(qp.astype(F32) * kn, axis=-1, keepdims=True))
    m_new = jnp.maximum(m, s)
    alpha = jnp.exp2(m - m_new)
    p = jnp.exp2(s - m_new)
    l = alpha * l + p
    acc = alpha * acc + p.astype(BF16).astype(F32) * cn
    o_ref[...] = acc / l

    @pl.when(b == nb - 1)
    def _():
        for a in range(CHUNK_AHEAD):
            wait_chunk(total + a, a)


def _paged_attn(pt_flat, ql, qp, c_new, k_new, cache_latent, cache_kpt):
    n = ql.shape[0]
    rows = PAGES_PER_CHUNK * PAGE_SIZE
    per_seq = lambda w: pl.BlockSpec((None,) + w, lambda i, pt: (i, 0, 0))
    return pl.pallas_call(
        functools.partial(_paged_kernel, nb=n),
        grid_spec=pltpu.PrefetchScalarGridSpec(
            num_scalar_prefetch=1,
            grid=(n,),
            in_specs=[per_seq((N_HEADS, KV_LORA)), per_seq((N_HEADS, QK_ROPE)),
                      per_seq((1, KV_LORA)), per_seq((1, QK_ROPE)),
                      pl.BlockSpec(memory_space=pl.ANY), pl.BlockSpec(memory_space=pl.ANY)],
            out_specs=per_seq((N_HEADS, KV_LORA)),
            scratch_shapes=[pltpu.VMEM((CHUNK_SLOTS, rows, KV_LORA), F32),
                            pltpu.VMEM((CHUNK_SLOTS, PAGES_PER_CHUNK, QK_ROPE, PAGE_SIZE), F32),
                            pltpu.SemaphoreType.DMA((2, CHUNK_SLOTS))],
        ),
        out_shape=jax.ShapeDtypeStruct((n, N_HEADS, KV_LORA), F32),
        compiler_params=_cparams(("arbitrary",)),
        name="paged_attn",
    )(pt_flat, ql, qp, c_new, k_new, cache_latent, cache_kpt)


def _attn_out_kernel(ol_ref, wuv_ref, wo_ref, h_ref, o_ref):
    acc = h_ref[...]
    for hd in range(N_HEADS):
        oh = ol_ref[:, hd * KV_LORA:(hd + 1) * KV_LORA].astype(BF16)
        th = jnp.dot(oh, wuv_ref[hd], preferred_element_type=F32).astype(BF16)
        acc = acc + jnp.dot(th, wo_ref[hd * V_HEAD:(hd + 1) * V_HEAD, :], preferred_element_type=F32)
    o_ref[...] = acc


def _attn_out(ol, wuv_h, wo, h):
    n = h.shape[0]
    return pl.pallas_call(
        _attn_out_kernel,
        grid=(1,),
        in_specs=[_full(ol.shape), _full(wuv_h.shape), _full(wo.shape), _full(h.shape)],
        out_specs=_full(h.shape),
        out_shape=jax.ShapeDtypeStruct((n, D_MODEL), F32),
        compiler_params=_cparams(("arbitrary",)),
        name="attn_out",
    )(ol, wuv_h, wo, h)


def _rope_tables(pos):
    half = QK_ROPE // 2
    inv = ROPE_THETA ** (-jnp.arange(half, dtype=F32) / half)
    ang = pos.astype(F32)[:, None] * inv[None, :]
    cos, sin = jnp.cos(ang), jnp.sin(ang)
    n = pos.shape[0]
    cos_t = jnp.concatenate([jnp.ones((n, QK_NOPE), F32), cos, cos,
                             jnp.ones((n, HEAD_PAD - ROPE_HI), F32)], axis=1)
    sin_t = jnp.concatenate([jnp.zeros((n, QK_NOPE), F32), sin, sin,
                             jnp.zeros((n, HEAD_PAD - ROPE_HI), F32)], axis=1)
    return cos_t, sin_t


def _moe(h, g_ffn, wr, br, w1, w3, w2, layer, tm):
    t = h.shape[0]
    u, code, gate, cnt = _router(h, g_ffn, wr, br, tm)
    counts = cnt[:, 0].astype(jnp.int32)
    padded = (counts + MOE_BM - 1) // MOE_BM * MOE_BM
    pends = jnp.cumsum(padded)
    pstarts = pends - padded
    n_steps = (t * TOP_K + N_EXPERTS * (MOE_BM - 1) + MOE_BM - 1) // MOE_BM
    n_rows = n_steps * MOE_BM
    eid = code >> RANK_BITS
    eiota = jnp.arange(N_EXPERTS, dtype=jnp.int32)
    start_of = jnp.sum(jnp.where(eid[..., None] == eiota, pstarts, 0), axis=-1)
    dest = (start_of + (code & ((1 << RANK_BITS) - 1))).reshape(-1)
    block_start = jnp.arange(n_steps, dtype=jnp.int32) * MOE_BM
    blk_e = jnp.minimum(jnp.sum((pends[None, :] <= block_start[:, None]).astype(jnp.int32), axis=1),
                        N_EXPERTS - 1)
    inv = jnp.full((n_rows,), -1, jnp.int32).at[dest].set(
        jnp.arange(TOP_K * t, dtype=jnp.int32), unique_indices=True)
    is_pad = inv < 0
    rows = jnp.arange(n_rows, dtype=jnp.int32)
    out_row = jnp.where(is_pad, TOP_K * t - 1 + jnp.cumsum(is_pad.astype(jnp.int32)), inv)
    src_tok = jnp.where(is_pad, rows, inv) & (t - 1)
    n_out = n_rows + MOE_BM
    assert t & (t - 1) == 0 and t <= (1 << TOK_BITS) and n_out <= (1 << (32 - TOK_BITS))
    idx = (out_row << TOK_BITS) | src_tok
    yp = _experts(blk_e, idx, u, w1, w3, w2, layer, n_steps, n_out)
    return yp, gate.T


def kernel(x_prompt, x_sample, cache_conv, cache_latent, cache_kpe, page_table, p_prompt, p_sample, g_mix, g_ffn, g_ple, g_final, conv_w_in, conv_b_in, conv_w_dw, conv_b_dw, conv_ln_g, conv_ln_b, conv_w_out, conv_b_out, g_kv_in, w_dkv, g_kv_lat, w_kr, w_uk, w_uv, w_dq, g_q_lat, w_uq, w_o, w_group, b_group, w_inner, b_inner, w_exp_gate, w_exp_up, w_exp_down, w_ple_gate, b_ple_gate, w_ple_proj):
    bp, sp, d = x_prompt.shape
    nd = x_sample.shape[0]
    tp = bp * sp
    row = lambda a: a.reshape(1, -1)

    w_in_b = conv_w_in[0].astype(BF16)
    w_out_b = conv_w_out[0].astype(BF16)
    wdw_pad = jnp.concatenate([conv_w_dw[0], jnp.zeros((HALO - CONV_W, d), F32)], axis=0)
    wr, br = [], []
    for i in range(2):
        wr.append(jnp.concatenate([w_group[i].T, jnp.zeros((8 - N_GROUPS, d), F32),
                                   w_inner[i].reshape(d, N_EXPERTS).T], axis=0))
        br.append(jnp.concatenate([b_group[i], jnp.zeros((8 - N_GROUPS,), F32),
                                   b_inner[i].reshape(N_EXPERTS)]).reshape(ROUTER_ROWS, 1))
    w_dkv_b = w_dkv.astype(BF16)
    w_kr_pad = jnp.zeros((d, HEAD_PAD), F32).at[:, ROPE_LO:ROPE_HI].set(w_kr).astype(BF16)
    w_uk_pad = jnp.pad(w_uk, ((0, 0), (0, 0), (0, HEAD_PAD - QK_NOPE))).reshape(KV_LORA, -1).astype(BF16)
    w_uv_b = w_uv.reshape(KV_LORA, -1).astype(BF16)
    w_dq_b = w_dq[0].astype(BF16)
    w_uq_pad = jnp.pad(w_uq[0].reshape(Q_LORA, N_HEADS, QK_NOPE + QK_ROPE),
                       ((0, 0), (0, 0), (0, HEAD_PAD - ROPE_HI))).reshape(Q_LORA, -1).astype(BF16)
    w_o_b = w_o[0].astype(BF16)
    w_ukt_pad = jnp.pad(jnp.transpose(w_uk, (1, 2, 0)),
                        ((0, 0), (0, HEAD_PAD - QK_NOPE), (0, 0))).astype(BF16)
    w_uv_h = jnp.transpose(w_uv, (1, 0, 2)).astype(BF16)
    wg_b = [w_ple_gate[i].astype(BF16) for i in range(2)]
    wp_b = [w_ple_proj[i].astype(BF16) for i in range(2)]

    def moe_ple(h, layer, p, tm, final):
        yp, gate_t = _moe(h, row(g_ffn[layer]), wr[layer], br[layer],
                          w_exp_gate, w_exp_up, w_exp_down, layer, tm)
        return _combine(yp, h, gate_t, p, row(g_ple[layer]), wg_b[layer], row(b_ple_gate[layer]),
                        wp_b[layer], row(g_final), tm, final)

    def kvq(h, cos_t, sin_t, tm, n_pos_blocks):
        return _kvq(h, row(g_kv_in), row(g_mix[1]), w_dkv_b, row(g_kv_lat), w_kr_pad, w_uk_pad, w_uv_b,
                    w_dq_b, row(g_q_lat[0]), w_uq_pad, cos_t, sin_t, tm, n_pos_blocks)

    xp = x_prompt.reshape(tp, d)
    z = _conv_in(xp, row(g_mix[0]), w_in_b, row(conv_b_in[0]), 512)
    z3 = z.reshape(bp, sp, d)
    conv_prompt = z3[:, sp - (CONV_W - 1):, :][None]
    ctx0 = jnp.zeros((bp, HALO, d), F32)
    h = _conv_mix(z3, ctx0, x_prompt, wdw_pad, row(conv_b_dw[0]), row(conv_ln_g[0]), row(conv_ln_b[0]),
                  w_out_b, row(conv_b_out[0])).reshape(tp, d)
    h = moe_ple(h, 0, p_prompt[0].reshape(tp, PLE_DIM), 512, False)
    cos_p, sin_p = _rope_tables(jnp.arange(sp, dtype=jnp.int32))
    lat_p, kpe_p, kcat, v, q = kvq(h, cos_p, sin_p, 512, sp // 512)
    o = _flash(q, kcat, v, bp, sp)
    h = _proj_res(o, w_o_b, h, 512)
    y_prompt = moe_ple(h, 1, p_prompt[1].reshape(tp, PLE_DIM), 512, True).reshape(bp, sp, d)
    latent_prompt = lat_p.reshape(bp, sp, KV_LORA)
    kpe_prompt = kpe_p[:, ROPE_LO:ROPE_HI].reshape(bp, sp, QK_ROPE)

    xs = x_sample.reshape(nd, d)
    zs = _conv_in(xs, row(g_mix[0]), w_in_b, row(conv_b_in[0]), nd)
    ctx_t = jnp.transpose(cache_conv[0], (1, 0, 2))
    conv_sample = jnp.transpose(jnp.concatenate([ctx_t[1:], zs[None]], axis=0), (1, 0, 2))[None]
    hs = _conv_step(ctx_t, conv_w_dw[0][:, None, :], zs, xs,
                    row(conv_b_dw[0]), row(conv_ln_g[0]), row(conv_ln_b[0]), w_out_b, row(conv_b_out[0]))
    hs = moe_ple(hs, 0, p_sample[0].reshape(nd, PLE_DIM), nd, False)
    n_pages = page_table.shape[1]
    pos_s = jnp.full((nd,), n_pages * PAGE_SIZE, jnp.int32)
    cos_s, sin_s = _rope_tables(pos_s)
    lat_s, kpe_s, _, _, qs = kvq(hs, cos_s, sin_s, nd, 1)
    ql = _qlat(qs, w_ukt_pad).reshape(nd, N_HEADS, KV_LORA)
    qp = qs.reshape(nd, N_HEADS, HEAD_PAD)[:, :, ROPE_LO:ROPE_HI]
    kpe_new = kpe_s[:, ROPE_LO:ROPE_HI]
    o_lat = _paged_attn(page_table.reshape(-1), ql, qp, lat_s.reshape(nd, 1, KV_LORA),
                        kpe_new.reshape(nd, 1, QK_ROPE), cache_latent.reshape(-1, KV_LORA),
                        jnp.transpose(cache_kpe, (0, 2, 1)))
    hs = _attn_out(o_lat.reshape(nd, N_HEADS * KV_LORA), w_uv_h, w_o_b, hs)
    y_sample = moe_ple(hs, 1, p_sample[1].reshape(nd, PLE_DIM), nd, True).reshape(nd, 1, d)
    latent_sample = lat_s.reshape(nd, 1, KV_LORA)
    kpe_sample = kpe_new.reshape(nd, 1, QK_ROPE)

    return (y_prompt, y_sample, conv_prompt, conv_sample, latent_prompt, kpe_prompt, latent_sample, kpe_sample)
```

```python
import functools
import math

import jax
import jax.numpy as jnp
from jax import lax
from jax.experimental import pallas as pl
from jax.experimental.pallas import tpu as pltpu

F32 = jnp.float32
BF16 = jnp.bfloat16

D_MODEL = 1024
PLE_DIM = 256
CONV_W = 31
N_HEADS = 16
QK_NOPE = 64
QK_ROPE = 32
V_HEAD = 64
Q_LORA = 384
KV_LORA = 256
ROPE_THETA = 10000.0
SCALE = 1.0 / math.sqrt(QK_NOPE + QK_ROPE)
LOG2E = 1.4426950408889634
N_GROUPS = 4
EXPERTS_PER_GROUP = 8
N_EXPERTS = N_GROUPS * EXPERTS_PER_GROUP
TOP_K = 2
EXPERT_FF = 512
PAGE_SIZE = 128
EPS = 1e-6

HEAD_PAD = 128
ROPE_LO = QK_NOPE
ROPE_MID = QK_NOPE + QK_ROPE // 2
ROPE_HI = QK_NOPE + QK_ROPE
HALO = 32
MOE_HALF = 128
MOE_BM = 2 * MOE_HALF
ROUTER_ROWS = 40
RANK_BITS = 20
TOK_BITS = 15
GATHER_PRIORITY = 0
SCATTER_PRIORITY = 1
VMEM_LIMIT = 56 * 1024 * 1024


def _cparams(sem):
    return pltpu.CompilerParams(dimension_semantics=sem, vmem_limit_bytes=VMEM_LIMIT)


def _rms(x, g):
    return x * lax.rsqrt(jnp.mean(x * x, axis=-1, keepdims=True) + EPS) * g


def _full(shape):
    n = len(shape)
    return pl.BlockSpec(shape, lambda *_: (0,) * n)


LANES = 128
SUBLANES = 8
TOK_ROWS = D_MODEL // LANES


def _load_token_tiles(ref, n):
    return jnp.concatenate([ref[pl.ds(s, n, stride=TOK_ROWS), :] for s in range(TOK_ROWS)], axis=1)


def _store_token_tiles(ref, x):
    n = x.shape[0]
    for s in range(TOK_ROWS):
        ref[pl.ds(s, n, stride=TOK_ROWS), :] = x[:, s * LANES:(s + 1) * LANES]


def _conv_in_kernel(x_ref, g_ref, w_ref, b_ref, z_ref):
    u = _rms(x_ref[...], g_ref[...])
    a = jnp.dot(u.astype(BF16), w_ref[...], preferred_element_type=F32) + b_ref[...]
    z_ref[...] = a[:, :D_MODEL] * jax.nn.sigmoid(a[:, D_MODEL:])


def _conv_in(x, g, w_in, b_in, tm):
    t = x.shape[0]
    return pl.pallas_call(
        _conv_in_kernel,
        grid=(t // tm,),
        in_specs=[pl.BlockSpec((tm, D_MODEL), lambda i: (i, 0)),
                  _full((1, D_MODEL)), _full((D_MODEL, 2 * D_MODEL)), _full((1, 2 * D_MODEL))],
        out_specs=pl.BlockSpec((tm, D_MODEL), lambda i: (i, 0)),
        out_shape=jax.ShapeDtypeStruct((t, D_MODEL), F32),
        compiler_params=_cparams(("parallel",)),
        name="conv_in",
    )(x, g, w_in, b_in)


def _ln_silu_out(y, lng, lnb, wout, bout):
    mu = jnp.mean(y, axis=-1, keepdims=True)
    yc = y - mu
    var = jnp.mean(yc * yc, axis=-1, keepdims=True)
    n = yc * lax.rsqrt(var + EPS) * lng + lnb
    act = n * jax.nn.sigmoid(n)
    return jnp.dot(act.astype(BF16), wout, preferred_element_type=F32) + bout


CONV_TS = 256
CONV_RC = 128
CONV_LC = 256


def _conv_mix_kernel(z_ref, halo_ref, ctx_ref, x_ref, wdw_ref, bdw_ref, lng_ref, lnb_ref,
                     wout_ref, bout_ref, h_ref, buf_ref, y_ref):
    s = pl.program_id(1)

    @pl.when(s == 0)
    def _():
        buf_ref[0, 0:HALO, :] = ctx_ref[...]

    @pl.when(s > 0)
    def _():
        buf_ref[0, 0:HALO, :] = halo_ref[...]

    buf_ref[0, HALO:, :] = z_ref[...]
    shifted_rows = CONV_TS + HALO - SUBLANES
    for sh in range(1, SUBLANES):
        buf_ref[sh, 0:shifted_rows, :] = buf_ref[0, sh:sh + shifted_rows, :]
    first = HALO - (CONV_W - 1)
    for lc in range(D_MODEL // CONV_LC):
        lanes = slice(lc * CONV_LC, (lc + 1) * CONV_LC)
        for rc in range(CONV_TS // CONV_RC):
            r0 = rc * CONV_RC
            acc = jnp.broadcast_to(bdw_ref[:, lanes], (CONV_RC, CONV_LC))
            for k in range(CONV_W):
                sh = (first + k) % SUBLANES
                a0 = r0 + first + k - sh
                acc = acc + wdw_ref[k:k + 1, lanes] * buf_ref[sh, a0:a0 + CONV_RC, lanes]
            y_ref[r0:r0 + CONV_RC, lanes] = acc
    out = _ln_silu_out(y_ref[...], lng_ref[...], lnb_ref[...], wout_ref[...], bout_ref[...])
    h_ref[...] = x_ref[...] + out


def _conv_mix(z3, ctx, x3, wdw, bdw, lng, lnb, wout, bout):
    b, s, d = z3.shape
    ts = CONV_TS
    hb = ts // HALO
    return pl.pallas_call(
        _conv_mix_kernel,
        grid=(b, s // ts),
        in_specs=[pl.BlockSpec((None, ts, d), lambda i, j: (i, j, 0)),
                  pl.BlockSpec((None, HALO, d), lambda i, j: (i, jnp.maximum(j * hb - 1, 0), 0)),
                  pl.BlockSpec((None, HALO, d), lambda i, j: (i, 0, 0)),
                  pl.BlockSpec((None, ts, d), lambda i, j: (i, j, 0)),
                  _full((HALO, d)), _full((1, d)), _full((1, d)), _full((1, d)),
                  _full((d, d)), _full((1, d))],
        out_specs=pl.BlockSpec((None, ts, d), lambda i, j: (i, j, 0)),
        out_shape=jax.ShapeDtypeStruct((b, s, d), F32),
        scratch_shapes=[pltpu.VMEM((SUBLANES, ts + HALO, d), F32), pltpu.VMEM((ts, d), F32)],
        compiler_params=_cparams(("parallel", "arbitrary")),
        name="conv_mix",
    )(z3, z3, ctx, x3, wdw, bdw, lng, lnb, wout, bout)


def _conv_step_kernel(ctx_ref, w_ref, z_ref, x_ref, bdw_ref, lng_ref, lnb_ref, wout_ref, bout_ref,
                      h_ref, acc_ref):
    k = pl.program_id(0)

    @pl.when(k == 0)
    def _():
        acc_ref[...] = jnp.broadcast_to(bdw_ref[...], acc_ref.shape)

    @pl.when(k < CONV_W - 1)
    def _():
        acc_ref[...] += w_ref[...] * ctx_ref[...]

    @pl.when(k == CONV_W - 1)
    def _():
        y = acc_ref[...] + w_ref[...] * z_ref[...]
        out = _ln_silu_out(y, lng_ref[...], lnb_ref[...], wout_ref[...], bout_ref[...])
        h_ref[...] = x_ref[...] + out


def _conv_step(ctx_t, wdw3, z, x, bdw, lng, lnb, wout, bout):
    n, d = z.shape
    last = CONV_W - 2
    return pl.pallas_call(
        _conv_step_kernel,
        grid=(CONV_W,),
        in_specs=[pl.BlockSpec((None, n, d), lambda k: (jnp.minimum(k, last), 0, 0)),
                  pl.BlockSpec((None, 1, d), lambda k: (k, 0, 0)),
                  _full((n, d)), _full((n, d)), _full((1, d)), _full((1, d)), _full((1, d)),
                  _full((d, d)), _full((1, d))],
        out_specs=_full((n, d)),
        out_shape=jax.ShapeDtypeStruct((n, d), F32),
        scratch_shapes=[pltpu.VMEM((n, d), F32)],
        compiler_params=_cparams(("arbitrary",)),
        name="conv_step",
    )(ctx_t, wdw3, z, x, bdw, lng, lnb, wout, bout)


def _router_kernel(h_ref, g_ref, wr_ref, br_ref, u_ref, code_ref, gate_ref, cnt_ref, carry_ref):
    i = pl.program_id(0)
    tm = h_ref.shape[0]

    @pl.when(i == 0)
    def _():
        carry_ref[...] = jnp.zeros_like(carry_ref)

    u = _rms(h_ref[...], g_ref[...])
    _store_token_tiles(u_ref, u)
    logits = lax.dot_general(wr_ref[...], u, (((1,), (1,)), ((), ())),
                             precision=lax.Precision.HIGHEST, preferred_element_type=F32) + br_ref[...]
    gl = logits[0:N_GROUPS]
    il = logits[8:8 + N_EXPERTS]
    gmax = jnp.max(gl, axis=0, keepdims=True)
    giota = lax.broadcasted_iota(jnp.int32, gl.shape, 0)
    gsel = jnp.min(jnp.where(gl == gmax, giota, N_GROUPS), axis=0, keepdims=True)
    gw = 1.0 / jnp.sum(jnp.exp(gl - gmax), axis=0, keepdims=True)
    eiota = lax.broadcasted_iota(jnp.int32, il.shape, 0)
    ml = jnp.where((eiota >> 3) == gsel, il, -jnp.inf)
    v1 = jnp.max(ml, axis=0, keepdims=True)
    i1 = jnp.min(jnp.where(ml == v1, eiota, N_EXPERTS), axis=0, keepdims=True)
    ml2 = jnp.where(eiota == i1, -jnp.inf, ml)
    v2 = jnp.max(ml2, axis=0, keepdims=True)
    i2 = jnp.min(jnp.where(ml2 == v2, eiota, N_EXPERTS), axis=0, keepdims=True)
    e2 = jnp.exp(v2 - v1)
    p1 = 1.0 / (1.0 + e2)
    gate_ref[0:1, :] = p1 * gw
    gate_ref[1:2, :] = e2 * p1 * gw
    hit1 = eiota == i1
    hit2 = eiota == i2
    onehot = jnp.where(hit1 | hit2, 1.0, 0.0)
    r = lax.broadcasted_iota(jnp.int32, (tm, tm), 0)
    c = lax.broadcasted_iota(jnp.int32, (tm, tm), 1)
    before = jnp.where(r < c, 1.0, 0.0).astype(BF16)
    cum = jnp.dot(onehot.astype(BF16), before, preferred_element_type=F32) + carry_ref[:, 0:1]
    rank1 = jnp.sum(jnp.where(hit1, cum, 0.0), axis=0, keepdims=True).astype(jnp.int32)
    rank2 = jnp.sum(jnp.where(hit2, cum, 0.0), axis=0, keepdims=True).astype(jnp.int32)
    code_ref[0:1, :] = (i1 << RANK_BITS) | rank1
    code_ref[1:2, :] = (i2 << RANK_BITS) | rank2
    carry_ref[...] += jnp.sum(onehot, axis=1, keepdims=True)
    cnt_ref[...] = carry_ref[...]


def _router(h, g, wr, br, tm):
    t = h.shape[0]
    row2 = pl.BlockSpec((TOP_K, tm), lambda i: (0, i))
    return pl.pallas_call(
        _router_kernel,
        grid=(t // tm,),
        in_specs=[pl.BlockSpec((tm, D_MODEL), lambda i: (i, 0)),
                  _full((1, D_MODEL)), _full((ROUTER_ROWS, D_MODEL)), _full((ROUTER_ROWS, 1))],
        out_specs=[pl.BlockSpec((tm * TOK_ROWS, LANES), lambda i: (i, 0)), row2, row2,
                   _full((N_EXPERTS, 128))],
        out_shape=[jax.ShapeDtypeStruct((t * TOK_ROWS, LANES), F32),
                   jax.ShapeDtypeStruct((TOP_K, t), jnp.int32),
                   jax.ShapeDtypeStruct((TOP_K, t), F32),
                   jax.ShapeDtypeStruct((N_EXPERTS, 128), F32)],
        scratch_shapes=[pltpu.VMEM((N_EXPERTS, 128), F32)],
        compiler_params=_cparams(("arbitrary",)),
        name="router",
    )(h, g, wr, br)


def _row_index_kernel(dest_ref, pad_lo_ref, pad_hi_ref, idx_ref, *, n_tok):
    n_pairs = dest_ref.shape[0]

    def place(p, carry):
        idx_ref[dest_ref[p]] = (p << TOK_BITS) | (p & (n_tok - 1))
        return carry

    lax.fori_loop(0, n_pairs, place, 0, unroll=8)

    def pad_range(e, q):
        def pad_row(r, q):
            idx_ref[r] = ((n_pairs + q) << TOK_BITS) | (r & (n_tok - 1))
            return q + 1
        return lax.fori_loop(pad_lo_ref[e], pad_hi_ref[e], pad_row, q)

    lax.fori_loop(0, pad_lo_ref.shape[0], pad_range, 0)


def _row_index(dest, pad_lo, pad_hi, n_tok, n_rows):
    return pl.pallas_call(
        functools.partial(_row_index_kernel, n_tok=n_tok),
        grid_spec=pltpu.PrefetchScalarGridSpec(
            num_scalar_prefetch=3,
            grid=(1,),
            in_specs=[],
            out_specs=pl.BlockSpec(memory_space=pltpu.SMEM),
        ),
        out_shape=jax.ShapeDtypeStruct((n_rows,), jnp.int32),
        compiler_params=_cparams(("arbitrary",)),
        name="moe_row_index",
    )(dest, pad_lo, pad_hi)


def _expert_kernel(blk_ref, idx_ref, u_hbm, w1_ref, w3_ref, w2_ref, yp_hbm, xbuf, ybuf, w1b, w3b, w2b,
                   gsem, ssem, *, n_steps):
    j = pl.program_id(0)
    prime0 = n_steps * MOE_BM
    half_rows = MOE_HALF * TOK_ROWS

    def tile_of(t):
        return pl.ds(pl.multiple_of(t * TOK_ROWS, TOK_ROWS), TOK_ROWS)

    def gather_start(step, half, par):
        base = step * MOE_BM + half * MOE_HALF
        for r in range(MOE_HALF):
            tok = idx_ref[base + r] & ((1 << TOK_BITS) - 1)
            pltpu.make_async_copy(u_hbm.at[tile_of(tok)], xbuf.at[par, half, tile_of(r)],
                                  gsem.at[par, half]).start(priority=GATHER_PRIORITY)

    def gather_wait(half, par):
        pltpu.make_async_copy(u_hbm.at[pl.ds(0, half_rows)], xbuf.at[par, half], gsem.at[par, half]).wait()

    def scatter_start(step, half, to_prime=None):
        base = step * MOE_BM + half * MOE_HALF
        for r in range(MOE_HALF):
            dst = lax.shift_right_logical(idx_ref[base + r], jnp.int32(TOK_BITS))
            if to_prime is not None:
                dst = jnp.where(to_prime, prime0 + half * MOE_HALF + r, dst)
            pltpu.make_async_copy(ybuf.at[half, tile_of(r)], yp_hbm.at[tile_of(dst)],
                                  ssem.at[half]).start(priority=SCATTER_PRIORITY)

    def scatter_wait(half):
        pltpu.make_async_copy(ybuf.at[half], yp_hbm.at[pl.ds(0, half_rows)], ssem.at[half]).wait()

    @pl.when(j == 0)
    def _():
        ybuf[...] = jnp.zeros_like(ybuf)
        pltpu.make_async_copy(ybuf.at[0], yp_hbm.at[pl.ds(prime0 * TOK_ROWS, half_rows)], ssem.at[0]).start()
        gather_start(0, 0, 0)
        gather_start(0, 1, 0)

    e = blk_ref[j]
    prev = blk_ref[jnp.maximum(j - 1, 0)]

    @pl.when((j == 0) | (e != prev))
    def _():
        w1b[...] = w1_ref[...].astype(BF16)
        w3b[...] = w3_ref[...].astype(BF16)
        w2b[...] = w2_ref[...].astype(BF16)

    def compute(half, par):
        x = _load_token_tiles(xbuf.at[par, half], MOE_HALF).astype(BF16)
        a = jnp.dot(x, w1b[...], preferred_element_type=F32)
        b = jnp.dot(x, w3b[...], preferred_element_type=F32)
        hmid = (a * jax.nn.sigmoid(a)) * b
        _store_token_tiles(ybuf.at[half], jnp.dot(hmid.astype(BF16), w2b[...], preferred_element_type=F32))

    nxt = jnp.minimum(j + 1, n_steps - 1)

    def step_body(par):
        gather_wait(0, par)
        scatter_wait(0)
        gather_start(nxt, 0, 1 - par)
        scatter_start(jnp.maximum(j - 1, 0), 1, to_prime=(j == 0))
        compute(0, par)
        gather_wait(1, par)
        scatter_wait(1)
        gather_start(nxt, 1, 1 - par)
        scatter_start(j, 0)
        compute(1, par)

    for par in range(2):
        pl.when(j % 2 == par)(functools.partial(step_body, par))

    @pl.when(j == n_steps - 1)
    def _():
        spare = n_steps % 2
        scatter_start(j, 1)
        gather_wait(0, spare)
        gather_wait(1, spare)
        scatter_wait(0)
        scatter_wait(1)


def _experts(blk_e, idx, u, w1, w3, w2, layer, n_steps, n_out_rows):
    wspec = lambda s: pl.BlockSpec((None, None) + s, lambda j, b, i: (layer, b[j], 0, 0))
    return pl.pallas_call(
        functools.partial(_expert_kernel, n_steps=n_steps),
        grid_spec=pltpu.PrefetchScalarGridSpec(
            num_scalar_prefetch=2,
            grid=(n_steps,),
            in_specs=[pl.BlockSpec(memory_space=pl.ANY),
                      wspec((D_MODEL, EXPERT_FF)), wspec((D_MODEL, EXPERT_FF)), wspec((EXPERT_FF, D_MODEL))],
            out_specs=pl.BlockSpec(memory_space=pl.ANY),
            scratch_shapes=[pltpu.VMEM((2, 2, MOE_HALF * TOK_ROWS, LANES), F32),
                            pltpu.VMEM((2, MOE_HALF * TOK_ROWS, LANES), F32),
                            pltpu.VMEM((D_MODEL, EXPERT_FF), BF16),
                            pltpu.VMEM((D_MODEL, EXPERT_FF), BF16),
                            pltpu.VMEM((EXPERT_FF, D_MODEL), BF16),
                            pltpu.SemaphoreType.DMA((2, 2)),
                            pltpu.SemaphoreType.DMA((2,))],
        ),
        out_shape=jax.ShapeDtypeStruct((n_out_rows * TOK_ROWS, LANES), F32),
        compiler_params=_cparams(("arbitrary",)),
        name="moe_experts",
    )(blk_e, idx, u, w1, w3, w2)


def _combine_kernel(y0_ref, y1_ref, h_ref, gate_ref, p_ref, g_ref, wg_ref, bg_ref, wp_ref, gf_ref,
                    o_ref, *, final):
    gate = gate_ref[...]
    tm = h_ref.shape[0]
    y0 = _load_token_tiles(y0_ref, tm)
    y1 = _load_token_tiles(y1_ref, tm)
    h = h_ref[...] + y0 * gate[:, 0:1] + y1 * gate[:, 1:2]
    u = _rms(h, g_ref[...])
    sg = jax.nn.sigmoid(jnp.dot(u.astype(BF16), wg_ref[...], preferred_element_type=F32) + bg_ref[...])
    pp = jnp.dot(p_ref[...].astype(BF16), wp_ref[...], preferred_element_type=F32)
    h = h + sg * pp
    if final:
        h = _rms(h, gf_ref[...])
    o_ref[...] = h


def _combine(yp, h, gate_t, p, g, wg, bg, wp, gf, tm, final):
    t = h.shape[0]
    nt = t // tm
    return pl.pallas_call(
        functools.partial(_combine_kernel, final=final),
        grid=(nt,),
        in_specs=[pl.BlockSpec((tm * TOK_ROWS, LANES), lambda i: (i, 0)),
                  pl.BlockSpec((tm * TOK_ROWS, LANES), lambda i: (nt + i, 0)),
                  pl.BlockSpec((tm, D_MODEL), lambda i: (i, 0)),
                  pl.BlockSpec((tm, TOP_K), lambda i: (i, 0)),
                  pl.BlockSpec((tm, PLE_DIM), lambda i: (i, 0)),
                  _full((1, D_MODEL)), _full((D_MODEL, D_MODEL)), _full((1, D_MODEL)),
                  _full((PLE_DIM, D_MODEL)), _full((1, D_MODEL))],
        out_specs=pl.BlockSpec((tm, D_MODEL), lambda i: (i, 0)),
        out_shape=jax.ShapeDtypeStruct((t, D_MODEL), F32),
        compiler_params=_cparams(("parallel",)),
        name="moe_combine_ple",
    )(yp, yp, h, gate_t, p, g, wg, bg, wp, gf)


def _rope_group(x, cos, sin, lane):
    half = QK_ROPE // 2
    rot = jnp.where(lane < ROPE_MID, -pltpu.roll(x, HEAD_PAD - half, 1), pltpu.roll(x, half, 1))
    return x * cos + rot * sin


def _kvq_kernel(h_ref, gkv_ref, gmix_ref, wdkv_ref, glat_ref, wkr_ref, wuk_ref, wuv_ref,
                wdq_ref, gq_ref, wuq_ref, cos_ref, sin_ref,
                lat_ref, kpe_ref, kcat_ref, v_ref, q_ref):
    h = h_ref[...]
    tm = h.shape[0]
    cos = cos_ref[...]
    sin = sin_ref[...]
    lane = lax.broadcasted_iota(jnp.int32, (tm, HEAD_PAD), 1)
    ukv = _rms(h, gkv_ref[...]).astype(BF16)
    c = _rms(jnp.dot(ukv, wdkv_ref[...], preferred_element_type=F32), glat_ref[...])
    lat_ref[...] = c
    kpe = _rope_group(jnp.dot(ukv, wkr_ref[...], preferred_element_type=F32), cos, sin, lane)
    kpe_ref[...] = kpe
    cb = c.astype(BF16)
    v_ref[...] = jnp.dot(cb, wuv_ref[...], preferred_element_type=F32).astype(BF16)
    kn = jnp.dot(cb, wuk_ref[...], preferred_element_type=F32)
    umix = _rms(h, gmix_ref[...]).astype(BF16)
    cq = _rms(jnp.dot(umix, wdq_ref[...], preferred_element_type=F32), gq_ref[...])
    q = jnp.dot(cq.astype(BF16), wuq_ref[...], preferred_element_type=F32)
    for hd in range(N_HEADS):
        lanes = slice(hd * HEAD_PAD, (hd + 1) * HEAD_PAD)
        kcat_ref[:, lanes] = (kn[:, lanes] + kpe).astype(BF16)
        q_ref[:, lanes] = (_rope_group(q[:, lanes], cos, sin, lane) * (SCALE * LOG2E)).astype(BF16)


def _kvq(h, gkv, gmix, wdkv, glat, wkr, wuk, wuv, wdq, gq, wuq, cos, sin, tm, n_pos_blocks):
    t = h.shape[0]
    row = lambda w: pl.BlockSpec((tm, w), lambda i: (i, 0))
    return pl.pallas_call(
        _kvq_kernel,
        grid=(t // tm,),
        in_specs=[row(D_MODEL), _full((1, D_MODEL)), _full((1, D_MODEL)),
                  _full((D_MODEL, KV_LORA)), _full((1, KV_LORA)), _full((D_MODEL, HEAD_PAD)),
                  _full((KV_LORA, N_HEADS * HEAD_PAD)), _full((KV_LORA, N_HEADS * V_HEAD)),
                  _full((D_MODEL, Q_LORA)), _full((1, Q_LORA)), _full((Q_LORA, N_HEADS * HEAD_PAD)),
                  pl.BlockSpec((tm, HEAD_PAD), lambda i: (i % n_pos_blocks, 0)),
                  pl.BlockSpec((tm, HEAD_PAD), lambda i: (i % n_pos_blocks, 0))],
        out_specs=[row(KV_LORA), row(HEAD_PAD), row(N_HEADS * HEAD_PAD), row(N_HEADS * V_HEAD),
                   row(N_HEADS * HEAD_PAD)],
        out_shape=[jax.ShapeDtypeStruct((t, KV_LORA), F32),
                   jax.ShapeDtypeStruct((t, HEAD_PAD), F32),
                   jax.ShapeDtypeStruct((t, N_HEADS * HEAD_PAD), BF16),
                   jax.ShapeDtypeStruct((t, N_HEADS * V_HEAD), BF16),
                   jax.ShapeDtypeStruct((t, N_HEADS * HEAD_PAD), BF16)],
        compiler_params=_cparams(("parallel",)),
        name="kvq_proj",
    )(h, gkv, gmix, wdkv, glat, wkr, wuk, wuv, wdq, gq, wuq, cos, sin)


ATT_TQ = 512
ATT_TK = 256


def _flash_kernel(q_ref, k_ref, v_ref, o_ref):
    s_len = q_ref.shape[0]
    tq, tk = ATT_TQ, ATT_TK
    row = lax.broadcasted_iota(jnp.int32, (tq, tk), 0)
    col = lax.broadcasted_iota(jnp.int32, (tq, tk), 1)
    lane = lax.broadcasted_iota(jnp.int32, (tq, 2 * V_HEAD), 1)
    nt = (((1,), (1,)), ((), ()))
    for qi in range(s_len // tq):
        q0 = qi * tq
        qs = [q_ref[q0:q0 + tq, hh * HEAD_PAD:(hh + 1) * HEAD_PAD] for hh in range(2)]
        m = [jnp.full((tq, 1), -jnp.inf, F32) for _ in range(2)]
        l = [jnp.zeros((tq, 1), F32) for _ in range(2)]
        acc = [jnp.zeros((tq, 2 * V_HEAD), F32) for _ in range(2)]
        for kj in range((q0 + tq) // tk):
            k0 = kj * tk
            v = v_ref[k0:k0 + tk, :]
            for hh in range(2):
                k = k_ref[k0:k0 + tk, hh * HEAD_PAD:(hh + 1) * HEAD_PAD]
                s = lax.dot_general(qs[hh], k, nt, preferred_element_type=F32)
                if k0 + tk - 1 > q0:
                    s = jnp.where(col + k0 <= row + q0, s, -jnp.inf)
                m_new = jnp.maximum(m[hh], jnp.max(s, axis=-1, keepdims=True))
                alpha = jnp.exp2(m[hh] - m_new)
                p = jnp.exp2(s - m_new)
                l[hh] = alpha * l[hh] + jnp.sum(p, axis=-1, keepdims=True)
                acc[hh] = alpha * acc[hh] + jnp.dot(p.astype(BF16), v, preferred_element_type=F32)
                m[hh] = m_new
        o = jnp.where(lane < V_HEAD, acc[0] / l[0], acc[1] / l[1])
        o_ref[q0:q0 + tq, :] = o.astype(BF16)


def _flash(q, kcat, v, b, s):
    return pl.pallas_call(
        _flash_kernel,
        grid=(b, N_HEADS // 2),
        in_specs=[pl.BlockSpec((s, 2 * HEAD_PAD), lambda i, h: (i, h)),
                  pl.BlockSpec((s, 2 * HEAD_PAD), lambda i, h: (i, h)),
                  pl.BlockSpec((s, 2 * V_HEAD), lambda i, h: (i, h))],
        out_specs=pl.BlockSpec((s, 2 * V_HEAD), lambda i, h: (i, h)),
        out_shape=jax.ShapeDtypeStruct((b * s, N_HEADS * V_HEAD), BF16),
        compiler_params=_cparams(("parallel", "parallel")),
        name="flash_attn",
    )(q, kcat, v)


def _proj_res_kernel(a_ref, w_ref, h_ref, o_ref):
    o_ref[...] = h_ref[...] + jnp.dot(a_ref[...], w_ref[...], preferred_element_type=F32)


def _proj_res(a, w, h, tm):
    t, k = a.shape
    return pl.pallas_call(
        _proj_res_kernel,
        grid=(t // tm,),
        in_specs=[pl.BlockSpec((tm, k), lambda i: (i, 0)), _full(w.shape),
                  pl.BlockSpec((tm, D_MODEL), lambda i: (i, 0))],
        out_specs=pl.BlockSpec((tm, D_MODEL), lambda i: (i, 0)),
        out_shape=jax.ShapeDtypeStruct((t, D_MODEL), F32),
        compiler_params=_cparams(("parallel",)),
        name="proj_residual",
    )(a, w, h)


def _qlat_kernel(q_ref, wukt_ref, o_ref):
    for hd in range(N_HEADS):
        qh = q_ref[:, hd * HEAD_PAD:(hd + 1) * HEAD_PAD]
        o_ref[:, hd * KV_LORA:(hd + 1) * KV_LORA] = jnp.dot(
            qh, wukt_ref[hd], preferred_element_type=F32).astype(BF16)


def _qlat(q, wukt):
    n = q.shape[0]
    return pl.pallas_call(
        _qlat_kernel,
        in_specs=[_full(q.shape), _full(wukt.shape)],
        out_specs=_full((n, N_HEADS * KV_LORA)),
        out_shape=jax.ShapeDtypeStruct((n, N_HEADS * KV_LORA), BF16),
        grid=(1,),
        compiler_params=_cparams(("arbitrary",)),
        name="q_absorb",
    )(q, wukt)


PAGES_PER_CHUNK = 32
CHUNK_SLOTS = 4
CHUNK_AHEAD = 2


def _paged_kernel(pt_ref, ql_ref, qp_ref, cn_ref, kn_ref, lat_hbm, kpt_hbm, o_ref, lbuf, kbuf, sems, *, nb):
    b = pl.program_id(0)
    n_pages = pt_ref.shape[0] // nb
    cp = PAGES_PER_CHUNK
    n_chunks = n_pages // cp
    assert n_chunks == CHUNK_SLOTS
    total = nb * n_chunks

    def start_chunk(g, slot):
        gg = jnp.where(g < total, g, 0)
        for j in range(cp):
            page = pt_ref[gg * cp + j]
            src = pl.ds(pl.multiple_of(page * PAGE_SIZE, PAGE_SIZE), PAGE_SIZE)
            pltpu.make_async_copy(lat_hbm.at[src], lbuf.at[slot, pl.ds(j * PAGE_SIZE, PAGE_SIZE)],
                                  sems.at[0, slot]).start()
            pltpu.make_async_copy(kpt_hbm.at[page], kbuf.at[slot, j], sems.at[1, slot]).start(priority=1)

    def wait_chunk(g, slot):
        del g
        pltpu.make_async_copy(lat_hbm.at[pl.ds(0, cp * PAGE_SIZE)], lbuf.at[slot], sems.at[0, slot]).wait()
        pltpu.make_async_copy(kpt_hbm.at[pl.ds(0, cp)], kbuf.at[slot], sems.at[1, slot]).wait()

    @pl.when(b == 0)
    def _():
        for a in range(CHUNK_AHEAD):
            start_chunk(a, a)

    ql = ql_ref[...]
    qp = qp_ref[...]
    nt = (((1,), (1,)), ((), ()))
    m = jnp.full((N_HEADS, 1), -jnp.inf, F32)
    l = jnp.zeros((N_HEADS, 1), F32)
    acc = jnp.zeros((N_HEADS, KV_LORA), F32)
    for c in range(n_chunks):
        g = b * n_chunks + c
        if c % CHUNK_AHEAD == 0:
            for a in range(CHUNK_AHEAD):
                start_chunk(g + CHUNK_AHEAD + a, (c + CHUNK_AHEAD + a) % CHUNK_SLOTS)
            for a in range(CHUNK_AHEAD):
                wait_chunk(g + a, c + a)
        kc = lbuf[c].astype(BF16)
        kp = jnp.concatenate([kbuf[c, jj] for jj in range(cp)], axis=1).astype(BF16)
        s = (lax.dot_general(ql, kc, nt, preferred_element_type=F32)
             + jnp.dot(qp, kp, preferred_element_type=F32))
        m_new = jnp.maximum(m, jnp.max(s, axis=-1, keepdims=True))
        alpha = jnp.exp2(m - m_new)
        p = jnp.exp2(s - m_new)
        l = alpha * l + jnp.sum(p, axis=-1, keepdims=True)
        acc = alpha * acc + jnp.dot(p.astype(BF16), kc, preferred_element_type=F32)
        m = m_new
    cn = cn_ref[...].astype(BF16).astype(F32)
    kn = kn_ref[...].astype(BF16).astype(F32)
    s = (jnp.sum(ql.astype(F32) * cn, axis=-1, keepdims=True)
         + jnp.sum(qp.astype(F32) * kn, axis=-1, keepdims=True))
    m_new = jnp.maximum(m, s)
    alpha = jnp.exp2(m - m_new)
    p = jnp.exp2(s - m_new)
    l = alpha * l + p
    acc = alpha * acc + p.astype(BF16).astype(F32) * cn
    o_ref[...] = acc / l

    @pl.when(b == nb - 1)
    def _():
        for a in range(CHUNK_AHEAD):
            wait_chunk(total + a, a)


def _paged_attn(pt_flat, ql, qp, c_new, k_new, cache_latent, cache_kpt):
    n = ql.shape[0]
    rows = PAGES_PER_CHUNK * PAGE_SIZE
    per_seq = lambda w: pl.BlockSpec((None,) + w, lambda i, pt: (i, 0, 0))
    return pl.pallas_call(
        functools.partial(_paged_kernel, nb=n),
        grid_spec=pltpu.PrefetchScalarGridSpec(
            num_scalar_prefetch=1,
            grid=(n,),
            in_specs=[per_seq((N_HEADS, KV_LORA)), per_seq((N_HEADS, QK_ROPE)),
                      per_seq((1, KV_LORA)), per_seq((1, QK_ROPE)),
                      pl.BlockSpec(memory_space=pl.ANY), pl.BlockSpec(memory_space=pl.ANY)],
            out_specs=per_seq((N_HEADS, KV_LORA)),
            scratch_shapes=[pltpu.VMEM((CHUNK_SLOTS, rows, KV_LORA), F32),
                            pltpu.VMEM((CHUNK_SLOTS, PAGES_PER_CHUNK, QK_ROPE, PAGE_SIZE), F32),
                            pltpu.SemaphoreType.DMA((2, CHUNK_SLOTS))],
        ),
        out_shape=jax.ShapeDtypeStruct((n, N_HEADS, KV_LORA), F32),
        compiler_params=_cparams(("arbitrary",)),
        name="paged_attn",
    )(pt_flat, ql, qp, c_new, k_new, cache_latent, cache_kpt)


def _attn_out_kernel(ol_ref, wuv_ref, wo_ref, h_ref, o_ref):
    acc = h_ref[...]
    for hd in range(N_HEADS):
        oh = ol_ref[:, hd * KV_LORA:(hd + 1) * KV_LORA].astype(BF16)
        th = jnp.dot(oh, wuv_ref[hd], preferred_element_type=F32).astype(BF16)
        acc = acc + jnp.dot(th, wo_ref[hd * V_HEAD:(hd + 1) * V_HEAD, :], preferred_element_type=F32)
    o_ref[...] = acc


def _attn_out(ol, wuv_h, wo, h):
    n = h.shape[0]
    return pl.pallas_call(
        _attn_out_kernel,
        grid=(1,),
        in_specs=[_full(ol.shape), _full(wuv_h.shape), _full(wo.shape), _full(h.shape)],
        out_specs=_full(h.shape),
        out_shape=jax.ShapeDtypeStruct((n, D_MODEL), F32),
        compiler_params=_cparams(("arbitrary",)),
        name="attn_out",
    )(ol, wuv_h, wo, h)


def _rope_tables(pos):
    half = QK_ROPE // 2
    inv = ROPE_THETA ** (-jnp.arange(half, dtype=F32) / half)
    ang = pos.astype(F32)[:, None] * inv[None, :]
    cos, sin = jnp.cos(ang), jnp.sin(ang)
    n = pos.shape[0]
    cos_t = jnp.concatenate([jnp.ones((n, QK_NOPE), F32), cos, cos,
                             jnp.ones((n, HEAD_PAD - ROPE_HI), F32)], axis=1)
    sin_t = jnp.concatenate([jnp.zeros((n, QK_NOPE), F32), sin, sin,
                             jnp.zeros((n, HEAD_PAD - ROPE_HI), F32)], axis=1)
    return cos_t, sin_t


def _moe(h, g_ffn, wr, br, w1, w3, w2, layer, tm):
    t = h.shape[0]
    u, code, gate, cnt = _router(h, g_ffn, wr, br, tm)
    counts = cnt[:, 0].astype(jnp.int32)
    padded = (counts + MOE_BM - 1) // MOE_BM * MOE_BM
    pends = jnp.cumsum(padded)
    pstarts = pends - padded
    n_steps = (t * TOP_K + N_EXPERTS * (MOE_BM - 1) + MOE_BM - 1) // MOE_BM
    n_rows = n_steps * MOE_BM
    eid = code >> RANK_BITS
    eiota = jnp.arange(N_EXPERTS, dtype=jnp.int32)
    start_of = jnp.sum(jnp.where(eid[..., None] == eiota, pstarts, 0), axis=-1)
    dest = (start_of + (code & ((1 << RANK_BITS) - 1))).reshape(-1)
    block_start = jnp.arange(n_steps, dtype=jnp.int32) * MOE_BM
    blk_e = jnp.minimum(jnp.sum((pends[None, :] <= block_start[:, None]).astype(jnp.int32), axis=1),
                        N_EXPERTS - 1)
    n_out = n_rows + MOE_BM
    assert t & (t - 1) == 0 and t <= (1 << TOK_BITS) and n_out <= (1 << (32 - TOK_BITS))
    pad_lo = jnp.concatenate([pstarts + counts, pends[-1:]])
    pad_hi = jnp.concatenate([pends, jnp.full((1,), n_rows, jnp.int32)])
    idx = _row_index(dest, pad_lo, pad_hi, t, n_rows)
    yp = _experts(blk_e, idx, u, w1, w3, w2, layer, n_steps, n_out)
    return yp, gate.T


def kernel(x_prompt, x_sample, cache_conv, cache_latent, cache_kpe, page_table, p_prompt, p_sample, g_mix, g_ffn, g_ple, g_final, conv_w_in, conv_b_in, conv_w_dw, conv_b_dw, conv_ln_g, conv_ln_b, conv_w_out, conv_b_out, g_kv_in, w_dkv, g_kv_lat, w_kr, w_uk, w_uv, w_dq, g_q_lat, w_uq, w_o, w_group, b_group, w_inner, b_inner, w_exp_gate, w_exp_up, w_exp_down, w_ple_gate, b_ple_gate, w_ple_proj):
    bp, sp, d = x_prompt.shape
    nd = x_sample.shape[0]
    tp = bp * sp
    row = lambda a: a.reshape(1, -1)

    w_in_b = conv_w_in[0].astype(BF16)
    w_out_b = conv_w_out[0].astype(BF16)
    wdw_pad = jnp.concatenate([conv_w_dw[0], jnp.zeros((HALO - CONV_W, d), F32)], axis=0)
    wr, br = [], []
    for i in range(2):
        wr.append(jnp.concatenate([w_group[i].T, jnp.zeros((8 - N_GROUPS, d), F32),
                                   w_inner[i].reshape(d, N_EXPERTS).T], axis=0))
        br.append(jnp.concatenate([b_group[i], jnp.zeros((8 - N_GROUPS,), F32),
                                   b_inner[i].reshape(N_EXPERTS)]).reshape(ROUTER_ROWS, 1))
    w_dkv_b = w_dkv.astype(BF16)
    w_kr_pad = jnp.zeros((d, HEAD_PAD), F32).at[:, ROPE_LO:ROPE_HI].set(w_kr).astype(BF16)
    w_uk_pad = jnp.pad(w_uk, ((0, 0), (0, 0), (0, HEAD_PAD - QK_NOPE))).reshape(KV_LORA, -1).astype(BF16)
    w_uv_b = w_uv.reshape(KV_LORA, -1).astype(BF16)
    w_dq_b = w_dq[0].astype(BF16)
    w_uq_pad = jnp.pad(w_uq[0].reshape(Q_LORA, N_HEADS, QK_NOPE + QK_ROPE),
                       ((0, 0), (0, 0), (0, HEAD_PAD - ROPE_HI))).reshape(Q_LORA, -1).astype(BF16)
    w_o_b = w_o[0].astype(BF16)
    w_ukt_pad = jnp.pad(jnp.transpose(w_uk, (1, 2, 0)),
                        ((0, 0), (0, HEAD_PAD - QK_NOPE), (0, 0))).astype(BF16)
    w_uv_h = jnp.transpose(w_uv, (1, 0, 2)).astype(BF16)
    wg_b = [w_ple_gate[i].astype(BF16) for i in range(2)]
    wp_b = [w_ple_proj[i].astype(BF16) for i in range(2)]

    def moe_ple(h, layer, p, tm, final):
        yp, gate_t = _moe(h, row(g_ffn[layer]), wr[layer], br[layer],
                          w_exp_gate, w_exp_up, w_exp_down, layer, tm)
        return _combine(yp, h, gate_t, p, row(g_ple[layer]), wg_b[layer], row(b_ple_gate[layer]),
                        wp_b[layer], row(g_final), tm, final)

    def kvq(h, cos_t, sin_t, tm, n_pos_blocks):
        return _kvq(h, row(g_kv_in), row(g_mix[1]), w_dkv_b, row(g_kv_lat), w_kr_pad, w_uk_pad, w_uv_b,
                    w_dq_b, row(g_q_lat[0]), w_uq_pad, cos_t, sin_t, tm, n_pos_blocks)

    xp = x_prompt.reshape(tp, d)
    z = _conv_in(xp, row(g_mix[0]), w_in_b, row(conv_b_in[0]), 512)
    z3 = z.reshape(bp, sp, d)
    conv_prompt = z3[:, sp - (CONV_W - 1):, :][None]
    ctx0 = jnp.zeros((bp, HALO, d), F32)
    h = _conv_mix(z3, ctx0, x_prompt, wdw_pad, row(conv_b_dw[0]), row(conv_ln_g[0]), row(conv_ln_b[0]),
                  w_out_b, row(conv_b_out[0])).reshape(tp, d)
    h = moe_ple(h, 0, p_prompt[0].reshape(tp, PLE_DIM), 512, False)
    cos_p, sin_p = _rope_tables(jnp.arange(sp, dtype=jnp.int32))
    lat_p, kpe_p, kcat, v, q = kvq(h, cos_p, sin_p, 512, sp // 512)
    o = _flash(q, kcat, v, bp, sp)
    h = _proj_res(o, w_o_b, h, 512)
    y_prompt = moe_ple(h, 1, p_prompt[1].reshape(tp, PLE_DIM), 512, True).reshape(bp, sp, d)
    latent_prompt = lat_p.reshape(bp, sp, KV_LORA)
    kpe_prompt = kpe_p[:, ROPE_LO:ROPE_HI].reshape(bp, sp, QK_ROPE)

    xs = x_sample.reshape(nd, d)
    zs = _conv_in(xs, row(g_mix[0]), w_in_b, row(conv_b_in[0]), nd)
    ctx_t = jnp.transpose(cache_conv[0], (1, 0, 2))
    conv_sample = jnp.transpose(jnp.concatenate([ctx_t[1:], zs[None]], axis=0), (1, 0, 2))[None]
    hs = _conv_step(ctx_t, conv_w_dw[0][:, None, :], zs, xs,
                    row(conv_b_dw[0]), row(conv_ln_g[0]), row(conv_ln_b[0]), w_out_b, row(conv_b_out[0]))
    hs = moe_ple(hs, 0, p_sample[0].reshape(nd, PLE_DIM), nd, False)
    n_pages = page_table.shape[1]
    pos_s = jnp.full((nd,), n_pages * PAGE_SIZE, jnp.int32)
    cos_s, sin_s = _rope_tables(pos_s)
    lat_s, kpe_s, _, _, qs = kvq(hs, cos_s, sin_s, nd, 1)
    ql = _qlat(qs, w_ukt_pad).reshape(nd, N_HEADS, KV_LORA)
    qp = qs.reshape(nd, N_HEADS, HEAD_PAD)[:, :, ROPE_LO:ROPE_HI]
    kpe_new = kpe_s[:, ROPE_LO:ROPE_HI]
    o_lat = _paged_attn(page_table.reshape(-1), ql, qp, lat_s.reshape(nd, 1, KV_LORA),
                        kpe_new.reshape(nd, 1, QK_ROPE), cache_latent.reshape(-1, KV_LORA),
                        jnp.transpose(cache_kpe, (0, 2, 1)))
    hs = _attn_out(o_lat.reshape(nd, N_HEADS * KV_LORA), w_uv_h, w_o_b, hs)
    y_sample = moe_ple(hs, 1, p_sample[1].reshape(nd, PLE_DIM), nd, True).reshape(nd, 1, d)
    latent_sample = lat_s.reshape(nd, 1, KV_LORA)
    kpe_sample = kpe_new.reshape(nd, 1, QK_ROPE)

    return (y_prompt, y_sample, conv_prompt, conv_sample, latent_prompt, kpe_prompt, latent_sample, kpe_sample)
```

```python
import functools
import math

import jax
import jax.numpy as jnp
from jax import lax
from jax.experimental import pallas as pl
from jax.experimental.pallas import tpu as pltpu

F32 = jnp.float32
BF16 = jnp.bfloat16

D_MODEL = 1024
PLE_DIM = 256
CONV_W = 31
N_HEADS = 16
QK_NOPE = 64
QK_ROPE = 32
V_HEAD = 64
Q_LORA = 384
KV_LORA = 256
ROPE_THETA = 10000.0
SCALE = 1.0 / math.sqrt(QK_NOPE + QK_ROPE)
LOG2E = 1.4426950408889634
N_GROUPS = 4
EXPERTS_PER_GROUP = 8
N_EXPERTS = N_GROUPS * EXPERTS_PER_GROUP
TOP_K = 2
EXPERT_FF = 512
PAGE_SIZE = 128
EPS = 1e-6

HEAD_PAD = 128
ROPE_LO = QK_NOPE
ROPE_MID = QK_NOPE + QK_ROPE // 2
ROPE_HI = QK_NOPE + QK_ROPE
HALO = 32
MOE_HALF = 128
MOE_BM = 2 * MOE_HALF
ROUTER_ROWS = 40
RANK_BITS = 20
TOK_BITS = 15
VMEM_LIMIT = 56 * 1024 * 1024


def _cparams(sem):
    return pltpu.CompilerParams(dimension_semantics=sem, vmem_limit_bytes=VMEM_LIMIT)


def _rms(x, g):
    return x * lax.rsqrt(jnp.mean(x * x, axis=-1, keepdims=True) + EPS) * g


def _full(shape):
    n = len(shape)
    return pl.BlockSpec(shape, lambda *_: (0,) * n)


LANES = 128
SUBLANES = 8
TOK_ROWS = D_MODEL // LANES


def _load_token_tiles(ref, n):
    return jnp.concatenate([ref[pl.ds(s, n, stride=TOK_ROWS), :] for s in range(TOK_ROWS)], axis=1)


def _store_token_tiles(ref, x):
    n = x.shape[0]
    for s in range(TOK_ROWS):
        ref[pl.ds(s, n, stride=TOK_ROWS), :] = x[:, s * LANES:(s + 1) * LANES]


def _conv_in_kernel(x_ref, g_ref, w_ref, b_ref, z_ref):
    u = _rms(x_ref[...], g_ref[...])
    a = jnp.dot(u.astype(BF16), w_ref[...], preferred_element_type=F32) + b_ref[...]
    z_ref[...] = a[:, :D_MODEL] * jax.nn.sigmoid(a[:, D_MODEL:])


def _conv_in(x, g, w_in, b_in, tm):
    t = x.shape[0]
    return pl.pallas_call(
        _conv_in_kernel,
        grid=(t // tm,),
        in_specs=[pl.BlockSpec((tm, D_MODEL), lambda i: (i, 0)),
                  _full((1, D_MODEL)), _full((D_MODEL, 2 * D_MODEL)), _full((1, 2 * D_MODEL))],
        out_specs=pl.BlockSpec((tm, D_MODEL), lambda i: (i, 0)),
        out_shape=jax.ShapeDtypeStruct((t, D_MODEL), F32),
        compiler_params=_cparams(("parallel",)),
        name="conv_in",
    )(x, g, w_in, b_in)


def _ln_silu_out(y, lng, lnb, wout, bout):
    mu = jnp.mean(y, axis=-1, keepdims=True)
    yc = y - mu
    var = jnp.mean(yc * yc, axis=-1, keepdims=True)
    n = yc * lax.rsqrt(var + EPS) * lng + lnb
    act = n * jax.nn.sigmoid(n)
    return jnp.dot(act.astype(BF16), wout, preferred_element_type=F32) + bout


CONV_TS = 256
CONV_RC = 128
CONV_LC = 256


def _conv_mix_kernel(z_ref, halo_ref, ctx_ref, x_ref, wdw_ref, bdw_ref, lng_ref, lnb_ref,
                     wout_ref, bout_ref, h_ref, buf_ref, y_ref):
    s = pl.program_id(1)

    @pl.when(s == 0)
    def _():
        buf_ref[0, 0:HALO, :] = ctx_ref[...]

    @pl.when(s > 0)
    def _():
        buf_ref[0, 0:HALO, :] = halo_ref[...]

    buf_ref[0, HALO:, :] = z_ref[...]
    shifted_rows = CONV_TS + HALO - SUBLANES
    for sh in range(1, SUBLANES):
        buf_ref[sh, 0:shifted_rows, :] = buf_ref[0, sh:sh + shifted_rows, :]
    first = HALO - (CONV_W - 1)
    for lc in range(D_MODEL // CONV_LC):
        lanes = slice(lc * CONV_LC, (lc + 1) * CONV_LC)
        for rc in range(CONV_TS // CONV_RC):
            r0 = rc * CONV_RC
            acc = jnp.broadcast_to(bdw_ref[:, lanes], (CONV_RC, CONV_LC))
            for k in range(CONV_W):
                sh = (first + k) % SUBLANES
                a0 = r0 + first + k - sh
                acc = acc + wdw_ref[k:k + 1, lanes] * buf_ref[sh, a0:a0 + CONV_RC, lanes]
            y_ref[r0:r0 + CONV_RC, lanes] = acc
    out = _ln_silu_out(y_ref[...], lng_ref[...], lnb_ref[...], wout_ref[...], bout_ref[...])
    h_ref[...] = x_ref[...] + out


def _conv_mix(z3, ctx, x3, wdw, bdw, lng, lnb, wout, bout):
    b, s, d = z3.shape
    ts = CONV_TS
    hb = ts // HALO
    return pl.pallas_call(
        _conv_mix_kernel,
        grid=(b, s // ts),
        in_specs=[pl.BlockSpec((None, ts, d), lambda i, j: (i, j, 0)),
                  pl.BlockSpec((None, HALO, d), lambda i, j: (i, jnp.maximum(j * hb - 1, 0), 0)),
                  pl.BlockSpec((None, HALO, d), lambda i, j: (i, 0, 0)),
                  pl.BlockSpec((None, ts, d), lambda i, j: (i, j, 0)),
                  _full((HALO, d)), _full((1, d)), _full((1, d)), _full((1, d)),
                  _full((d, d)), _full((1, d))],
        out_specs=pl.BlockSpec((None, ts, d), lambda i, j: (i, j, 0)),
        out_shape=jax.ShapeDtypeStruct((b, s, d), F32),
        scratch_shapes=[pltpu.VMEM((SUBLANES, ts + HALO, d), F32), pltpu.VMEM((ts, d), F32)],
        compiler_params=_cparams(("parallel", "arbitrary")),
        name="conv_mix",
    )(z3, z3, ctx, x3, wdw, bdw, lng, lnb, wout, bout)


def _conv_step_kernel(ctx_ref, w_ref, z_ref, x_ref, bdw_ref, lng_ref, lnb_ref, wout_ref, bout_ref,
                      h_ref, acc_ref):
    k = pl.program_id(0)

    @pl.when(k == 0)
    def _():
        acc_ref[...] = jnp.broadcast_to(bdw_ref[...], acc_ref.shape)

    @pl.when(k < CONV_W - 1)
    def _():
        acc_ref[...] += w_ref[...] * ctx_ref[...]

    @pl.when(k == CONV_W - 1)
    def _():
        y = acc_ref[...] + w_ref[...] * z_ref[...]
        out = _ln_silu_out(y, lng_ref[...], lnb_ref[...], wout_ref[...], bout_ref[...])
        h_ref[...] = x_ref[...] + out


def _conv_step(ctx_t, wdw3, z, x, bdw, lng, lnb, wout, bout):
    n, d = z.shape
    last = CONV_W - 2
    return pl.pallas_call(
        _conv_step_kernel,
        grid=(CONV_W,),
        in_specs=[pl.BlockSpec((None, n, d), lambda k: (jnp.minimum(k, last), 0, 0)),
                  pl.BlockSpec((None, 1, d), lambda k: (k, 0, 0)),
                  _full((n, d)), _full((n, d)), _full((1, d)), _full((1, d)), _full((1, d)),
                  _full((d, d)), _full((1, d))],
        out_specs=_full((n, d)),
        out_shape=jax.ShapeDtypeStruct((n, d), F32),
        scratch_shapes=[pltpu.VMEM((n, d), F32)],
        compiler_params=_cparams(("arbitrary",)),
        name="conv_step",
    )(ctx_t, wdw3, z, x, bdw, lng, lnb, wout, bout)


def _router_kernel(h_ref, g_ref, wr_ref, br_ref, u_ref, code_ref, gate_ref, cnt_ref, carry_ref):
    i = pl.program_id(0)
    tm = h_ref.shape[0]

    @pl.when(i == 0)
    def _():
        carry_ref[...] = jnp.zeros_like(carry_ref)

    u = _rms(h_ref[...], g_ref[...])
    _store_token_tiles(u_ref, u)
    logits = lax.dot_general(wr_ref[...], u, (((1,), (1,)), ((), ())),
                             precision=lax.Precision.HIGHEST, preferred_element_type=F32) + br_ref[...]
    gl = logits[0:N_GROUPS]
    il = logits[8:8 + N_EXPERTS]
    gmax = jnp.max(gl, axis=0, keepdims=True)
    giota = lax.broadcasted_iota(jnp.int32, gl.shape, 0)
    gsel = jnp.min(jnp.where(gl == gmax, giota, N_GROUPS), axis=0, keepdims=True)
    gw = 1.0 / jnp.sum(jnp.exp(gl - gmax), axis=0, keepdims=True)
    eiota = lax.broadcasted_iota(jnp.int32, il.shape, 0)
    ml = jnp.where((eiota >> 3) == gsel, il, -jnp.inf)
    v1 = jnp.max(ml, axis=0, keepdims=True)
    i1 = jnp.min(jnp.where(ml == v1, eiota, N_EXPERTS), axis=0, keepdims=True)
    ml2 = jnp.where(eiota == i1, -jnp.inf, ml)
    v2 = jnp.max(ml2, axis=0, keepdims=True)
    i2 = jnp.min(jnp.where(ml2 == v2, eiota, N_EXPERTS), axis=0, keepdims=True)
    e2 = jnp.exp(v2 - v1)
    p1 = 1.0 / (1.0 + e2)
    gate_ref[0:1, :] = p1 * gw
    gate_ref[1:2, :] = e2 * p1 * gw
    hit1 = eiota == i1
    hit2 = eiota == i2
    onehot = jnp.where(hit1 | hit2, 1.0, 0.0)
    r = lax.broadcasted_iota(jnp.int32, (tm, tm), 0)
    c = lax.broadcasted_iota(jnp.int32, (tm, tm), 1)
    before = jnp.where(r < c, 1.0, 0.0).astype(BF16)
    cum = jnp.dot(onehot.astype(BF16), before, preferred_element_type=F32) + carry_ref[:, 0:1]
    rank1 = jnp.sum(jnp.where(hit1, cum, 0.0), axis=0, keepdims=True).astype(jnp.int32)
    rank2 = jnp.sum(jnp.where(hit2, cum, 0.0), axis=0, keepdims=True).astype(jnp.int32)
    code_ref[0:1, :] = (i1 << RANK_BITS) | rank1
    code_ref[1:2, :] = (i2 << RANK_BITS) | rank2
    carry_ref[...] += jnp.sum(onehot, axis=1, keepdims=True)
    cnt_ref[...] = carry_ref[...]


def _router(h, g, wr, br, tm):
    t = h.shape[0]
    row2 = pl.BlockSpec((TOP_K, tm), lambda i: (0, i))
    return pl.pallas_call(
        _router_kernel,
        grid=(t // tm,),
        in_specs=[pl.BlockSpec((tm, D_MODEL), lambda i: (i, 0)),
                  _full((1, D_MODEL)), _full((ROUTER_ROWS, D_MODEL)), _full((ROUTER_ROWS, 1))],
        out_specs=[pl.BlockSpec((tm * TOK_ROWS, LANES), lambda i: (i, 0)), row2, row2,
                   _full((N_EXPERTS, 128))],
        out_shape=[jax.ShapeDtypeStruct((t * TOK_ROWS, LANES), F32),
                   jax.ShapeDtypeStruct((TOP_K, t), jnp.int32),
                   jax.ShapeDtypeStruct((TOP_K, t), F32),
                   jax.ShapeDtypeStruct((N_EXPERTS, 128), F32)],
        scratch_shapes=[pltpu.VMEM((N_EXPERTS, 128), F32)],
        compiler_params=_cparams(("arbitrary",)),
        name="router",
    )(h, g, wr, br)


def _expert_kernel(blk_ref, idx_ref, u_hbm, w1_ref, w3_ref, w2_ref, yp_hbm, xbuf, ybuf, w1b, w3b, w2b,
                   gsem, ssem, *, n_steps):
    j = pl.program_id(0)
    prime0 = n_steps * MOE_BM
    half_rows = MOE_HALF * TOK_ROWS

    def tile_of(t):
        return pl.ds(pl.multiple_of(t * TOK_ROWS, TOK_ROWS), TOK_ROWS)

    def gather_start(step, half, par):
        base = step * MOE_BM + half * MOE_HALF
        for r in range(MOE_HALF):
            tok = idx_ref[base + r] & ((1 << TOK_BITS) - 1)
            pltpu.make_async_copy(u_hbm.at[tile_of(tok)], xbuf.at[par, half, tile_of(r)],
                                  gsem.at[par, half]).start()

    def gather_wait(half, par):
        pltpu.make_async_copy(u_hbm.at[pl.ds(0, half_rows)], xbuf.at[par, half], gsem.at[par, half]).wait()

    def prime_rows(half, par):
        return prime0 + (2 * par + half) * MOE_HALF

    def scatter_start(step, half, par, to_prime=None):
        base = step * MOE_BM + half * MOE_HALF
        for r in range(MOE_HALF):
            dst = lax.shift_right_logical(idx_ref[base + r], jnp.int32(TOK_BITS))
            if to_prime is not None:
                dst = jnp.where(to_prime, prime_rows(half, par) + r, dst)
            pltpu.make_async_copy(ybuf.at[par, half, tile_of(r)], yp_hbm.at[tile_of(dst)],
                                  ssem.at[par, half]).start()

    def scatter_wait(half, par):
        pltpu.make_async_copy(ybuf.at[par, half], yp_hbm.at[pl.ds(0, half_rows)], ssem.at[par, half]).wait()

    @pl.when(j == 0)
    def _():
        ybuf[...] = jnp.zeros_like(ybuf)
        for par, half in ((0, 0), (0, 1), (1, 0)):
            pltpu.make_async_copy(ybuf.at[par, half],
                                  yp_hbm.at[pl.ds(prime_rows(half, par) * TOK_ROWS, half_rows)],
                                  ssem.at[par, half]).start()
        gather_start(0, 0, 0)
        gather_start(0, 1, 0)

    e = blk_ref[j]
    prev = blk_ref[jnp.maximum(j - 1, 0)]

    @pl.when((j == 0) | (e != prev))
    def _():
        w1b[...] = w1_ref[...].astype(BF16)
        w3b[...] = w3_ref[...].astype(BF16)
        w2b[...] = w2_ref[...].astype(BF16)

    def compute(half, par):
        x = _load_token_tiles(xbuf.at[par, half], MOE_HALF).astype(BF16)
        a = jnp.dot(x, w1b[...], preferred_element_type=F32)
        b = jnp.dot(x, w3b[...], preferred_element_type=F32)
        hmid = (a * jax.nn.sigmoid(a)) * b
        _store_token_tiles(ybuf.at[par, half],
                           jnp.dot(hmid.astype(BF16), w2b[...], preferred_element_type=F32))

    nxt = jnp.minimum(j + 1, n_steps - 1)

    def step_body(par):
        gather_wait(0, par)
        scatter_wait(0, par)
        gather_start(nxt, 0, 1 - par)
        scatter_start(jnp.maximum(j - 1, 0), 1, 1 - par, to_prime=(j == 0))
        compute(0, par)
        gather_wait(1, par)
        scatter_wait(1, par)
        gather_start(nxt, 1, 1 - par)
        scatter_start(j, 0, par)
        compute(1, par)

    for par in range(2):
        pl.when(j % 2 == par)(functools.partial(step_body, par))

    @pl.when(j == n_steps - 1)
    def _():
        last = (n_steps - 1) % 2
        scatter_start(j, 1, last)
        for half in range(2):
            gather_wait(half, 1 - last)
            scatter_wait(half, 1 - last)
            scatter_wait(half, last)


def _experts(blk_e, idx, u, w1, w3, w2, layer, n_steps, n_out_rows):
    wspec = lambda s: pl.BlockSpec((None, None) + s, lambda j, b, i: (layer, b[j], 0, 0))
    return pl.pallas_call(
        functools.partial(_expert_kernel, n_steps=n_steps),
        grid_spec=pltpu.PrefetchScalarGridSpec(
            num_scalar_prefetch=2,
            grid=(n_steps,),
            in_specs=[pl.BlockSpec(memory_space=pl.ANY),
                      wspec((D_MODEL, EXPERT_FF)), wspec((D_MODEL, EXPERT_FF)), wspec((EXPERT_FF, D_MODEL))],
            out_specs=pl.BlockSpec(memory_space=pl.ANY),
            scratch_shapes=[pltpu.VMEM((2, 2, MOE_HALF * TOK_ROWS, LANES), F32),
                            pltpu.VMEM((2, 2, MOE_HALF * TOK_ROWS, LANES), F32),
                            pltpu.VMEM((D_MODEL, EXPERT_FF), BF16),
                            pltpu.VMEM((D_MODEL, EXPERT_FF), BF16),
                            pltpu.VMEM((EXPERT_FF, D_MODEL), BF16),
                            pltpu.SemaphoreType.DMA((2, 2)),
                            pltpu.SemaphoreType.DMA((2, 2))],
        ),
        out_shape=jax.ShapeDtypeStruct((n_out_rows * TOK_ROWS, LANES), F32),
        compiler_params=_cparams(("arbitrary",)),
        name="moe_experts",
    )(blk_e, idx, u, w1, w3, w2)


def _combine_kernel(y0_ref, y1_ref, h_ref, gate_ref, p_ref, g_ref, wg_ref, bg_ref, wp_ref, gf_ref,
                    o_ref, *, final):
    gate = gate_ref[...]
    tm = h_ref.shape[0]
    y0 = _load_token_tiles(y0_ref, tm)
    y1 = _load_token_tiles(y1_ref, tm)
    h = h_ref[...] + y0 * gate[:, 0:1] + y1 * gate[:, 1:2]
    u = _rms(h, g_ref[...])
    sg = jax.nn.sigmoid(jnp.dot(u.astype(BF16), wg_ref[...], preferred_element_type=F32) + bg_ref[...])
    pp = jnp.dot(p_ref[...].astype(BF16), wp_ref[...], preferred_element_type=F32)
    h = h + sg * pp
    if final:
        h = _rms(h, gf_ref[...])
    o_ref[...] = h


def _combine(yp, h, gate_t, p, g, wg, bg, wp, gf, tm, final):
    t = h.shape[0]
    nt = t // tm
    return pl.pallas_call(
        functools.partial(_combine_kernel, final=final),
        grid=(nt,),
        in_specs=[pl.BlockSpec((tm * TOK_ROWS, LANES), lambda i: (i, 0)),
                  pl.BlockSpec((tm * TOK_ROWS, LANES), lambda i: (nt + i, 0)),
                  pl.BlockSpec((tm, D_MODEL), lambda i: (i, 0)),
                  pl.BlockSpec((tm, TOP_K), lambda i: (i, 0)),
                  pl.BlockSpec((tm, PLE_DIM), lambda i: (i, 0)),
                  _full((1, D_MODEL)), _full((D_MODEL, D_MODEL)), _full((1, D_MODEL)),
                  _full((PLE_DIM, D_MODEL)), _full((1, D_MODEL))],
        out_specs=pl.BlockSpec((tm, D_MODEL), lambda i: (i, 0)),
        out_shape=jax.ShapeDtypeStruct((t, D_MODEL), F32),
        compiler_params=_cparams(("parallel",)),
        name="moe_combine_ple",
    )(yp, yp, h, gate_t, p, g, wg, bg, wp, gf)


def _rope_group(x, cos, sin, lane):
    half = QK_ROPE // 2
    rot = jnp.where(lane < ROPE_MID, -pltpu.roll(x, HEAD_PAD - half, 1), pltpu.roll(x, half, 1))
    return x * cos + rot * sin


def _kvq_kernel(h_ref, gkv_ref, gmix_ref, wdkv_ref, glat_ref, wkr_ref, wuk_ref, wuv_ref,
                wdq_ref, gq_ref, wuq_ref, cos_ref, sin_ref,
                lat_ref, kpe_ref, kcat_ref, v_ref, q_ref):
    h = h_ref[...]
    tm = h.shape[0]
    cos = cos_ref[...]
    sin = sin_ref[...]
    lane = lax.broadcasted_iota(jnp.int32, (tm, HEAD_PAD), 1)
    ukv = _rms(h, gkv_ref[...]).astype(BF16)
    c = _rms(jnp.dot(ukv, wdkv_ref[...], preferred_element_type=F32), glat_ref[...])
    lat_ref[...] = c
    kpe = _rope_group(jnp.dot(ukv, wkr_ref[...], preferred_element_type=F32), cos, sin, lane)
    kpe_ref[...] = kpe
    cb = c.astype(BF16)
    v_ref[...] = jnp.dot(cb, wuv_ref[...], preferred_element_type=F32).astype(BF16)
    kn = jnp.dot(cb, wuk_ref[...], preferred_element_type=F32)
    umix = _rms(h, gmix_ref[...]).astype(BF16)
    cq = _rms(jnp.dot(umix, wdq_ref[...], preferred_element_type=F32), gq_ref[...])
    q = jnp.dot(cq.astype(BF16), wuq_ref[...], preferred_element_type=F32)
    for hd in range(N_HEADS):
        lanes = slice(hd * HEAD_PAD, (hd + 1) * HEAD_PAD)
        kcat_ref[:, lanes] = (kn[:, lanes] + kpe).astype(BF16)
        q_ref[:, lanes] = (_rope_group(q[:, lanes], cos, sin, lane) * (SCALE * LOG2E)).astype(BF16)


def _kvq(h, gkv, gmix, wdkv, glat, wkr, wuk, wuv, wdq, gq, wuq, cos, sin, tm, n_pos_blocks):
    t = h.shape[0]
    row = lambda w: pl.BlockSpec((tm, w), lambda i: (i, 0))
    return pl.pallas_call(
        _kvq_kernel,
        grid=(t // tm,),
        in_specs=[row(D_MODEL), _full((1, D_MODEL)), _full((1, D_MODEL)),
                  _full((D_MODEL, KV_LORA)), _full((1, KV_LORA)), _full((D_MODEL, HEAD_PAD)),
                  _full((KV_LORA, N_HEADS * HEAD_PAD)), _full((KV_LORA, N_HEADS * V_HEAD)),
                  _full((D_MODEL, Q_LORA)), _full((1, Q_LORA)), _full((Q_LORA, N_HEADS * HEAD_PAD)),
                  pl.BlockSpec((tm, HEAD_PAD), lambda i: (i % n_pos_blocks, 0)),
                  pl.BlockSpec((tm, HEAD_PAD), lambda i: (i % n_pos_blocks, 0))],
        out_specs=[row(KV_LORA), row(HEAD_PAD), row(N_HEADS * HEAD_PAD), row(N_HEADS * V_HEAD),
                   row(N_HEADS * HEAD_PAD)],
        out_shape=[jax.ShapeDtypeStruct((t, KV_LORA), F32),
                   jax.ShapeDtypeStruct((t, HEAD_PAD), F32),
                   jax.ShapeDtypeStruct((t, N_HEADS * HEAD_PAD), BF16),
                   jax.ShapeDtypeStruct((t, N_HEADS * V_HEAD), BF16),
                   jax.ShapeDtypeStruct((t, N_HEADS * HEAD_PAD), BF16)],
        compiler_params=_cparams(("parallel",)),
        name="kvq_proj",
    )(h, gkv, gmix, wdkv, glat, wkr, wuk, wuv, wdq, gq, wuq, cos, sin)


ATT_TQ = 512
ATT_TK = 256


def _flash_kernel(q_ref, k_ref, v_ref, o_ref):
    s_len = q_ref.shape[0]
    tq, tk = ATT_TQ, ATT_TK
    row = lax.broadcasted_iota(jnp.int32, (tq, tk), 0)
    col = lax.broadcasted_iota(jnp.int32, (tq, tk), 1)
    lane = lax.broadcasted_iota(jnp.int32, (tq, 2 * V_HEAD), 1)
    nt = (((1,), (1,)), ((), ()))
    for qi in range(s_len // tq):
        q0 = qi * tq
        qs = [q_ref[q0:q0 + tq, hh * HEAD_PAD:(hh + 1) * HEAD_PAD] for hh in range(2)]
        m = [jnp.full((tq, 1), -jnp.inf, F32) for _ in range(2)]
        l = [jnp.zeros((tq, 1), F32) for _ in range(2)]
        acc = [jnp.zeros((tq, 2 * V_HEAD), F32) for _ in range(2)]
        for kj in range((q0 + tq) // tk):
            k0 = kj * tk
            v = v_ref[k0:k0 + tk, :]
            for hh in range(2):
                k = k_ref[k0:k0 + tk, hh * HEAD_PAD:(hh + 1) * HEAD_PAD]
                s = lax.dot_general(qs[hh], k, nt, preferred_element_type=F32)
                if k0 + tk - 1 > q0:
                    s = jnp.where(col + k0 <= row + q0, s, -jnp.inf)
                m_new = jnp.maximum(m[hh], jnp.max(s, axis=-1, keepdims=True))
                alpha = jnp.exp2(m[hh] - m_new)
                p = jnp.exp2(s - m_new)
                l[hh] = alpha * l[hh] + jnp.sum(p, axis=-1, keepdims=True)
                acc[hh] = alpha * acc[hh] + jnp.dot(p.astype(BF16), v, preferred_element_type=F32)
                m[hh] = m_new
        o = jnp.where(lane < V_HEAD, acc[0] / l[0], acc[1] / l[1])
        o_ref[q0:q0 + tq, :] = o.astype(BF16)


def _flash(q, kcat, v, b, s):
    return pl.pallas_call(
        _flash_kernel,
        grid=(b, N_HEADS // 2),
        in_specs=[pl.BlockSpec((s, 2 * HEAD_PAD), lambda i, h: (i, h)),
                  pl.BlockSpec((s, 2 * HEAD_PAD), lambda i, h: (i, h)),
                  pl.BlockSpec((s, 2 * V_HEAD), lambda i, h: (i, h))],
        out_specs=pl.BlockSpec((s, 2 * V_HEAD), lambda i, h: (i, h)),
        out_shape=jax.ShapeDtypeStruct((b * s, N_HEADS * V_HEAD), BF16),
        compiler_params=_cparams(("parallel", "parallel")),
        name="flash_attn",
    )(q, kcat, v)


def _proj_res_kernel(a_ref, w_ref, h_ref, o_ref):
    o_ref[...] = h_ref[...] + jnp.dot(a_ref[...], w_ref[...], preferred_element_type=F32)


def _proj_res(a, w, h, tm):
    t, k = a.shape
    return pl.pallas_call(
        _proj_res_kernel,
        grid=(t // tm,),
        in_specs=[pl.BlockSpec((tm, k), lambda i: (i, 0)), _full(w.shape),
                  pl.BlockSpec((tm, D_MODEL), lambda i: (i, 0))],
        out_specs=pl.BlockSpec((tm, D_MODEL), lambda i: (i, 0)),
        out_shape=jax.ShapeDtypeStruct((t, D_MODEL), F32),
        compiler_params=_cparams(("parallel",)),
        name="proj_residual",
    )(a, w, h)


def _qlat_kernel(q_ref, wukt_ref, o_ref):
    for hd in range(N_HEADS):
        qh = q_ref[:, hd * HEAD_PAD:(hd + 1) * HEAD_PAD]
        o_ref[:, hd * KV_LORA:(hd + 1) * KV_LORA] = jnp.dot(
            qh, wukt_ref[hd], preferred_element_type=F32).astype(BF16)


def _qlat(q, wukt):
    n = q.shape[0]
    return pl.pallas_call(
        _qlat_kernel,
        in_specs=[_full(q.shape), _full(wukt.shape)],
        out_specs=_full((n, N_HEADS * KV_LORA)),
        out_shape=jax.ShapeDtypeStruct((n, N_HEADS * KV_LORA), BF16),
        grid=(1,),
        compiler_params=_cparams(("arbitrary",)),
        name="q_absorb",
    )(q, wukt)


PAGES_PER_CHUNK = 32
CHUNK_SLOTS = 4
CHUNK_AHEAD = 2


def _paged_kernel(pt_ref, ql_ref, qp_ref, cn_ref, kn_ref, lat_hbm, kpt_hbm, o_ref, lbuf, kbuf, sems, *, nb):
    b = pl.program_id(0)
    n_pages = pt_ref.shape[0] // nb
    cp = PAGES_PER_CHUNK
    n_chunks = n_pages // cp
    assert n_chunks == CHUNK_SLOTS
    total = nb * n_chunks

    def start_chunk(g, slot):
        gg = jnp.where(g < total, g, 0)
        for j in range(cp):
            page = pt_ref[gg * cp + j]
            src = pl.ds(pl.multiple_of(page * PAGE_SIZE, PAGE_SIZE), PAGE_SIZE)
            pltpu.make_async_copy(lat_hbm.at[src], lbuf.at[slot, pl.ds(j * PAGE_SIZE, PAGE_SIZE)],
                                  sems.at[0, slot]).start()
            pltpu.make_async_copy(kpt_hbm.at[page], kbuf.at[slot, j], sems.at[1, slot]).start()

    def wait_chunk(g, slot):
        del g
        pltpu.make_async_copy(lat_hbm.at[pl.ds(0, cp * PAGE_SIZE)], lbuf.at[slot], sems.at[0, slot]).wait()
        pltpu.make_async_copy(kpt_hbm.at[pl.ds(0, cp)], kbuf.at[slot], sems.at[1, slot]).wait()

    @pl.when(b == 0)
    def _():
        for a in range(CHUNK_AHEAD):
            start_chunk(a, a)

    ql = ql_ref[...]
    qp = qp_ref[...]
    nt = (((1,), (1,)), ((), ()))
    m = jnp.full((N_HEADS, 1), -jnp.inf, F32)
    l = jnp.zeros((N_HEADS, 1), F32)
    acc = jnp.zeros((N_HEADS, KV_LORA), F32)
    for c in range(n_chunks):
        g = b * n_chunks + c
        if c % CHUNK_AHEAD == 0:
            for a in range(CHUNK_AHEAD):
                start_chunk(g + CHUNK_AHEAD + a, (c + CHUNK_AHEAD + a) % CHUNK_SLOTS)
            for a in range(CHUNK_AHEAD):
                wait_chunk(g + a, c + a)
        kc = lbuf[c].astype(BF16)
        kp = jnp.concatenate([kbuf[c, jj] for jj in range(cp)], axis=1).astype(BF16)
        s = (lax.dot_general(ql, kc, nt, preferred_element_type=F32)
             + jnp.dot(qp, kp, preferred_element_type=F32))
        m_new = jnp.maximum(m, jnp.max(s, axis=-1, keepdims=True))
        alpha = jnp.exp2(m - m_new)
        p = jnp.exp2(s - m_new)
        l = alpha * l + jnp.sum(p, axis=-1, keepdims=True)
        acc = alpha * acc + jnp.dot(p.astype(BF16), kc, preferred_element_type=F32)
        m = m_new
    cn = cn_ref[...].astype(BF16).astype(F32)
    kn = kn_ref[...].astype(BF16).astype(F32)
    s = (jnp.sum(ql.astype(F32) * cn, axis=-1, keepdims=True)
         + jnp.sum(qp.astype(F32) * kn, axis=-1, keepdims=True))
    m_new = jnp.maximum(m, s)
    alpha = jnp.exp2(m - m_new)
    p = jnp.exp2(s - m_new)
    l = alpha * l + p
    acc = alpha * acc + p.astype(BF16).astype(F32) * cn
    o_ref[...] = acc / l

    @pl.when(b == nb - 1)
    def _():
        for a in range(CHUNK_AHEAD):
            wait_chunk(total + a, a)


def _paged_attn(pt_flat, ql, qp, c_new, k_new, cache_latent, cache_kpt):
    n = ql.shape[0]
    rows = PAGES_PER_CHUNK * PAGE_SIZE
    per_seq = lambda w: pl.BlockSpec((None,) + w, lambda i, pt: (i, 0, 0))
    return pl.pallas_call(
        functools.partial(_paged_kernel, nb=n),
        grid_spec=pltpu.PrefetchScalarGridSpec(
            num_scalar_prefetch=1,
            grid=(n,),
            in_specs=[per_seq((N_HEADS, KV_LORA)), per_seq((N_HEADS, QK_ROPE)),
                      per_seq((1, KV_LORA)), per_seq((1, QK_ROPE)),
                      pl.BlockSpec(memory_space=pl.ANY), pl.BlockSpec(memory_space=pl.ANY)],
            out_specs=per_seq((N_HEADS, KV_LORA)),
            scratch_shapes=[pltpu.VMEM((CHUNK_SLOTS, rows, KV_LORA), F32),
                            pltpu.VMEM((CHUNK_SLOTS, PAGES_PER_CHUNK, QK_ROPE, PAGE_SIZE), F32),
                            pltpu.SemaphoreType.DMA((2, CHUNK_SLOTS))],
        ),
        out_shape=jax.ShapeDtypeStruct((n, N_HEADS, KV_LORA), F32),
        compiler_params=_cparams(("arbitrary",)),
        name="paged_attn",
    )(pt_flat, ql, qp, c_new, k_new, cache_latent, cache_kpt)


def _attn_out_kernel(ol_ref, wuv_ref, wo_ref, h_ref, o_ref):
    acc = h_ref[...]
    for hd in range(N_HEADS):
        oh = ol_ref[:, hd * KV_LORA:(hd + 1) * KV_LORA].astype(BF16)
        th = jnp.dot(oh, wuv_ref[hd], preferred_element_type=F32).astype(BF16)
        acc = acc + jnp.dot(th, wo_ref[hd * V_HEAD:(hd + 1) * V_HEAD, :], preferred_element_type=F32)
    o_ref[...] = acc


def _attn_out(ol, wuv_h, wo, h):
    n = h.shape[0]
    return pl.pallas_call(
        _attn_out_kernel,
        grid=(1,),
        in_specs=[_full(ol.shape), _full(wuv_h.shape), _full(wo.shape), _full(h.shape)],
        out_specs=_full(h.shape),
        out_shape=jax.ShapeDtypeStruct((n, D_MODEL), F32),
        compiler_params=_cparams(("arbitrary",)),
        name="attn_out",
    )(ol, wuv_h, wo, h)


def _rope_tables(pos):
    half = QK_ROPE // 2
    inv = ROPE_THETA ** (-jnp.arange(half, dtype=F32) / half)
    ang = pos.astype(F32)[:, None] * inv[None, :]
    cos, sin = jnp.cos(ang), jnp.sin(ang)
    n = pos.shape[0]
    cos_t = jnp.concatenate([jnp.ones((n, QK_NOPE), F32), cos, cos,
                             jnp.ones((n, HEAD_PAD - ROPE_HI), F32)], axis=1)
    sin_t = jnp.concatenate([jnp.zeros((n, QK_NOPE), F32), sin, sin,
                             jnp.zeros((n, HEAD_PAD - ROPE_HI), F32)], axis=1)
    return cos_t, sin_t


def _moe(h, g_ffn, wr, br, w1, w3, w2, layer, tm):
    t = h.shape[0]
    u, code, gate, cnt = _router(h, g_ffn, wr, br, tm)
    counts = cnt[:, 0].astype(jnp.int32)
    padded = (counts + MOE_BM - 1) // MOE_BM * MOE_BM
    pends = jnp.cumsum(padded)
    pstarts = pends - padded
    n_steps = (t * TOP_K + N_EXPERTS * (MOE_BM - 1) + MOE_BM - 1) // MOE_BM
    n_rows = n_steps * MOE_BM
    eid = code >> RANK_BITS
    eiota = jnp.arange(N_EXPERTS, dtype=jnp.int32)
    start_of = jnp.sum(jnp.where(eid[..., None] == eiota, pstarts, 0), axis=-1)
    dest = (start_of + (code & ((1 << RANK_BITS) - 1))).reshape(-1)
    block_start = jnp.arange(n_steps, dtype=jnp.int32) * MOE_BM
    blk_e = jnp.minimum(jnp.sum((pends[None, :] <= block_start[:, None]).astype(jnp.int32), axis=1),
                        N_EXPERTS - 1)
    inv = jnp.full((n_rows,), -1, jnp.int32).at[dest].set(
        jnp.arange(TOP_K * t, dtype=jnp.int32), unique_indices=True)
    is_pad = inv < 0
    rows = jnp.arange(n_rows, dtype=jnp.int32)
    out_row = jnp.where(is_pad, TOP_K * t - 1 + jnp.cumsum(is_pad.astype(jnp.int32)), inv)
    src_tok = jnp.where(is_pad, rows, inv) & (t - 1)
    n_out = n_rows + 2 * MOE_BM
    assert t & (t - 1) == 0 and t <= (1 << TOK_BITS) and n_out <= (1 << (32 - TOK_BITS))
    idx = (out_row << TOK_BITS) | src_tok
    yp = _experts(blk_e, idx, u, w1, w3, w2, layer, n_steps, n_out)
    return yp, gate.T


def kernel(x_prompt, x_sample, cache_conv, cache_latent, cache_kpe, page_table, p_prompt, p_sample, g_mix, g_ffn, g_ple, g_final, conv_w_in, conv_b_in, conv_w_dw, conv_b_dw, conv_ln_g, conv_ln_b, conv_w_out, conv_b_out, g_kv_in, w_dkv, g_kv_lat, w_kr, w_uk, w_uv, w_dq, g_q_lat, w_uq, w_o, w_group, b_group, w_inner, b_inner, w_exp_gate, w_exp_up, w_exp_down, w_ple_gate, b_ple_gate, w_ple_proj):
    bp, sp, d = x_prompt.shape
    nd = x_sample.shape[0]
    tp = bp * sp
    row = lambda a: a.reshape(1, -1)

    w_in_b = conv_w_in[0].astype(BF16)
    w_out_b = conv_w_out[0].astype(BF16)
    wdw_pad = jnp.concatenate([conv_w_dw[0], jnp.zeros((HALO - CONV_W, d), F32)], axis=0)
    wr, br = [], []
    for i in range(2):
        wr.append(jnp.concatenate([w_group[i].T, jnp.zeros((8 - N_GROUPS, d), F32),
                                   w_inner[i].reshape(d, N_EXPERTS).T], axis=0))
        br.append(jnp.concatenate([b_group[i], jnp.zeros((8 - N_GROUPS,), F32),
                                   b_inner[i].reshape(N_EXPERTS)]).reshape(ROUTER_ROWS, 1))
    w_dkv_b = w_dkv.astype(BF16)
    w_kr_pad = jnp.zeros((d, HEAD_PAD), F32).at[:, ROPE_LO:ROPE_HI].set(w_kr).astype(BF16)
    w_uk_pad = jnp.pad(w_uk, ((0, 0), (0, 0), (0, HEAD_PAD - QK_NOPE))).reshape(KV_LORA, -1).astype(BF16)
    w_uv_b = w_uv.reshape(KV_LORA, -1).astype(BF16)
    w_dq_b = w_dq[0].astype(BF16)
    w_uq_pad = jnp.pad(w_uq[0].reshape(Q_LORA, N_HEADS, QK_NOPE + QK_ROPE),
                       ((0, 0), (0, 0), (0, HEAD_PAD - ROPE_HI))).reshape(Q_LORA, -1).astype(BF16)
    w_o_b = w_o[0].astype(BF16)
    w_ukt_pad = jnp.pad(jnp.transpose(w_uk, (1, 2, 0)),
                        ((0, 0), (0, HEAD_PAD - QK_NOPE), (0, 0))).astype(BF16)
    w_uv_h = jnp.transpose(w_uv, (1, 0, 2)).astype(BF16)
    wg_b = [w_ple_gate[i].astype(BF16) for i in range(2)]
    wp_b = [w_ple_proj[i].astype(BF16) for i in range(2)]

    def moe_ple(h, layer, p, tm, final):
        yp, gate_t = _moe(h, row(g_ffn[layer]), wr[layer], br[layer],
                          w_exp_gate, w_exp_up, w_exp_down, layer, tm)
        return _combine(yp, h, gate_t, p, row(g_ple[layer]), wg_b[layer], row(b_ple_gate[layer]),
                        wp_b[layer], row(g_final), tm, final)

    def kvq(h, cos_t, sin_t, tm, n_pos_blocks):
        return _kvq(h, row(g_kv_in), row(g_mix[1]), w_dkv_b, row(g_kv_lat), w_kr_pad, w_uk_pad, w_uv_b,
                    w_dq_b, row(g_q_lat[0]), w_uq_pad, cos_t, sin_t, tm, n_pos_blocks)

    xp = x_prompt.reshape(tp, d)
    z = _conv_in(xp, row(g_mix[0]), w_in_b, row(conv_b_in[0]), 512)
    z3 = z.reshape(bp, sp, d)
    conv_prompt = z3[:, sp - (CONV_W - 1):, :][None]
    ctx0 = jnp.zeros((bp, HALO, d), F32)
    h = _conv_mix(z3, ctx0, x_prompt, wdw_pad, row(conv_b_dw[0]), row(conv_ln_g[0]), row(conv_ln_b[0]),
                  w_out_b, row(conv_b_out[0])).reshape(tp, d)
    h = moe_ple(h, 0, p_prompt[0].reshape(tp, PLE_DIM), 512, False)
    cos_p, sin_p = _rope_tables(jnp.arange(sp, dtype=jnp.int32))
    lat_p, kpe_p, kcat, v, q = kvq(h, cos_p, sin_p, 512, sp // 512)
    o = _flash(q, kcat, v, bp, sp)
    h = _proj_res(o, w_o_b, h, 512)
    y_prompt = moe_ple(h, 1, p_prompt[1].reshape(tp, PLE_DIM), 512, True).reshape(bp, sp, d)
    latent_prompt = lat_p.reshape(bp, sp, KV_LORA)
    kpe_prompt = kpe_p[:, ROPE_LO:ROPE_HI].reshape(bp, sp, QK_ROPE)

    xs = x_sample.reshape(nd, d)
    zs = _conv_in(xs, row(g_mix[0]), w_in_b, row(conv_b_in[0]), nd)
    ctx_t = jnp.transpose(cache_conv[0], (1, 0, 2))
    conv_sample = jnp.transpose(jnp.concatenate([ctx_t[1:], zs[None]], axis=0), (1, 0, 2))[None]
    hs = _conv_step(ctx_t, conv_w_dw[0][:, None, :], zs, xs,
                    row(conv_b_dw[0]), row(conv_ln_g[0]), row(conv_ln_b[0]), w_out_b, row(conv_b_out[0]))
    hs = moe_ple(hs, 0, p_sample[0].reshape(nd, PLE_DIM), nd, False)
    n_pages = page_table.shape[1]
    pos_s = jnp.full((nd,), n_pages * PAGE_SIZE, jnp.int32)
    cos_s, sin_s = _rope_tables(pos_s)
    lat_s, kpe_s, _, _, qs = kvq(hs, cos_s, sin_s, nd, 1)
    ql = _qlat(qs, w_ukt_pad).reshape(nd, N_HEADS, KV_LORA)
    qp = qs.reshape(nd, N_HEADS, HEAD_PAD)[:, :, ROPE_LO:ROPE_HI]
    kpe_new = kpe_s[:, ROPE_LO:ROPE_HI]
    o_lat = _paged_attn(page_table.reshape(-1), ql, qp, lat_s.reshape(nd, 1, KV_LORA),
                        kpe_new.reshape(nd, 1, QK_ROPE), cache_latent.reshape(-1, KV_LORA),
                        jnp.transpose(cache_kpe, (0, 2, 1)))
    hs = _attn_out(o_lat.reshape(nd, N_HEADS * KV_LORA), w_uv_h, w_o_b, hs)
    y_sample = moe_ple(hs, 1, p_sample[1].reshape(nd, PLE_DIM), nd, True).reshape(nd, 1, d)
    latent_sample = lat_s.reshape(nd, 1, KV_LORA)
    kpe_sample = kpe_new.reshape(nd, 1, QK_ROPE)

    return (y_prompt, y_sample, conv_prompt, conv_sample, latent_prompt, kpe_prompt, latent_sample, kpe_sample)
```

```python
import functools
import math

import jax
import jax.numpy as jnp
from jax import lax
from jax.experimental import pallas as pl
from jax.experimental.pallas import tpu as pltpu

F32 = jnp.float32
BF16 = jnp.bfloat16

D_MODEL = 1024
PLE_DIM = 256
CONV_W = 31
N_HEADS = 16
QK_NOPE = 64
QK_ROPE = 32
V_HEAD = 64
Q_LORA = 384
KV_LORA = 256
ROPE_THETA = 10000.0
SCALE = 1.0 / math.sqrt(QK_NOPE + QK_ROPE)
LOG2E = 1.4426950408889634
N_GROUPS = 4
EXPERTS_PER_GROUP = 8
N_EXPERTS = N_GROUPS * EXPERTS_PER_GROUP
TOP_K = 2
EXPERT_FF = 512
PAGE_SIZE = 128
EPS = 1e-6

HEAD_PAD = 128
ROPE_LO = QK_NOPE
ROPE_MID = QK_NOPE + QK_ROPE // 2
ROPE_HI = QK_NOPE + QK_ROPE
HALO = 32
MOE_HALF = 128
MOE_BM = 2 * MOE_HALF
ROUTER_ROWS = 48
RANK_BITS = 20
TOK_BITS = 15
VMEM_LIMIT = 56 * 1024 * 1024


def _cparams(sem):
    return pltpu.CompilerParams(dimension_semantics=sem, vmem_limit_bytes=VMEM_LIMIT)


def _rms(x, g):
    return x * lax.rsqrt(jnp.mean(x * x, axis=-1, keepdims=True) + EPS) * g


def _full(shape):
    n = len(shape)
    return pl.BlockSpec(shape, lambda *_: (0,) * n)


LANES = 128
SUBLANES = 8
TOK_ROWS = D_MODEL // LANES


def _load_token_tiles(ref, n):
    return jnp.concatenate([ref[pl.ds(s, n, stride=TOK_ROWS), :] for s in range(TOK_ROWS)], axis=1)


def _store_token_tiles(ref, x):
    n = x.shape[0]
    for s in range(TOK_ROWS):
        ref[pl.ds(s, n, stride=TOK_ROWS), :] = x[:, s * LANES:(s + 1) * LANES]


def _conv_in_kernel(x_ref, g_ref, w_ref, b_ref, z_ref):
    u = _rms(x_ref[...], g_ref[...])
    a = jnp.dot(u.astype(BF16), w_ref[...], preferred_element_type=F32) + b_ref[...]
    z_ref[...] = a[:, :D_MODEL] * jax.nn.sigmoid(a[:, D_MODEL:])


def _conv_in(x, g, w_in, b_in, tm):
    t = x.shape[0]
    return pl.pallas_call(
        _conv_in_kernel,
        grid=(t // tm,),
        in_specs=[pl.BlockSpec((tm, D_MODEL), lambda i: (i, 0)),
                  _full((1, D_MODEL)), _full((D_MODEL, 2 * D_MODEL)), _full((1, 2 * D_MODEL))],
        out_specs=pl.BlockSpec((tm, D_MODEL), lambda i: (i, 0)),
        out_shape=jax.ShapeDtypeStruct((t, D_MODEL), F32),
        compiler_params=_cparams(("parallel",)),
        name="conv_in",
    )(x, g, w_in, b_in)


def _ln_silu_out(y, lng, lnb, wout, bout):
    mu = jnp.mean(y, axis=-1, keepdims=True)
    yc = y - mu
    var = jnp.mean(yc * yc, axis=-1, keepdims=True)
    n = yc * lax.rsqrt(var + EPS) * lng + lnb
    act = n * jax.nn.sigmoid(n)
    return jnp.dot(act.astype(BF16), wout, preferred_element_type=F32) + bout


CONV_TS = 256
CONV_RC = 128
CONV_LC = 256


def _conv_mix_kernel(z_ref, halo_ref, ctx_ref, x_ref, wdw_ref, bdw_ref, lng_ref, lnb_ref,
                     wout_ref, bout_ref, h_ref, buf_ref, y_ref):
    s = pl.program_id(1)

    @pl.when(s == 0)
    def _():
        buf_ref[0, 0:HALO, :] = ctx_ref[...]

    @pl.when(s > 0)
    def _():
        buf_ref[0, 0:HALO, :] = halo_ref[...]

    buf_ref[0, HALO:, :] = z_ref[...]
    shifted_rows = CONV_TS + HALO - SUBLANES
    for sh in range(1, SUBLANES):
        buf_ref[sh, 0:shifted_rows, :] = buf_ref[0, sh:sh + shifted_rows, :]
    first = HALO - (CONV_W - 1)
    for lc in range(D_MODEL // CONV_LC):
        lanes = slice(lc * CONV_LC, (lc + 1) * CONV_LC)
        for rc in range(CONV_TS // CONV_RC):
            r0 = rc * CONV_RC
            acc = jnp.broadcast_to(bdw_ref[:, lanes], (CONV_RC, CONV_LC))
            for k in range(CONV_W):
                sh = (first + k) % SUBLANES
                a0 = r0 + first + k - sh
                acc = acc + wdw_ref[k:k + 1, lanes] * buf_ref[sh, a0:a0 + CONV_RC, lanes]
            y_ref[r0:r0 + CONV_RC, lanes] = acc
    out = _ln_silu_out(y_ref[...], lng_ref[...], lnb_ref[...], wout_ref[...], bout_ref[...])
    h_ref[...] = x_ref[...] + out


def _conv_mix(z3, ctx, x3, wdw, bdw, lng, lnb, wout, bout):
    b, s, d = z3.shape
    ts = CONV_TS
    hb = ts // HALO
    return pl.pallas_call(
        _conv_mix_kernel,
        grid=(b, s // ts),
        in_specs=[pl.BlockSpec((None, ts, d), lambda i, j: (i, j, 0)),
                  pl.BlockSpec((None, HALO, d), lambda i, j: (i, jnp.maximum(j * hb - 1, 0), 0)),
                  pl.BlockSpec((None, HALO, d), lambda i, j: (i, 0, 0)),
                  pl.BlockSpec((None, ts, d), lambda i, j: (i, j, 0)),
                  _full((HALO, d)), _full((1, d)), _full((1, d)), _full((1, d)),
                  _full((d, d)), _full((1, d))],
        out_specs=pl.BlockSpec((None, ts, d), lambda i, j: (i, j, 0)),
        out_shape=jax.ShapeDtypeStruct((b, s, d), F32),
        scratch_shapes=[pltpu.VMEM((SUBLANES, ts + HALO, d), F32), pltpu.VMEM((ts, d), F32)],
        compiler_params=_cparams(("parallel", "arbitrary")),
        name="conv_mix",
    )(z3, z3, ctx, x3, wdw, bdw, lng, lnb, wout, bout)


def _conv_step_kernel(ctx_ref, w_ref, z_ref, x_ref, bdw_ref, lng_ref, lnb_ref, wout_ref, bout_ref,
                      h_ref, acc_ref):
    k = pl.program_id(0)

    @pl.when(k == 0)
    def _():
        acc_ref[...] = jnp.broadcast_to(bdw_ref[...], acc_ref.shape)

    @pl.when(k < CONV_W - 1)
    def _():
        acc_ref[...] += w_ref[...] * ctx_ref[...]

    @pl.when(k == CONV_W - 1)
    def _():
        y = acc_ref[...] + w_ref[...] * z_ref[...]
        out = _ln_silu_out(y, lng_ref[...], lnb_ref[...], wout_ref[...], bout_ref[...])
        h_ref[...] = x_ref[...] + out


def _conv_step(ctx_t, wdw3, z, x, bdw, lng, lnb, wout, bout):
    n, d = z.shape
    last = CONV_W - 2
    return pl.pallas_call(
        _conv_step_kernel,
        grid=(CONV_W,),
        in_specs=[pl.BlockSpec((None, n, d), lambda k: (jnp.minimum(k, last), 0, 0)),
                  pl.BlockSpec((None, 1, d), lambda k: (k, 0, 0)),
                  _full((n, d)), _full((n, d)), _full((1, d)), _full((1, d)), _full((1, d)),
                  _full((d, d)), _full((1, d))],
        out_specs=_full((n, d)),
        out_shape=jax.ShapeDtypeStruct((n, d), F32),
        scratch_shapes=[pltpu.VMEM((n, d), F32)],
        compiler_params=_cparams(("arbitrary",)),
        name="conv_step",
    )(ctx_t, wdw3, z, x, bdw, lng, lnb, wout, bout)


def _router_kernel(h_ref, g_ref, wr_ref, br_ref, before_ref, u_ref, code_ref, gate_ref, cnt_ref, carry_ref):
    i = pl.program_id(0)

    @pl.when(i == 0)
    def _():
        carry_ref[...] = jnp.zeros_like(carry_ref)

    u = _rms(h_ref[...], g_ref[...])
    _store_token_tiles(u_ref, u)
    nt = (((1,), (1,)), ((), ()))
    u_hi = u.astype(BF16)
    u_lo = (u - u_hi.astype(F32)).astype(BF16)
    part = lax.dot_general(wr_ref[...], u_hi, nt, preferred_element_type=F32)
    logits = (part[0:ROUTER_ROWS] + part[ROUTER_ROWS:]
              + lax.dot_general(wr_ref[0:ROUTER_ROWS, :], u_lo, nt, preferred_element_type=F32) + br_ref[...])
    gl = logits[0:N_GROUPS]
    il = logits[8:8 + N_EXPERTS]
    gmax = jnp.max(gl, axis=0, keepdims=True)
    giota = lax.broadcasted_iota(jnp.int32, gl.shape, 0)
    gsel = jnp.min(jnp.where(gl == gmax, giota, N_GROUPS), axis=0, keepdims=True)
    gw = 1.0 / jnp.sum(jnp.exp(gl - gmax), axis=0, keepdims=True)
    eiota = lax.broadcasted_iota(jnp.int32, il.shape, 0)
    ml = jnp.where((eiota >> 3) == gsel, il, -jnp.inf)
    v1 = jnp.max(ml, axis=0, keepdims=True)
    i1 = jnp.min(jnp.where(ml == v1, eiota, N_EXPERTS), axis=0, keepdims=True)
    ml2 = jnp.where(eiota == i1, -jnp.inf, ml)
    v2 = jnp.max(ml2, axis=0, keepdims=True)
    i2 = jnp.min(jnp.where(ml2 == v2, eiota, N_EXPERTS), axis=0, keepdims=True)
    e2 = jnp.exp(v2 - v1)
    p1 = 1.0 / (1.0 + e2)
    gate_ref[0:1, :] = p1 * gw
    gate_ref[1:2, :] = e2 * p1 * gw
    hit1 = eiota == i1
    hit2 = eiota == i2
    onehot = jnp.where(hit1 | hit2, 1.0, 0.0)
    cum = jnp.dot(onehot.astype(BF16), before_ref[...], preferred_element_type=F32) + carry_ref[:, 0:1]
    rank1 = jnp.sum(jnp.where(hit1, cum, 0.0), axis=0, keepdims=True).astype(jnp.int32)
    rank2 = jnp.sum(jnp.where(hit2, cum, 0.0), axis=0, keepdims=True).astype(jnp.int32)
    code_ref[0:1, :] = (i1 << RANK_BITS) | rank1
    code_ref[1:2, :] = (i2 << RANK_BITS) | rank2
    carry_ref[...] += jnp.sum(onehot, axis=1, keepdims=True)
    cnt_ref[...] = carry_ref[...]


def _router(h, g, wr, br, tm):
    t = h.shape[0]
    row2 = pl.BlockSpec((TOP_K, tm), lambda i: (0, i))
    before = (jnp.arange(tm)[:, None] < jnp.arange(tm)[None, :]).astype(BF16)
    wr_hi = wr.astype(BF16)
    wr_split = jnp.concatenate([wr_hi, (wr - wr_hi.astype(F32)).astype(BF16)], axis=0)
    return pl.pallas_call(
        _router_kernel,
        grid=(t // tm,),
        in_specs=[pl.BlockSpec((tm, D_MODEL), lambda i: (i, 0)),
                  _full((1, D_MODEL)), _full((2 * ROUTER_ROWS, D_MODEL)), _full((ROUTER_ROWS, 1)),
                  _full((tm, tm))],
        out_specs=[pl.BlockSpec((tm * TOK_ROWS, LANES), lambda i: (i, 0)), row2, row2,
                   _full((N_EXPERTS, 128))],
        out_shape=[jax.ShapeDtypeStruct((t * TOK_ROWS, LANES), F32),
                   jax.ShapeDtypeStruct((TOP_K, t), jnp.int32),
                   jax.ShapeDtypeStruct((TOP_K, t), F32),
                   jax.ShapeDtypeStruct((N_EXPERTS, 128), F32)],
        scratch_shapes=[pltpu.VMEM((N_EXPERTS, 128), F32)],
        compiler_params=_cparams(("arbitrary",)),
        name="router",
    )(h, g, wr_split, br, before)


def _expert_kernel(blk_ref, idx_ref, u_hbm, w1_ref, w3_ref, w2_ref, yp_hbm, xbuf, ybuf, w1b, w3b, w2b,
                   gsem, ssem, *, n_steps):
    j = pl.program_id(0)
    prime0 = n_steps * MOE_BM
    half_rows = MOE_HALF * TOK_ROWS

    def tile_of(t):
        return pl.ds(pl.multiple_of(t * TOK_ROWS, TOK_ROWS), TOK_ROWS)

    def gather_start(step, half, par):
        base = step * MOE_BM + half * MOE_HALF
        for r in range(MOE_HALF):
            tok = idx_ref[base + r] & ((1 << TOK_BITS) - 1)
            pltpu.make_async_copy(u_hbm.at[tile_of(tok)], xbuf.at[par, half, tile_of(r)],
                                  gsem.at[par, half]).start()

    def gather_wait(half, par):
        pltpu.make_async_copy(u_hbm.at[pl.ds(0, half_rows)], xbuf.at[par, half], gsem.at[par, half]).wait()

    def prime_rows(half, par):
        return prime0 + (2 * par + half) * MOE_HALF

    def scatter_start(step, half, par, to_prime=None):
        base = step * MOE_BM + half * MOE_HALF
        for r in range(MOE_HALF):
            dst = lax.shift_right_logical(idx_ref[base + r], jnp.int32(TOK_BITS))
            if to_prime is not None:
                dst = jnp.where(to_prime, prime_rows(half, par) + r, dst)
            pltpu.make_async_copy(ybuf.at[par, half, tile_of(r)], yp_hbm.at[tile_of(dst)],
                                  ssem.at[par, half]).start()

    def scatter_wait(half, par):
        pltpu.make_async_copy(ybuf.at[par, half], yp_hbm.at[pl.ds(0, half_rows)], ssem.at[par, half]).wait()

    @pl.when(j == 0)
    def _():
        ybuf[...] = jnp.zeros_like(ybuf)
        for par, half in ((0, 0), (0, 1), (1, 0)):
            pltpu.make_async_copy(ybuf.at[par, half],
                                  yp_hbm.at[pl.ds(prime_rows(half, par) * TOK_ROWS, half_rows)],
                                  ssem.at[par, half]).start()
        gather_start(0, 0, 0)
        gather_start(0, 1, 0)

    e = blk_ref[j]
    prev = blk_ref[jnp.maximum(j - 1, 0)]

    @pl.when((j == 0) | (e != prev))
    def _():
        w1b[...] = w1_ref[...].astype(BF16)
        w3b[...] = w3_ref[...].astype(BF16)
        w2b[...] = w2_ref[...].astype(BF16)

    def compute(half, par):
        x = _load_token_tiles(xbuf.at[par, half], MOE_HALF).astype(BF16)
        a = jnp.dot(x, w1b[...], preferred_element_type=F32)
        b = jnp.dot(x, w3b[...], preferred_element_type=F32)
        hmid = (a * jax.nn.sigmoid(a)) * b
        _store_token_tiles(ybuf.at[par, half],
                           jnp.dot(hmid.astype(BF16), w2b[...], preferred_element_type=F32))

    nxt = jnp.minimum(j + 1, n_steps - 1)

    def step_body(par):
        gather_wait(0, par)
        scatter_wait(0, par)
        gather_start(nxt, 0, 1 - par)
        scatter_start(jnp.maximum(j - 1, 0), 1, 1 - par, to_prime=(j == 0))
        compute(0, par)
        gather_wait(1, par)
        scatter_wait(1, par)
        gather_start(nxt, 1, 1 - par)
        scatter_start(j, 0, par)
        compute(1, par)

    for par in range(2):
        pl.when(j % 2 == par)(functools.partial(step_body, par))

    @pl.when(j == n_steps - 1)
    def _():
        last = (n_steps - 1) % 2
        scatter_start(j, 1, last)
        for half in range(2):
            gather_wait(half, 1 - last)
            scatter_wait(half, 1 - last)
            scatter_wait(half, last)


def _experts(blk_e, idx, u, w1, w3, w2, layer, n_steps, n_out_rows):
    wspec = lambda s: pl.BlockSpec((None, None) + s, lambda j, b, i: (layer, b[j], 0, 0))
    return pl.pallas_call(
        functools.partial(_expert_kernel, n_steps=n_steps),
        grid_spec=pltpu.PrefetchScalarGridSpec(
            num_scalar_prefetch=2,
            grid=(n_steps,),
            in_specs=[pl.BlockSpec(memory_space=pl.ANY),
                      wspec((D_MODEL, EXPERT_FF)), wspec((D_MODEL, EXPERT_FF)), wspec((EXPERT_FF, D_MODEL))],
            out_specs=pl.BlockSpec(memory_space=pl.ANY),
            scratch_shapes=[pltpu.VMEM((2, 2, MOE_HALF * TOK_ROWS, LANES), F32),
                            pltpu.VMEM((2, 2, MOE_HALF * TOK_ROWS, LANES), F32),
                            pltpu.VMEM((D_MODEL, EXPERT_FF), BF16),
                            pltpu.VMEM((D_MODEL, EXPERT_FF), BF16),
                            pltpu.VMEM((EXPERT_FF, D_MODEL), BF16),
                            pltpu.SemaphoreType.DMA((2, 2)),
                            pltpu.SemaphoreType.DMA((2, 2))],
        ),
        out_shape=jax.ShapeDtypeStruct((n_out_rows * TOK_ROWS, LANES), F32),
        compiler_params=_cparams(("arbitrary",)),
        name="moe_experts",
    )(blk_e, idx, u, w1, w3, w2)


def _combine_kernel(y0_ref, y1_ref, h_ref, gate_ref, p_ref, g_ref, wg_ref, bg_ref, wp_ref, gf_ref,
                    o_ref, *, final):
    gate = gate_ref[...]
    tm = h_ref.shape[0]
    y0 = _load_token_tiles(y0_ref, tm)
    y1 = _load_token_tiles(y1_ref, tm)
    h = h_ref[...] + y0 * gate[:, 0:1] + y1 * gate[:, 1:2]
    u = _rms(h, g_ref[...])
    sg = jax.nn.sigmoid(jnp.dot(u.astype(BF16), wg_ref[...], preferred_element_type=F32) + bg_ref[...])
    pp = jnp.dot(p_ref[...].astype(BF16), wp_ref[...], preferred_element_type=F32)
    h = h + sg * pp
    if final:
        h = _rms(h, gf_ref[...])
    o_ref[...] = h


def _combine(yp, h, gate_t, p, g, wg, bg, wp, gf, tm, final):
    t = h.shape[0]
    nt = t // tm
    return pl.pallas_call(
        functools.partial(_combine_kernel, final=final),
        grid=(nt,),
        in_specs=[pl.BlockSpec((tm * TOK_ROWS, LANES), lambda i: (i, 0)),
                  pl.BlockSpec((tm * TOK_ROWS, LANES), lambda i: (nt + i, 0)),
                  pl.BlockSpec((tm, D_MODEL), lambda i: (i, 0)),
                  pl.BlockSpec((tm, TOP_K), lambda i: (i, 0)),
                  pl.BlockSpec((tm, PLE_DIM), lambda i: (i, 0)),
                  _full((1, D_MODEL)), _full((D_MODEL, D_MODEL)), _full((1, D_MODEL)),
                  _full((PLE_DIM, D_MODEL)), _full((1, D_MODEL))],
        out_specs=pl.BlockSpec((tm, D_MODEL), lambda i: (i, 0)),
        out_shape=jax.ShapeDtypeStruct((t, D_MODEL), F32),
        compiler_params=_cparams(("parallel",)),
        name="moe_combine_ple",
    )(yp, yp, h, gate_t, p, g, wg, bg, wp, gf)


def _rope_group(x, cos, sin, lane):
    half = QK_ROPE // 2
    rot = jnp.where(lane < ROPE_MID, -pltpu.roll(x, HEAD_PAD - half, 1), pltpu.roll(x, half, 1))
    return x * cos + rot * sin


def _kvq_kernel(h_ref, gkv_ref, gmix_ref, wdkv_ref, glat_ref, wkr_ref, wuk_ref, wuv_ref,
                wdq_ref, gq_ref, wuq_ref, cos_ref, sin_ref,
                lat_ref, kpe_ref, kcat_ref, v_ref, q_ref):
    h = h_ref[...]
    tm = h.shape[0]
    cos = cos_ref[...]
    sin = sin_ref[...]
    lane = lax.broadcasted_iota(jnp.int32, (tm, HEAD_PAD), 1)
    ukv = _rms(h, gkv_ref[...]).astype(BF16)
    c = _rms(jnp.dot(ukv, wdkv_ref[...], preferred_element_type=F32), glat_ref[...])
    lat_ref[...] = c
    kpe = _rope_group(jnp.dot(ukv, wkr_ref[...], preferred_element_type=F32), cos, sin, lane)
    kpe_ref[...] = kpe
    cb = c.astype(BF16)
    v_ref[...] = jnp.dot(cb, wuv_ref[...], preferred_element_type=F32).astype(BF16)
    kn = jnp.dot(cb, wuk_ref[...], preferred_element_type=F32)
    umix = _rms(h, gmix_ref[...]).astype(BF16)
    cq = _rms(jnp.dot(umix, wdq_ref[...], preferred_element_type=F32), gq_ref[...])
    q = jnp.dot(cq.astype(BF16), wuq_ref[...], preferred_element_type=F32)
    for hd in range(N_HEADS):
        lanes = slice(hd * HEAD_PAD, (hd + 1) * HEAD_PAD)
        kcat_ref[:, lanes] = (kn[:, lanes] + kpe).astype(BF16)
        q_ref[:, lanes] = (_rope_group(q[:, lanes], cos, sin, lane) * (SCALE * LOG2E)).astype(BF16)


def _kvq(h, gkv, gmix, wdkv, glat, wkr, wuk, wuv, wdq, gq, wuq, cos, sin, tm, n_pos_blocks):
    t = h.shape[0]
    row = lambda w: pl.BlockSpec((tm, w), lambda i: (i, 0))
    return pl.pallas_call(
        _kvq_kernel,
        grid=(t // tm,),
        in_specs=[row(D_MODEL), _full((1, D_MODEL)), _full((1, D_MODEL)),
                  _full((D_MODEL, KV_LORA)), _full((1, KV_LORA)), _full((D_MODEL, HEAD_PAD)),
                  _full((KV_LORA, N_HEADS * HEAD_PAD)), _full((KV_LORA, N_HEADS * V_HEAD)),
                  _full((D_MODEL, Q_LORA)), _full((1, Q_LORA)), _full((Q_LORA, N_HEADS * HEAD_PAD)),
                  pl.BlockSpec((tm, HEAD_PAD), lambda i: (i % n_pos_blocks, 0)),
                  pl.BlockSpec((tm, HEAD_PAD), lambda i: (i % n_pos_blocks, 0))],
        out_specs=[row(KV_LORA), row(HEAD_PAD), row(N_HEADS * HEAD_PAD), row(N_HEADS * V_HEAD),
                   row(N_HEADS * HEAD_PAD)],
        out_shape=[jax.ShapeDtypeStruct((t, KV_LORA), F32),
                   jax.ShapeDtypeStruct((t, HEAD_PAD), F32),
                   jax.ShapeDtypeStruct((t, N_HEADS * HEAD_PAD), BF16),
                   jax.ShapeDtypeStruct((t, N_HEADS * V_HEAD), BF16),
                   jax.ShapeDtypeStruct((t, N_HEADS * HEAD_PAD), BF16)],
        compiler_params=_cparams(("parallel",)),
        name="kvq_proj",
    )(h, gkv, gmix, wdkv, glat, wkr, wuk, wuv, wdq, gq, wuq, cos, sin)


ATT_TQ = 512
ATT_TK = 256


def _flash_kernel(q_ref, k_ref, v_ref, o_ref):
    s_len = q_ref.shape[0]
    tq, tk = ATT_TQ, ATT_TK
    row = lax.broadcasted_iota(jnp.int32, (tq, tk), 0)
    col = lax.broadcasted_iota(jnp.int32, (tq, tk), 1)
    lane = lax.broadcasted_iota(jnp.int32, (tq, 2 * V_HEAD), 1)
    nt = (((1,), (1,)), ((), ()))
    for qi in range(s_len // tq):
        q0 = qi * tq
        qs = [q_ref[q0:q0 + tq, hh * HEAD_PAD:(hh + 1) * HEAD_PAD] for hh in range(2)]
        m = [jnp.full((tq, 1), -jnp.inf, F32) for _ in range(2)]
        l = [jnp.zeros((tq, 1), F32) for _ in range(2)]
        acc = [jnp.zeros((tq, 2 * V_HEAD), F32) for _ in range(2)]
        for kj in range((q0 + tq) // tk):
            k0 = kj * tk
            v = v_ref[k0:k0 + tk, :]
            for hh in range(2):
                k = k_ref[k0:k0 + tk, hh * HEAD_PAD:(hh + 1) * HEAD_PAD]
                s = lax.dot_general(qs[hh], k, nt, preferred_element_type=F32)
                if k0 + tk - 1 > q0:
                    s = jnp.where(col + k0 <= row + q0, s, -jnp.inf)
                m_new = jnp.maximum(m[hh], jnp.max(s, axis=-1, keepdims=True))
                alpha = jnp.exp2(m[hh] - m_new)
                p = jnp.exp2(s - m_new)
                l[hh] = alpha * l[hh] + jnp.sum(p, axis=-1, keepdims=True)
                acc[hh] = alpha * acc[hh] + jnp.dot(p.astype(BF16), v, preferred_element_type=F32)
                m[hh] = m_new
        o = jnp.where(lane < V_HEAD, acc[0] / l[0], acc[1] / l[1])
        o_ref[q0:q0 + tq, :] = o.astype(BF16)


def _flash(q, kcat, v, b, s):
    return pl.pallas_call(
        _flash_kernel,
        grid=(b, N_HEADS // 2),
        in_specs=[pl.BlockSpec((s, 2 * HEAD_PAD), lambda i, h: (i, h)),
                  pl.BlockSpec((s, 2 * HEAD_PAD), lambda i, h: (i, h)),
                  pl.BlockSpec((s, 2 * V_HEAD), lambda i, h: (i, h))],
        out_specs=pl.BlockSpec((s, 2 * V_HEAD), lambda i, h: (i, h)),
        out_shape=jax.ShapeDtypeStruct((b * s, N_HEADS * V_HEAD), BF16),
        compiler_params=_cparams(("parallel", "parallel")),
        name="flash_attn",
    )(q, kcat, v)


def _proj_res_kernel(a_ref, w_ref, h_ref, o_ref):
    o_ref[...] = h_ref[...] + jnp.dot(a_ref[...], w_ref[...], preferred_element_type=F32)


def _proj_res(a, w, h, tm):
    t, k = a.shape
    return pl.pallas_call(
        _proj_res_kernel,
        grid=(t // tm,),
        in_specs=[pl.BlockSpec((tm, k), lambda i: (i, 0)), _full(w.shape),
                  pl.BlockSpec((tm, D_MODEL), lambda i: (i, 0))],
        out_specs=pl.BlockSpec((tm, D_MODEL), lambda i: (i, 0)),
        out_shape=jax.ShapeDtypeStruct((t, D_MODEL), F32),
        compiler_params=_cparams(("parallel",)),
        name="proj_residual",
    )(a, w, h)


def _qlat_kernel(q_ref, wukt_ref, o_ref):
    for hd in range(N_HEADS):
        qh = q_ref[:, hd * HEAD_PAD:(hd + 1) * HEAD_PAD]
        o_ref[:, hd * KV_LORA:(hd + 1) * KV_LORA] = jnp.dot(
            qh, wukt_ref[hd], preferred_element_type=F32).astype(BF16)


def _qlat(q, wukt):
    n = q.shape[0]
    return pl.pallas_call(
        _qlat_kernel,
        in_specs=[_full(q.shape), _full(wukt.shape)],
        out_specs=_full((n, N_HEADS * KV_LORA)),
        out_shape=jax.ShapeDtypeStruct((n, N_HEADS * KV_LORA), BF16),
        grid=(1,),
        compiler_params=_cparams(("arbitrary",)),
        name="q_absorb",
    )(q, wukt)


PAGES_PER_CHUNK = 32
CHUNK_SLOTS = 4
CHUNK_AHEAD = 2


def _paged_kernel(pt_ref, ql_ref, qp_ref, cn_ref, kn_ref, lat_hbm, kpt_hbm, o_ref, lbuf, kbuf, sems, *, nb):
    b = pl.program_id(0)
    n_pages = pt_ref.shape[0] // nb
    cp = PAGES_PER_CHUNK
    n_chunks = n_pages // cp
    assert n_chunks == CHUNK_SLOTS
    total = nb * n_chunks

    def start_chunk(g, slot):
        gg = jnp.where(g < total, g, 0)
        for j in range(cp):
            page = pt_ref[gg * cp + j]
            src = pl.ds(pl.multiple_of(page * PAGE_SIZE, PAGE_SIZE), PAGE_SIZE)
            pltpu.make_async_copy(lat_hbm.at[src], lbuf.at[slot, pl.ds(j * PAGE_SIZE, PAGE_SIZE)],
                                  sems.at[0, slot]).start()
            pltpu.make_async_copy(kpt_hbm.at[page], kbuf.at[slot, j], sems.at[1, slot]).start()

    def wait_chunk(g, slot):
        del g
        pltpu.make_async_copy(lat_hbm.at[pl.ds(0, cp * PAGE_SIZE)], lbuf.at[slot], sems.at[0, slot]).wait()
        pltpu.make_async_copy(kpt_hbm.at[pl.ds(0, cp)], kbuf.at[slot], sems.at[1, slot]).wait()

    @pl.when(b == 0)
    def _():
        for a in range(CHUNK_AHEAD):
            start_chunk(a, a)

    ql = ql_ref[...]
    qp = qp_ref[...]
    nt = (((1,), (1,)), ((), ()))
    m = jnp.full((N_HEADS, 1), -jnp.inf, F32)
    l = jnp.zeros((N_HEADS, 1), F32)
    acc = jnp.zeros((N_HEADS, KV_LORA), F32)
    rows = cp * PAGE_SIZE
    for c0 in range(0, n_chunks, CHUNK_AHEAD):
        g = b * n_chunks + c0
        group = range(c0, c0 + CHUNK_AHEAD)
        for a in range(CHUNK_AHEAD):
            start_chunk(g + CHUNK_AHEAD + a, (c0 + CHUNK_AHEAD + a) % CHUNK_SLOTS)
        for a in range(CHUNK_AHEAD):
            wait_chunk(g + a, c0 + a)
        kcs = [lbuf[c].astype(BF16) for c in group]
        kps = [jnp.concatenate([kbuf[c, jj] for jj in range(cp)], axis=1).astype(BF16)
               for c in group]
        s = jnp.concatenate([lax.dot_general(ql, kc, nt, preferred_element_type=F32)
                             + jnp.dot(qp, kp, preferred_element_type=F32)
                             for kc, kp in zip(kcs, kps)], axis=1)
        m_new = jnp.maximum(m, jnp.max(s, axis=-1, keepdims=True))
        alpha = jnp.exp2(m - m_new)
        p = jnp.exp2(s - m_new)
        l = alpha * l + jnp.sum(p, axis=-1, keepdims=True)
        acc = alpha * acc
        for a, kc in enumerate(kcs):
            acc = acc + jnp.dot(p[:, a * rows:(a + 1) * rows].astype(BF16), kc, preferred_element_type=F32)
        m = m_new
    cn = cn_ref[...].astype(BF16).astype(F32)
    kn = kn_ref[...].astype(BF16).astype(F32)
    s = (jnp.sum(ql.astype(F32) * cn, axis=-1, keepdims=True)
         + jnp.sum(qp.astype(F32) * kn, axis=-1, keepdims=True))
    m_new = jnp.maximum(m, s)
    alpha = jnp.exp2(m - m_new)
    p = jnp.exp2(s - m_new)
    l = alpha * l + p
    acc = alpha * acc + p.astype(BF16).astype(F32) * cn
    o_ref[...] = acc / l

    @pl.when(b == nb - 1)
    def _():
        for a in range(CHUNK_AHEAD):
            wait_chunk(total + a, a)


def _paged_attn(pt_flat, ql, qp, c_new, k_new, cache_latent, cache_kpt):
    n = ql.shape[0]
    rows = PAGES_PER_CHUNK * PAGE_SIZE
    per_seq = lambda w: pl.BlockSpec((None,) + w, lambda i, pt: (i, 0, 0))
    return pl.pallas_call(
        functools.partial(_paged_kernel, nb=n),
        grid_spec=pltpu.PrefetchScalarGridSpec(
            num_scalar_prefetch=1,
            grid=(n,),
            in_specs=[per_seq((N_HEADS, KV_LORA)), per_seq((N_HEADS, QK_ROPE)),
                      per_seq((1, KV_LORA)), per_seq((1, QK_ROPE)),
                      pl.BlockSpec(memory_space=pl.ANY), pl.BlockSpec(memory_space=pl.ANY)],
            out_specs=per_seq((N_HEADS, KV_LORA)),
            scratch_shapes=[pltpu.VMEM((CHUNK_SLOTS, rows, KV_LORA), F32),
                            pltpu.VMEM((CHUNK_SLOTS, PAGES_PER_CHUNK, QK_ROPE, PAGE_SIZE), F32),
                            pltpu.SemaphoreType.DMA((2, CHUNK_SLOTS))],
        ),
        out_shape=jax.ShapeDtypeStruct((n, N_HEADS, KV_LORA), F32),
        compiler_params=_cparams(("arbitrary",)),
        name="paged_attn",
    )(pt_flat, ql, qp, c_new, k_new, cache_latent, cache_kpt)


def _attn_out_kernel(ol_ref, wuv_ref, wo_ref, h_ref, o_ref):
    acc = h_ref[...]
    for hd in range(N_HEADS):
        oh = ol_ref[:, hd * KV_LORA:(hd + 1) * KV_LORA].astype(BF16)
        th = jnp.dot(oh, wuv_ref[hd], preferred_element_type=F32).astype(BF16)
        acc = acc + jnp.dot(th, wo_ref[hd * V_HEAD:(hd + 1) * V_HEAD, :], preferred_element_type=F32)
    o_ref[...] = acc


def _attn_out(ol, wuv_h, wo, h):
    n = h.shape[0]
    return pl.pallas_call(
        _attn_out_kernel,
        grid=(1,),
        in_specs=[_full(ol.shape), _full(wuv_h.shape), _full(wo.shape), _full(h.shape)],
        out_specs=_full(h.shape),
        out_shape=jax.ShapeDtypeStruct((n, D_MODEL), F32),
        compiler_params=_cparams(("arbitrary",)),
        name="attn_out",
    )(ol, wuv_h, wo, h)


def _rope_tables(pos):
    half = QK_ROPE // 2
    inv = ROPE_THETA ** (-jnp.arange(half, dtype=F32) / half)
    ang = pos.astype(F32)[:, None] * inv[None, :]
    cos, sin = jnp.cos(ang), jnp.sin(ang)
    n = pos.shape[0]
    cos_t = jnp.concatenate([jnp.ones((n, QK_NOPE), F32), cos, cos,
                             jnp.ones((n, HEAD_PAD - ROPE_HI), F32)], axis=1)
    sin_t = jnp.concatenate([jnp.zeros((n, QK_NOPE), F32), sin, sin,
                             jnp.zeros((n, HEAD_PAD - ROPE_HI), F32)], axis=1)
    return cos_t, sin_t


def _moe(h, g_ffn, wr, br, w1, w3, w2, layer, tm):
    t = h.shape[0]
    u, code, gate, cnt = _router(h, g_ffn, wr, br, tm)
    counts = cnt[:, 0].astype(jnp.int32)
    padded = (counts + MOE_BM - 1) // MOE_BM * MOE_BM
    pends = jnp.cumsum(padded)
    pstarts = pends - padded
    n_steps = (t * TOP_K + N_EXPERTS * (MOE_BM - 1) + MOE_BM - 1) // MOE_BM
    n_rows = n_steps * MOE_BM
    eid = code >> RANK_BITS
    eiota = jnp.arange(N_EXPERTS, dtype=jnp.int32)
    start_of = jnp.sum(jnp.where(eid[..., None] == eiota, pstarts, 0), axis=-1)
    dest = (start_of + (code & ((1 << RANK_BITS) - 1))).reshape(-1)
    block_start = jnp.arange(n_steps, dtype=jnp.int32) * MOE_BM
    blk_e = jnp.minimum(jnp.sum((pends[None, :] <= block_start[:, None]).astype(jnp.int32), axis=1),
                        N_EXPERTS - 1)
    inv = jnp.full((n_rows,), -1, jnp.int32).at[dest].set(
        jnp.arange(TOP_K * t, dtype=jnp.int32), unique_indices=True)
    is_pad = inv < 0
    rows = jnp.arange(n_rows, dtype=jnp.int32)
    out_row = jnp.where(is_pad, TOP_K * t - 1 + jnp.cumsum(is_pad.astype(jnp.int32)), inv)
    src_tok = jnp.where(is_pad, rows, inv) & (t - 1)
    n_out = n_rows + 2 * MOE_BM
    assert t & (t - 1) == 0 and t <= (1 << TOK_BITS) and n_out <= (1 << (32 - TOK_BITS))
    idx = (out_row << TOK_BITS) | src_tok
    yp = _experts(blk_e, idx, u, w1, w3, w2, layer, n_steps, n_out)
    return yp, gate.T


def kernel(x_prompt, x_sample, cache_conv, cache_latent, cache_kpe, page_table, p_prompt, p_sample, g_mix, g_ffn, g_ple, g_final, conv_w_in, conv_b_in, conv_w_dw, conv_b_dw, conv_ln_g, conv_ln_b, conv_w_out, conv_b_out, g_kv_in, w_dkv, g_kv_lat, w_kr, w_uk, w_uv, w_dq, g_q_lat, w_uq, w_o, w_group, b_group, w_inner, b_inner, w_exp_gate, w_exp_up, w_exp_down, w_ple_gate, b_ple_gate, w_ple_proj):
    bp, sp, d = x_prompt.shape
    nd = x_sample.shape[0]
    tp = bp * sp
    row = lambda a: a.reshape(1, -1)

    w_in_b = conv_w_in[0].astype(BF16)
    w_out_b = conv_w_out[0].astype(BF16)
    wdw_pad = jnp.concatenate([conv_w_dw[0], jnp.zeros((HALO - CONV_W, d), F32)], axis=0)
    wr, br = [], []
    for i in range(2):
        tail = ROUTER_ROWS - 8 - N_EXPERTS
        wr.append(jnp.concatenate([w_group[i].T, jnp.zeros((8 - N_GROUPS, d), F32),
                                   w_inner[i].reshape(d, N_EXPERTS).T, jnp.zeros((tail, d), F32)], axis=0))
        br.append(jnp.concatenate([b_group[i], jnp.zeros((8 - N_GROUPS,), F32),
                                   b_inner[i].reshape(N_EXPERTS), jnp.zeros((tail,), F32)]).reshape(ROUTER_ROWS, 1))
    w_dkv_b = w_dkv.astype(BF16)
    w_kr_pad = jnp.zeros((d, HEAD_PAD), F32).at[:, ROPE_LO:ROPE_HI].set(w_kr).astype(BF16)
    w_uk_pad = jnp.pad(w_uk, ((0, 0), (0, 0), (0, HEAD_PAD - QK_NOPE))).reshape(KV_LORA, -1).astype(BF16)
    w_uv_b = w_uv.reshape(KV_LORA, -1).astype(BF16)
    w_dq_b = w_dq[0].astype(BF16)
    w_uq_pad = jnp.pad(w_uq[0].reshape(Q_LORA, N_HEADS, QK_NOPE + QK_ROPE),
                       ((0, 0), (0, 0), (0, HEAD_PAD - ROPE_HI))).reshape(Q_LORA, -1).astype(BF16)
    w_o_b = w_o[0].astype(BF16)
    w_ukt_pad = jnp.pad(jnp.transpose(w_uk, (1, 2, 0)),
                        ((0, 0), (0, HEAD_PAD - QK_NOPE), (0, 0))).astype(BF16)
    w_uv_h = jnp.transpose(w_uv, (1, 0, 2)).astype(BF16)
    wg_b = [w_ple_gate[i].astype(BF16) for i in range(2)]
    wp_b = [w_ple_proj[i].astype(BF16) for i in range(2)]

    def moe_ple(h, layer, p, tm, final):
        yp, gate_t = _moe(h, row(g_ffn[layer]), wr[layer], br[layer],
                          w_exp_gate, w_exp_up, w_exp_down, layer, tm)
        return _combine(yp, h, gate_t, p, row(g_ple[layer]), wg_b[layer], row(b_ple_gate[layer]),
                        wp_b[layer], row(g_final), tm, final)

    def kvq(h, cos_t, sin_t, tm, n_pos_blocks):
        return _kvq(h, row(g_kv_in), row(g_mix[1]), w_dkv_b, row(g_kv_lat), w_kr_pad, w_uk_pad, w_uv_b,
                    w_dq_b, row(g_q_lat[0]), w_uq_pad, cos_t, sin_t, tm, n_pos_blocks)

    xp = x_prompt.reshape(tp, d)
    z = _conv_in(xp, row(g_mix[0]), w_in_b, row(conv_b_in[0]), 512)
    z3 = z.reshape(bp, sp, d)
    conv_prompt = z3[:, sp - (CONV_W - 1):, :][None]
    ctx0 = jnp.zeros((bp, HALO, d), F32)
    h = _conv_mix(z3, ctx0, x_prompt, wdw_pad, row(conv_b_dw[0]), row(conv_ln_g[0]), row(conv_ln_b[0]),
                  w_out_b, row(conv_b_out[0])).reshape(tp, d)
    h = moe_ple(h, 0, p_prompt[0].reshape(tp, PLE_DIM), 512, False)
    cos_p, sin_p = _rope_tables(jnp.arange(sp, dtype=jnp.int32))
    lat_p, kpe_p, kcat, v, q = kvq(h, cos_p, sin_p, 512, sp // 512)
    o = _flash(q, kcat, v, bp, sp)
    h = _proj_res(o, w_o_b, h, 512)
    y_prompt = moe_ple(h, 1, p_prompt[1].reshape(tp, PLE_DIM), 512, True).reshape(bp, sp, d)
    latent_prompt = lat_p.reshape(bp, sp, KV_LORA)
    kpe_prompt = kpe_p[:, ROPE_LO:ROPE_HI].reshape(bp, sp, QK_ROPE)

    xs = x_sample.reshape(nd, d)
    zs = _conv_in(xs, row(g_mix[0]), w_in_b, row(conv_b_in[0]), nd)
    ctx_t = jnp.transpose(cache_conv[0], (1, 0, 2))
    conv_sample = jnp.transpose(jnp.concatenate([ctx_t[1:], zs[None]], axis=0), (1, 0, 2))[None]
    hs = _conv_step(ctx_t, conv_w_dw[0][:, None, :], zs, xs,
                    row(conv_b_dw[0]), row(conv_ln_g[0]), row(conv_ln_b[0]), w_out_b, row(conv_b_out[0]))
    hs = moe_ple(hs, 0, p_sample[0].reshape(nd, PLE_DIM), nd, False)
    n_pages = page_table.shape[1]
    pos_s = jnp.full((nd,), n_pages * PAGE_SIZE, jnp.int32)
    cos_s, sin_s = _rope_tables(pos_s)
    lat_s, kpe_s, _, _, qs = kvq(hs, cos_s, sin_s, nd, 1)
    ql = _qlat(qs, w_ukt_pad).reshape(nd, N_HEADS, KV_LORA)
    qp = qs.reshape(nd, N_HEADS, HEAD_PAD)[:, :, ROPE_LO:ROPE_HI]
    kpe_new = kpe_s[:, ROPE_LO:ROPE_HI]
    o_lat = _paged_attn(page_table.reshape(-1), ql, qp, lat_s.reshape(nd, 1, KV_LORA),
                        kpe_new.reshape(nd, 1, QK_ROPE), cache_latent.reshape(-1, KV_LORA),
                        jnp.transpose(cache_kpe, (0, 2, 1)))
    hs = _attn_out(o_lat.reshape(nd, N_HEADS * KV_LORA), w_uv_h, w_o_b, hs)
    y_sample = moe_ple(hs, 1, p_sample[1].reshape(nd, PLE_DIM), nd, True).reshape(nd, 1, d)
    latent_sample = lat_s.reshape(nd, 1, KV_LORA)
    kpe_sample = kpe_new.reshape(nd, 1, QK_ROPE)

    return (y_prompt, y_sample, conv_prompt, conv_sample, latent_prompt, kpe_prompt, latent_sample, kpe_sample)
```

```python
import functools
import math

import jax
import jax.numpy as jnp
from jax import lax
from jax.experimental import pallas as pl
from jax.experimental.pallas import tpu as pltpu

F32 = jnp.float32
BF16 = jnp.bfloat16

D_MODEL = 1024
PLE_DIM = 256
CONV_W = 31
N_HEADS = 16
QK_NOPE = 64
QK_ROPE = 32
V_HEAD = 64
Q_LORA = 384
KV_LORA = 256
ROPE_THETA = 10000.0
SCALE = 1.0 / math.sqrt(QK_NOPE + QK_ROPE)
LOG2E = 1.4426950408889634
N_GROUPS = 4
EXPERTS_PER_GROUP = 8
N_EXPERTS = N_GROUPS * EXPERTS_PER_GROUP
TOP_K = 2
EXPERT_FF = 512
PAGE_SIZE = 128
EPS = 1e-6

HEAD_PAD = 128
ROPE_LO = QK_NOPE
ROPE_MID = QK_NOPE + QK_ROPE // 2
ROPE_HI = QK_NOPE + QK_ROPE
HALO = 32
MOE_HALF = 128
MOE_BM = 2 * MOE_HALF
ROUTER_ROWS = 48
RANK_BITS = 20
VMEM_LIMIT = 56 * 1024 * 1024


def _cparams(sem):
    return pltpu.CompilerParams(dimension_semantics=sem, vmem_limit_bytes=VMEM_LIMIT)


def _rms(x, g):
    return x * lax.rsqrt(jnp.mean(x * x, axis=-1, keepdims=True) + EPS) * g


def _full(shape):
    n = len(shape)
    return pl.BlockSpec(shape, lambda *_: (0,) * n)


LANES = 128
SUBLANES = 8
TOK_ROWS = D_MODEL // LANES


def _load_token_tiles(ref, n):
    return jnp.concatenate([ref[pl.ds(s, n, stride=TOK_ROWS), :] for s in range(TOK_ROWS)], axis=1)


def _store_token_tiles(ref, x):
    n = x.shape[0]
    for s in range(TOK_ROWS):
        ref[pl.ds(s, n, stride=TOK_ROWS), :] = x[:, s * LANES:(s + 1) * LANES]


def _conv_in_kernel(x_ref, g_ref, w_ref, b_ref, z_ref):
    u = _rms(x_ref[...], g_ref[...])
    a = jnp.dot(u.astype(BF16), w_ref[...], preferred_element_type=F32) + b_ref[...]
    z_ref[...] = a[:, :D_MODEL] * jax.nn.sigmoid(a[:, D_MODEL:])


def _conv_in(x, g, w_in, b_in, tm):
    t = x.shape[0]
    return pl.pallas_call(
        _conv_in_kernel,
        grid=(t // tm,),
        in_specs=[pl.BlockSpec((tm, D_MODEL), lambda i: (i, 0)),
                  _full((1, D_MODEL)), _full((D_MODEL, 2 * D_MODEL)), _full((1, 2 * D_MODEL))],
        out_specs=pl.BlockSpec((tm, D_MODEL), lambda i: (i, 0)),
        out_shape=jax.ShapeDtypeStruct((t, D_MODEL), F32),
        compiler_params=_cparams(("parallel",)),
        name="conv_in",
    )(x, g, w_in, b_in)


def _ln_silu_out(y, lng, lnb, wout, bout):
    mu = jnp.mean(y, axis=-1, keepdims=True)
    yc = y - mu
    var = jnp.mean(yc * yc, axis=-1, keepdims=True)
    n = yc * lax.rsqrt(var + EPS) * lng + lnb
    act = n * jax.nn.sigmoid(n)
    return jnp.dot(act.astype(BF16), wout, preferred_element_type=F32) + bout


CONV_TS = 256
CONV_RC = 128
CONV_LC = 256


def _conv_mix_kernel(z_ref, halo_ref, ctx_ref, x_ref, wdw_ref, bdw_ref, lng_ref, lnb_ref,
                     wout_ref, bout_ref, h_ref, buf_ref, y_ref):
    s = pl.program_id(1)

    @pl.when(s == 0)
    def _():
        buf_ref[0, 0:HALO, :] = ctx_ref[...]

    @pl.when(s > 0)
    def _():
        buf_ref[0, 0:HALO, :] = halo_ref[...]

    buf_ref[0, HALO:, :] = z_ref[...]
    shifted_rows = CONV_TS + HALO - SUBLANES
    for sh in range(1, SUBLANES):
        buf_ref[sh, 0:shifted_rows, :] = buf_ref[0, sh:sh + shifted_rows, :]
    first = HALO - (CONV_W - 1)
    for lc in range(D_MODEL // CONV_LC):
        lanes = slice(lc * CONV_LC, (lc + 1) * CONV_LC)
        for rc in range(CONV_TS // CONV_RC):
            r0 = rc * CONV_RC
            acc = jnp.broadcast_to(bdw_ref[:, lanes], (CONV_RC, CONV_LC))
            for k in range(CONV_W):
                sh = (first + k) % SUBLANES
                a0 = r0 + first + k - sh
                acc = acc + wdw_ref[k:k + 1, lanes] * buf_ref[sh, a0:a0 + CONV_RC, lanes]
            y_ref[r0:r0 + CONV_RC, lanes] = acc
    out = _ln_silu_out(y_ref[...], lng_ref[...], lnb_ref[...], wout_ref[...], bout_ref[...])
    h_ref[...] = x_ref[...] + out


def _conv_mix(z3, ctx, x3, wdw, bdw, lng, lnb, wout, bout):
    b, s, d = z3.shape
    ts = CONV_TS
    hb = ts // HALO
    return pl.pallas_call(
        _conv_mix_kernel,
        grid=(b, s // ts),
        in_specs=[pl.BlockSpec((None, ts, d), lambda i, j: (i, j, 0)),
                  pl.BlockSpec((None, HALO, d), lambda i, j: (i, jnp.maximum(j * hb - 1, 0), 0)),
                  pl.BlockSpec((None, HALO, d), lambda i, j: (i, 0, 0)),
                  pl.BlockSpec((None, ts, d), lambda i, j: (i, j, 0)),
                  _full((HALO, d)), _full((1, d)), _full((1, d)), _full((1, d)),
                  _full((d, d)), _full((1, d))],
        out_specs=pl.BlockSpec((None, ts, d), lambda i, j: (i, j, 0)),
        out_shape=jax.ShapeDtypeStruct((b, s, d), F32),
        scratch_shapes=[pltpu.VMEM((SUBLANES, ts + HALO, d), F32), pltpu.VMEM((ts, d), F32)],
        compiler_params=_cparams(("parallel", "arbitrary")),
        name="conv_mix",
    )(z3, z3, ctx, x3, wdw, bdw, lng, lnb, wout, bout)


def _conv_step_kernel(ctx_ref, w_ref, z_ref, x_ref, bdw_ref, lng_ref, lnb_ref, wout_ref, bout_ref,
                      h_ref, acc_ref):
    k = pl.program_id(0)

    @pl.when(k == 0)
    def _():
        acc_ref[...] = jnp.broadcast_to(bdw_ref[...], acc_ref.shape)

    @pl.when(k < CONV_W - 1)
    def _():
        acc_ref[...] += w_ref[...] * ctx_ref[...]

    @pl.when(k == CONV_W - 1)
    def _():
        y = acc_ref[...] + w_ref[...] * z_ref[...]
        out = _ln_silu_out(y, lng_ref[...], lnb_ref[...], wout_ref[...], bout_ref[...])
        h_ref[...] = x_ref[...] + out


def _conv_step(ctx_t, wdw3, z, x, bdw, lng, lnb, wout, bout):
    n, d = z.shape
    last = CONV_W - 2
    return pl.pallas_call(
        _conv_step_kernel,
        grid=(CONV_W,),
        in_specs=[pl.BlockSpec((None, n, d), lambda k: (jnp.minimum(k, last), 0, 0)),
                  pl.BlockSpec((None, 1, d), lambda k: (k, 0, 0)),
                  _full((n, d)), _full((n, d)), _full((1, d)), _full((1, d)), _full((1, d)),
                  _full((d, d)), _full((1, d))],
        out_specs=_full((n, d)),
        out_shape=jax.ShapeDtypeStruct((n, d), F32),
        scratch_shapes=[pltpu.VMEM((n, d), F32)],
        compiler_params=_cparams(("arbitrary",)),
        name="conv_step",
    )(ctx_t, wdw3, z, x, bdw, lng, lnb, wout, bout)


def _router_kernel(*refs, with_proj):
    if with_proj:
        (h_ref, g_ref, wr_ref, br_ref, before_ref, a_ref, w_ref,
         u_ref, code_ref, gate_ref, cnt_ref, hout_ref, carry_ref) = refs
    else:
        h_ref, g_ref, wr_ref, br_ref, before_ref, u_ref, code_ref, gate_ref, cnt_ref, carry_ref = refs
    i = pl.program_id(0)

    @pl.when(i == 0)
    def _():
        carry_ref[...] = jnp.zeros_like(carry_ref)

    h = h_ref[...]
    if with_proj:
        h = h + jnp.dot(a_ref[...], w_ref[...], preferred_element_type=F32)
        hout_ref[...] = h
    u = _rms(h, g_ref[...])
    _store_token_tiles(u_ref, u)
    nt = (((1,), (1,)), ((), ()))
    u_hi = u.astype(BF16)
    u_lo = (u - u_hi.astype(F32)).astype(BF16)
    part = lax.dot_general(wr_ref[...], u_hi, nt, preferred_element_type=F32)
    logits = (part[0:ROUTER_ROWS] + part[ROUTER_ROWS:]
              + lax.dot_general(wr_ref[0:ROUTER_ROWS, :], u_lo, nt, preferred_element_type=F32) + br_ref[...])
    gl = logits[0:N_GROUPS]
    il = logits[8:8 + N_EXPERTS]
    gmax = jnp.max(gl, axis=0, keepdims=True)
    giota = lax.broadcasted_iota(jnp.int32, gl.shape, 0)
    gsel = jnp.min(jnp.where(gl == gmax, giota, N_GROUPS), axis=0, keepdims=True)
    gw = 1.0 / jnp.sum(jnp.exp(gl - gmax), axis=0, keepdims=True)
    eiota = lax.broadcasted_iota(jnp.int32, il.shape, 0)
    ml = jnp.where((eiota >> 3) == gsel, il, -jnp.inf)
    v1 = jnp.max(ml, axis=0, keepdims=True)
    i1 = jnp.min(jnp.where(ml == v1, eiota, N_EXPERTS), axis=0, keepdims=True)
    ml2 = jnp.where(eiota == i1, -jnp.inf, ml)
    v2 = jnp.max(ml2, axis=0, keepdims=True)
    i2 = jnp.min(jnp.where(ml2 == v2, eiota, N_EXPERTS), axis=0, keepdims=True)
    e2 = jnp.exp(v2 - v1)
    p1 = 1.0 / (1.0 + e2)
    gate_ref[0:1, :] = p1 * gw
    gate_ref[1:2, :] = e2 * p1 * gw
    hit1 = eiota == i1
    hit2 = eiota == i2
    onehot = jnp.where(hit1 | hit2, 1.0, 0.0)
    cum = jnp.dot(onehot.astype(BF16), before_ref[...], preferred_element_type=F32) + carry_ref[:, 0:1]
    rank1 = jnp.sum(jnp.where(hit1, cum, 0.0), axis=0, keepdims=True).astype(jnp.int32)
    rank2 = jnp.sum(jnp.where(hit2, cum, 0.0), axis=0, keepdims=True).astype(jnp.int32)
    code_ref[0:1, :] = (i1 << RANK_BITS) | rank1
    code_ref[1:2, :] = (i2 << RANK_BITS) | rank2
    carry_ref[...] += jnp.sum(onehot, axis=1, keepdims=True)
    cnt_ref[...] = carry_ref[...]


def _router(h, g, wr, br, tm, proj=None):
    t = h.shape[0]
    row2 = pl.BlockSpec((TOP_K, tm), lambda i: (0, i))
    rows = pl.BlockSpec((tm, D_MODEL), lambda i: (i, 0))
    before = (jnp.arange(tm)[:, None] < jnp.arange(tm)[None, :]).astype(BF16)
    wr_hi = wr.astype(BF16)
    wr_split = jnp.concatenate([wr_hi, (wr - wr_hi.astype(F32)).astype(BF16)], axis=0)
    in_specs = [rows, _full((1, D_MODEL)), _full((2 * ROUTER_ROWS, D_MODEL)), _full((ROUTER_ROWS, 1)),
                _full((tm, tm))]
    out_specs = [pl.BlockSpec((tm * TOK_ROWS, LANES), lambda i: (i, 0)), row2, row2, _full((N_EXPERTS, 128))]
    out_shape = [jax.ShapeDtypeStruct((t * TOK_ROWS, LANES), F32),
                 jax.ShapeDtypeStruct((TOP_K, t), jnp.int32),
                 jax.ShapeDtypeStruct((TOP_K, t), F32),
                 jax.ShapeDtypeStruct((N_EXPERTS, 128), F32)]
    args = [h, g, wr_split, br, before]
    if proj is not None:
        a, w = proj
        in_specs += [pl.BlockSpec((tm, a.shape[1]), lambda i: (i, 0)), _full(w.shape)]
        out_specs.append(rows)
        out_shape.append(jax.ShapeDtypeStruct((t, D_MODEL), F32))
        args += [a, w]
    outs = pl.pallas_call(
        functools.partial(_router_kernel, with_proj=proj is not None),
        grid=(t // tm,),
        in_specs=in_specs,
        out_specs=out_specs,
        out_shape=out_shape,
        scratch_shapes=[pltpu.VMEM((N_EXPERTS, 128), F32)],
        compiler_params=_cparams(("arbitrary",)),
        name="router",
    )(*args)
    if proj is not None:
        return (outs[4],) + tuple(outs[:4])
    return (h,) + tuple(outs)


def _expert_kernel(blk_ref, meta_ref, order_ref, u_hbm, w1_ref, w3_ref, w2_ref, yp_hbm, xbuf, ybuf,
                   w1b, w3b, w2b, gsem, ssem, *, n_tok, n_steps):
    j = pl.program_id(0)
    prime0 = n_steps * MOE_BM
    half_rows = MOE_HALF * TOK_ROWS

    def tile_of(t):
        return pl.ds(pl.multiple_of(t * TOK_ROWS, TOK_ROWS), TOK_ROWS)

    def sorted_rows(step, half):
        shift = meta_ref[step]
        pairs_end = meta_ref[n_steps + step]
        base = step * MOE_BM + half * MOE_HALF
        for r in range(MOE_HALF):
            g = base + r
            valid = g < pairs_end
            yield r, valid, order_ref[jnp.where(valid, g - shift, 0)], g

    def gather_start(step, half, par):
        for r, valid, pair, g in sorted_rows(step, half):
            tok = jnp.where(valid, pair, g) & (n_tok - 1)
            pltpu.make_async_copy(u_hbm.at[tile_of(tok)], xbuf.at[par, half, tile_of(r)],
                                  gsem.at[par, half]).start()

    def gather_wait(half, par):
        pltpu.make_async_copy(u_hbm.at[pl.ds(0, half_rows)], xbuf.at[par, half], gsem.at[par, half]).wait()

    def prime_rows(half, par):
        return prime0 + (2 * par + half) * MOE_HALF

    def scatter_start(step, half, par, to_prime=None):
        pairs_before = meta_ref[2 * n_steps + step]
        for r, valid, pair, g in sorted_rows(step, half):
            dst = jnp.where(valid, pair, TOP_K * n_tok + g - pairs_before)
            if to_prime is not None:
                dst = jnp.where(to_prime, prime_rows(half, par) + r, dst)
            pltpu.make_async_copy(ybuf.at[par, half, tile_of(r)], yp_hbm.at[tile_of(dst)],
                                  ssem.at[par, half]).start()

    def scatter_wait(half, par):
        pltpu.make_async_copy(ybuf.at[par, half], yp_hbm.at[pl.ds(0, half_rows)], ssem.at[par, half]).wait()

    @pl.when(j == 0)
    def _():
        ybuf[...] = jnp.zeros_like(ybuf)
        for par, half in ((0, 0), (0, 1), (1, 0)):
            pltpu.make_async_copy(ybuf.at[par, half],
                                  yp_hbm.at[pl.ds(prime_rows(half, par) * TOK_ROWS, half_rows)],
                                  ssem.at[par, half]).start()
        gather_start(0, 0, 0)
        gather_start(0, 1, 0)

    e = blk_ref[j]
    prev = blk_ref[jnp.maximum(j - 1, 0)]

    @pl.when((j == 0) | (e != prev))
    def _():
        w1b[...] = w1_ref[...].astype(BF16)
        w3b[...] = w3_ref[...].astype(BF16)
        w2b[...] = w2_ref[...].astype(BF16)

    def compute(half, par):
        x = _load_token_tiles(xbuf.at[par, half], MOE_HALF).astype(BF16)
        a = jnp.dot(x, w1b[...], preferred_element_type=F32)
        b = jnp.dot(x, w3b[...], preferred_element_type=F32)
        hmid = (a * jax.nn.sigmoid(a)) * b
        _store_token_tiles(ybuf.at[par, half],
                           jnp.dot(hmid.astype(BF16), w2b[...], preferred_element_type=F32))

    nxt = jnp.minimum(j + 1, n_steps - 1)

    def step_body(par):
        gather_wait(0, par)
        scatter_wait(0, par)
        gather_start(nxt, 0, 1 - par)
        scatter_start(jnp.maximum(j - 1, 0), 1, 1 - par, to_prime=(j == 0))
        compute(0, par)
        gather_wait(1, par)
        scatter_wait(1, par)
        gather_start(nxt, 1, 1 - par)
        scatter_start(j, 0, par)
        compute(1, par)

    for par in range(2):
        pl.when(j % 2 == par)(functools.partial(step_body, par))

    @pl.when(j == n_steps - 1)
    def _():
        last = (n_steps - 1) % 2
        scatter_start(j, 1, last)
        for half in range(2):
            gather_wait(half, 1 - last)
            scatter_wait(half, 1 - last)
            scatter_wait(half, last)


def _experts(blk_e, meta, order, u, w1, w3, w2, layer, n_steps, n_out_rows):
    wspec = lambda s: pl.BlockSpec((None, None) + s, lambda j, b, m, o: (layer, b[j], 0, 0))
    return pl.pallas_call(
        functools.partial(_expert_kernel, n_tok=u.shape[0] // TOK_ROWS, n_steps=n_steps),
        grid_spec=pltpu.PrefetchScalarGridSpec(
            num_scalar_prefetch=3,
            grid=(n_steps,),
            in_specs=[pl.BlockSpec(memory_space=pl.ANY),
                      wspec((D_MODEL, EXPERT_FF)), wspec((D_MODEL, EXPERT_FF)), wspec((EXPERT_FF, D_MODEL))],
            out_specs=pl.BlockSpec(memory_space=pl.ANY),
            scratch_shapes=[pltpu.VMEM((2, 2, MOE_HALF * TOK_ROWS, LANES), F32),
                            pltpu.VMEM((2, 2, MOE_HALF * TOK_ROWS, LANES), F32),
                            pltpu.VMEM((D_MODEL, EXPERT_FF), BF16),
                            pltpu.VMEM((D_MODEL, EXPERT_FF), BF16),
                            pltpu.VMEM((EXPERT_FF, D_MODEL), BF16),
                            pltpu.SemaphoreType.DMA((2, 2)),
                            pltpu.SemaphoreType.DMA((2, 2))],
        ),
        out_shape=jax.ShapeDtypeStruct((n_out_rows * TOK_ROWS, LANES), F32),
        compiler_params=_cparams(("arbitrary",)),
        name="moe_experts",
    )(blk_e, meta, order, u, w1, w3, w2)


def _combine_kernel(y0_ref, y1_ref, h_ref, gate_ref, p_ref, g_ref, wg_ref, bg_ref, wp_ref, gf_ref,
                    o_ref, *, final):
    gate = gate_ref[...]
    tm = h_ref.shape[0]
    y0 = _load_token_tiles(y0_ref, tm)
    y1 = _load_token_tiles(y1_ref, tm)
    h = h_ref[...] + y0 * gate[:, 0:1] + y1 * gate[:, 1:2]
    u = _rms(h, g_ref[...])
    sg = jax.nn.sigmoid(jnp.dot(u.astype(BF16), wg_ref[...], preferred_element_type=F32) + bg_ref[...])
    pp = jnp.dot(p_ref[...].astype(BF16), wp_ref[...], preferred_element_type=F32)
    h = h + sg * pp
    if final:
        h = _rms(h, gf_ref[...])
    o_ref[...] = h


def _combine(yp, h, gate_t, p, g, wg, bg, wp, gf, tm, final):
    t = h.shape[0]
    nt = t // tm
    return pl.pallas_call(
        functools.partial(_combine_kernel, final=final),
        grid=(nt,),
        in_specs=[pl.BlockSpec((tm * TOK_ROWS, LANES), lambda i: (i, 0)),
                  pl.BlockSpec((tm * TOK_ROWS, LANES), lambda i: (nt + i, 0)),
                  pl.BlockSpec((tm, D_MODEL), lambda i: (i, 0)),
                  pl.BlockSpec((tm, TOP_K), lambda i: (i, 0)),
                  pl.BlockSpec((tm, PLE_DIM), lambda i: (i, 0)),
                  _full((1, D_MODEL)), _full((D_MODEL, D_MODEL)), _full((1, D_MODEL)),
                  _full((PLE_DIM, D_MODEL)), _full((1, D_MODEL))],
        out_specs=pl.BlockSpec((tm, D_MODEL), lambda i: (i, 0)),
        out_shape=jax.ShapeDtypeStruct((t, D_MODEL), F32),
        compiler_params=_cparams(("parallel",)),
        name="moe_combine_ple",
    )(yp, yp, h, gate_t, p, g, wg, bg, wp, gf)


def _rope_group(x, cos, sin, lane):
    half = QK_ROPE // 2
    rot = jnp.where(lane < ROPE_MID, -pltpu.roll(x, HEAD_PAD - half, 1), pltpu.roll(x, half, 1))
    return x * cos + rot * sin


def _kvq_kernel(h_ref, gkv_ref, gmix_ref, wdkv_ref, glat_ref, wkr_ref, wuk_ref, wuv_ref,
                wdq_ref, gq_ref, wuq_ref, cos_ref, sin_ref,
                lat_ref, kpe_ref, kcat_ref, v_ref, q_ref):
    h = h_ref[...]
    tm = h.shape[0]
    cos = cos_ref[...]
    sin = sin_ref[...]
    lane = lax.broadcasted_iota(jnp.int32, (tm, HEAD_PAD), 1)
    ukv = _rms(h, gkv_ref[...]).astype(BF16)
    c = _rms(jnp.dot(ukv, wdkv_ref[...], preferred_element_type=F32), glat_ref[...])
    lat_ref[...] = c
    kpe = _rope_group(jnp.dot(ukv, wkr_ref[...], preferred_element_type=F32), cos, sin, lane)
    kpe_ref[...] = kpe
    cb = c.astype(BF16)
    v_ref[...] = jnp.dot(cb, wuv_ref[...], preferred_element_type=F32).astype(BF16)
    kn = jnp.dot(cb, wuk_ref[...], preferred_element_type=F32)
    umix = _rms(h, gmix_ref[...]).astype(BF16)
    cq = _rms(jnp.dot(umix, wdq_ref[...], preferred_element_type=F32), gq_ref[...])
    q = jnp.dot(cq.astype(BF16), wuq_ref[...], preferred_element_type=F32)
    for hd in range(N_HEADS):
        lanes = slice(hd * HEAD_PAD, (hd + 1) * HEAD_PAD)
        kcat_ref[:, lanes] = (kn[:, lanes] + kpe).astype(BF16)
        q_ref[:, lanes] = (_rope_group(q[:, lanes], cos, sin, lane) * (SCALE * LOG2E)).astype(BF16)


def _kvq(h, gkv, gmix, wdkv, glat, wkr, wuk, wuv, wdq, gq, wuq, cos, sin, tm, n_pos_blocks):
    t = h.shape[0]
    row = lambda w: pl.BlockSpec((tm, w), lambda i: (i, 0))
    return pl.pallas_call(
        _kvq_kernel,
        grid=(t // tm,),
        in_specs=[row(D_MODEL), _full((1, D_MODEL)), _full((1, D_MODEL)),
                  _full((D_MODEL, KV_LORA)), _full((1, KV_LORA)), _full((D_MODEL, HEAD_PAD)),
                  _full((KV_LORA, N_HEADS * HEAD_PAD)), _full((KV_LORA, N_HEADS * V_HEAD)),
                  _full((D_MODEL, Q_LORA)), _full((1, Q_LORA)), _full((Q_LORA, N_HEADS * HEAD_PAD)),
                  pl.BlockSpec((tm, HEAD_PAD), lambda i: (i % n_pos_blocks, 0)),
                  pl.BlockSpec((tm, HEAD_PAD), lambda i: (i % n_pos_blocks, 0))],
        out_specs=[row(KV_LORA), row(HEAD_PAD), row(N_HEADS * HEAD_PAD), row(N_HEADS * V_HEAD),
                   row(N_HEADS * HEAD_PAD)],
        out_shape=[jax.ShapeDtypeStruct((t, KV_LORA), F32),
                   jax.ShapeDtypeStruct((t, HEAD_PAD), F32),
                   jax.ShapeDtypeStruct((t, N_HEADS * HEAD_PAD), BF16),
                   jax.ShapeDtypeStruct((t, N_HEADS * V_HEAD), BF16),
                   jax.ShapeDtypeStruct((t, N_HEADS * HEAD_PAD), BF16)],
        compiler_params=_cparams(("parallel",)),
        name="kvq_proj",
    )(h, gkv, gmix, wdkv, glat, wkr, wuk, wuv, wdq, gq, wuq, cos, sin)


ATT_TQ = 512
ATT_TK = 256


def _flash_kernel(q_ref, k_ref, v_ref, o_ref):
    s_len = q_ref.shape[0]
    tq, tk = ATT_TQ, ATT_TK
    row = lax.broadcasted_iota(jnp.int32, (tq, tk), 0)
    col = lax.broadcasted_iota(jnp.int32, (tq, tk), 1)
    lane = lax.broadcasted_iota(jnp.int32, (tq, 2 * V_HEAD), 1)
    nt = (((1,), (1,)), ((), ()))
    for qi in range(s_len // tq):
        q0 = qi * tq
        qs = [q_ref[q0:q0 + tq, hh * HEAD_PAD:(hh + 1) * HEAD_PAD] for hh in range(2)]
        m = [jnp.full((tq, 1), -jnp.inf, F32) for _ in range(2)]
        l = [jnp.zeros((tq, 1), F32) for _ in range(2)]
        acc = [jnp.zeros((tq, 2 * V_HEAD), F32) for _ in range(2)]
        for kj in range((q0 + tq) // tk):
            k0 = kj * tk
            v = v_ref[k0:k0 + tk, :]
            for hh in range(2):
                k = k_ref[k0:k0 + tk, hh * HEAD_PAD:(hh + 1) * HEAD_PAD]
                s = lax.dot_general(qs[hh], k, nt, preferred_element_type=F32)
                if k0 + tk - 1 > q0:
                    s = jnp.where(col + k0 <= row + q0, s, -jnp.inf)
                m_new = jnp.maximum(m[hh], jnp.max(s, axis=-1, keepdims=True))
                alpha = jnp.exp2(m[hh] - m_new)
                p = jnp.exp2(s - m_new)
                l[hh] = alpha * l[hh] + jnp.sum(p, axis=-1, keepdims=True)
                acc[hh] = alpha * acc[hh] + jnp.dot(p.astype(BF16), v, preferred_element_type=F32)
                m[hh] = m_new
        o = jnp.where(lane < V_HEAD, acc[0] / l[0], acc[1] / l[1])
        o_ref[q0:q0 + tq, :] = o.astype(BF16)


def _flash(q, kcat, v, b, s):
    return pl.pallas_call(
        _flash_kernel,
        grid=(b, N_HEADS // 2),
        in_specs=[pl.BlockSpec((s, 2 * HEAD_PAD), lambda i, h: (i, h)),
                  pl.BlockSpec((s, 2 * HEAD_PAD), lambda i, h: (i, h)),
                  pl.BlockSpec((s, 2 * V_HEAD), lambda i, h: (i, h))],
        out_specs=pl.BlockSpec((s, 2 * V_HEAD), lambda i, h: (i, h)),
        out_shape=jax.ShapeDtypeStruct((b * s, N_HEADS * V_HEAD), BF16),
        compiler_params=_cparams(("parallel", "parallel")),
        name="flash_attn",
    )(q, kcat, v)


def _qlat_kernel(q_ref, wukt_ref, o_ref):
    for hd in range(N_HEADS):
        qh = q_ref[:, hd * HEAD_PAD:(hd + 1) * HEAD_PAD]
        o_ref[:, hd * KV_LORA:(hd + 1) * KV_LORA] = jnp.dot(
            qh, wukt_ref[hd], preferred_element_type=F32).astype(BF16)


def _qlat(q, wukt):
    n = q.shape[0]
    return pl.pallas_call(
        _qlat_kernel,
        in_specs=[_full(q.shape), _full(wukt.shape)],
        out_specs=_full((n, N_HEADS * KV_LORA)),
        out_shape=jax.ShapeDtypeStruct((n, N_HEADS * KV_LORA), BF16),
        grid=(1,),
        compiler_params=_cparams(("arbitrary",)),
        name="q_absorb",
    )(q, wukt)


PAGES_PER_CHUNK = 32
CHUNK_SLOTS = 4
CHUNK_AHEAD = 2


def _paged_kernel(pt_ref, ql_ref, qp_ref, cn_ref, kn_ref, lat_hbm, kpt_hbm, o_ref, lbuf, kbuf, sems, *, nb):
    b = pl.program_id(0)
    n_pages = pt_ref.shape[0] // nb
    cp = PAGES_PER_CHUNK
    n_chunks = n_pages // cp
    assert n_chunks == CHUNK_SLOTS
    total = nb * n_chunks

    def start_chunk(g, slot):
        gg = jnp.where(g < total, g, 0)
        for j in range(cp):
            page = pt_ref[gg * cp + j]
            src = pl.ds(pl.multiple_of(page * PAGE_SIZE, PAGE_SIZE), PAGE_SIZE)
            pltpu.make_async_copy(lat_hbm.at[src], lbuf.at[slot, pl.ds(j * PAGE_SIZE, PAGE_SIZE)],
                                  sems.at[0, slot]).start()
            pltpu.make_async_copy(kpt_hbm.at[page], kbuf.at[slot, j], sems.at[1, slot]).start()

    def wait_chunk(g, slot):
        del g
        pltpu.make_async_copy(lat_hbm.at[pl.ds(0, cp * PAGE_SIZE)], lbuf.at[slot], sems.at[0, slot]).wait()
        pltpu.make_async_copy(kpt_hbm.at[pl.ds(0, cp)], kbuf.at[slot], sems.at[1, slot]).wait()

    @pl.when(b == 0)
    def _():
        for a in range(CHUNK_AHEAD):
            start_chunk(a, a)

    ql = ql_ref[...]
    qp = qp_ref[...]
    nt = (((1,), (1,)), ((), ()))
    m = jnp.full((N_HEADS, 1), -jnp.inf, F32)
    l = jnp.zeros((N_HEADS, 1), F32)
    acc = jnp.zeros((N_HEADS, KV_LORA), F32)
    rows = cp * PAGE_SIZE
    for c0 in range(0, n_chunks, CHUNK_AHEAD):
        g = b * n_chunks + c0
        group = range(c0, c0 + CHUNK_AHEAD)
        for a in range(CHUNK_AHEAD):
            start_chunk(g + CHUNK_AHEAD + a, (c0 + CHUNK_AHEAD + a) % CHUNK_SLOTS)
        for a in range(CHUNK_AHEAD):
            wait_chunk(g + a, c0 + a)
        kcs = [lbuf[c].astype(BF16) for c in group]
        kps = [jnp.concatenate([kbuf[c, jj] for jj in range(cp)], axis=1).astype(BF16)
               for c in group]
        s = jnp.concatenate([lax.dot_general(ql, kc, nt, preferred_element_type=F32)
                             + jnp.dot(qp, kp, preferred_element_type=F32)
                             for kc, kp in zip(kcs, kps)], axis=1)
        m_new = jnp.maximum(m, jnp.max(s, axis=-1, keepdims=True))
        alpha = jnp.exp2(m - m_new)
        p = jnp.exp2(s - m_new)
        l = alpha * l + jnp.sum(p, axis=-1, keepdims=True)
        acc = alpha * acc
        for a, kc in enumerate(kcs):
            acc = acc + jnp.dot(p[:, a * rows:(a + 1) * rows].astype(BF16), kc, preferred_element_type=F32)
        m = m_new
    cn = cn_ref[...].astype(BF16).astype(F32)
    kn = kn_ref[...].astype(BF16).astype(F32)
    s = (jnp.sum(ql.astype(F32) * cn, axis=-1, keepdims=True)
         + jnp.sum(qp.astype(F32) * kn, axis=-1, keepdims=True))
    m_new = jnp.maximum(m, s)
    alpha = jnp.exp2(m - m_new)
    p = jnp.exp2(s - m_new)
    l = alpha * l + p
    acc = alpha * acc + p.astype(BF16).astype(F32) * cn
    o_ref[...] = acc / l

    @pl.when(b == nb - 1)
    def _():
        for a in range(CHUNK_AHEAD):
            wait_chunk(total + a, a)


def _paged_attn(pt_flat, ql, qp, c_new, k_new, cache_latent, cache_kpt):
    n = ql.shape[0]
    rows = PAGES_PER_CHUNK * PAGE_SIZE
    per_seq = lambda w: pl.BlockSpec((None,) + w, lambda i, pt: (i, 0, 0))
    return pl.pallas_call(
        functools.partial(_paged_kernel, nb=n),
        grid_spec=pltpu.PrefetchScalarGridSpec(
            num_scalar_prefetch=1,
            grid=(n,),
            in_specs=[per_seq((N_HEADS, KV_LORA)), per_seq((N_HEADS, QK_ROPE)),
                      per_seq((1, KV_LORA)), per_seq((1, QK_ROPE)),
                      pl.BlockSpec(memory_space=pl.ANY), pl.BlockSpec(memory_space=pl.ANY)],
            out_specs=per_seq((N_HEADS, KV_LORA)),
            scratch_shapes=[pltpu.VMEM((CHUNK_SLOTS, rows, KV_LORA), F32),
                            pltpu.VMEM((CHUNK_SLOTS, PAGES_PER_CHUNK, QK_ROPE, PAGE_SIZE), F32),
                            pltpu.SemaphoreType.DMA((2, CHUNK_SLOTS))],
        ),
        out_shape=jax.ShapeDtypeStruct((n, N_HEADS, KV_LORA), F32),
        compiler_params=_cparams(("arbitrary",)),
        name="paged_attn",
    )(pt_flat, ql, qp, c_new, k_new, cache_latent, cache_kpt)


def _attn_out_kernel(ol_ref, wuv_ref, wo_ref, h_ref, o_ref):
    acc = h_ref[...]
    for hd in range(N_HEADS):
        oh = ol_ref[:, hd * KV_LORA:(hd + 1) * KV_LORA].astype(BF16)
        th = jnp.dot(oh, wuv_ref[hd], preferred_element_type=F32).astype(BF16)
        acc = acc + jnp.dot(th, wo_ref[hd * V_HEAD:(hd + 1) * V_HEAD, :], preferred_element_type=F32)
    o_ref[...] = acc


def _attn_out(ol, wuv_h, wo, h):
    n = h.shape[0]
    return pl.pallas_call(
        _attn_out_kernel,
        grid=(1,),
        in_specs=[_full(ol.shape), _full(wuv_h.shape), _full(wo.shape), _full(h.shape)],
        out_specs=_full(h.shape),
        out_shape=jax.ShapeDtypeStruct((n, D_MODEL), F32),
        compiler_params=_cparams(("arbitrary",)),
        name="attn_out",
    )(ol, wuv_h, wo, h)


def _rope_tables(pos):
    half = QK_ROPE // 2
    inv = ROPE_THETA ** (-jnp.arange(half, dtype=F32) / half)
    ang = pos.astype(F32)[:, None] * inv[None, :]
    cos, sin = jnp.cos(ang), jnp.sin(ang)
    n = pos.shape[0]
    cos_t = jnp.concatenate([jnp.ones((n, QK_NOPE), F32), cos, cos,
                             jnp.ones((n, HEAD_PAD - ROPE_HI), F32)], axis=1)
    sin_t = jnp.concatenate([jnp.zeros((n, QK_NOPE), F32), sin, sin,
                             jnp.zeros((n, HEAD_PAD - ROPE_HI), F32)], axis=1)
    return cos_t, sin_t


def _moe(h, g_ffn, wr, br, w1, w3, w2, layer, tm, proj=None):
    t = h.shape[0]
    h, u, code, gate, cnt = _router(h, g_ffn, wr, br, tm, proj)
    counts = cnt[:, 0].astype(jnp.int32)
    padded = (counts + MOE_BM - 1) // MOE_BM * MOE_BM
    pends = jnp.cumsum(padded)
    pstarts = pends - padded
    n_steps = (t * TOP_K + N_EXPERTS * (MOE_BM - 1) + MOE_BM - 1) // MOE_BM
    n_rows = n_steps * MOE_BM
    eid = code >> RANK_BITS
    eiota = jnp.arange(N_EXPERTS, dtype=jnp.int32)
    start_of = jnp.sum(jnp.where(eid[..., None] == eiota, pstarts, 0), axis=-1)
    dest = (start_of + (code & ((1 << RANK_BITS) - 1))).reshape(-1)
    block_start = jnp.arange(n_steps, dtype=jnp.int32) * MOE_BM
    blk_e = jnp.minimum(jnp.sum((pends[None, :] <= block_start[:, None]).astype(jnp.int32), axis=1),
                        N_EXPERTS - 1)
    order = jnp.argsort(dest).astype(jnp.int32)
    starts = jnp.cumsum(counts) - counts
    per_expert = jnp.stack([pstarts - starts, pstarts + counts, starts + counts])
    meta = jnp.sum(jnp.where(blk_e[None, :, None] == eiota, per_expert[:, None, :], 0), axis=-1)
    n_out = n_rows + 2 * MOE_BM
    assert t & (t - 1) == 0
    yp = _experts(blk_e, meta.reshape(-1).astype(jnp.int32), order, u, w1, w3, w2, layer, n_steps, n_out)
    return h, yp, gate.T


def kernel(x_prompt, x_sample, cache_conv, cache_latent, cache_kpe, page_table, p_prompt, p_sample, g_mix, g_ffn, g_ple, g_final, conv_w_in, conv_b_in, conv_w_dw, conv_b_dw, conv_ln_g, conv_ln_b, conv_w_out, conv_b_out, g_kv_in, w_dkv, g_kv_lat, w_kr, w_uk, w_uv, w_dq, g_q_lat, w_uq, w_o, w_group, b_group, w_inner, b_inner, w_exp_gate, w_exp_up, w_exp_down, w_ple_gate, b_ple_gate, w_ple_proj):
    bp, sp, d = x_prompt.shape
    nd = x_sample.shape[0]
    tp = bp * sp
    row = lambda a: a.reshape(1, -1)

    w_in_b = conv_w_in[0].astype(BF16)
    w_out_b = conv_w_out[0].astype(BF16)
    wdw_pad = jnp.concatenate([conv_w_dw[0], jnp.zeros((HALO - CONV_W, d), F32)], axis=0)
    wr, br = [], []
    for i in range(2):
        tail = ROUTER_ROWS - 8 - N_EXPERTS
        wr.append(jnp.concatenate([w_group[i].T, jnp.zeros((8 - N_GROUPS, d), F32),
                                   w_inner[i].reshape(d, N_EXPERTS).T, jnp.zeros((tail, d), F32)], axis=0))
        br.append(jnp.concatenate([b_group[i], jnp.zeros((8 - N_GROUPS,), F32),
                                   b_inner[i].reshape(N_EXPERTS), jnp.zeros((tail,), F32)]).reshape(ROUTER_ROWS, 1))
    w_dkv_b = w_dkv.astype(BF16)
    w_kr_pad = jnp.zeros((d, HEAD_PAD), F32).at[:, ROPE_LO:ROPE_HI].set(w_kr).astype(BF16)
    w_uk_pad = jnp.pad(w_uk, ((0, 0), (0, 0), (0, HEAD_PAD - QK_NOPE))).reshape(KV_LORA, -1).astype(BF16)
    w_uv_b = w_uv.reshape(KV_LORA, -1).astype(BF16)
    w_dq_b = w_dq[0].astype(BF16)
    w_uq_pad = jnp.pad(w_uq[0].reshape(Q_LORA, N_HEADS, QK_NOPE + QK_ROPE),
                       ((0, 0), (0, 0), (0, HEAD_PAD - ROPE_HI))).reshape(Q_LORA, -1).astype(BF16)
    w_o_b = w_o[0].astype(BF16)
    w_ukt_pad = jnp.pad(jnp.transpose(w_uk, (1, 2, 0)),
                        ((0, 0), (0, HEAD_PAD - QK_NOPE), (0, 0))).astype(BF16)
    w_uv_h = jnp.transpose(w_uv, (1, 0, 2)).astype(BF16)
    wg_b = [w_ple_gate[i].astype(BF16) for i in range(2)]
    wp_b = [w_ple_proj[i].astype(BF16) for i in range(2)]

    def moe_ple(h, layer, p, tm, final, proj=None):
        h, yp, gate_t = _moe(h, row(g_ffn[layer]), wr[layer], br[layer],
                             w_exp_gate, w_exp_up, w_exp_down, layer, tm, proj)
        return _combine(yp, h, gate_t, p, row(g_ple[layer]), wg_b[layer], row(b_ple_gate[layer]),
                        wp_b[layer], row(g_final), tm, final)

    def kvq(h, cos_t, sin_t, tm, n_pos_blocks):
        return _kvq(h, row(g_kv_in), row(g_mix[1]), w_dkv_b, row(g_kv_lat), w_kr_pad, w_uk_pad, w_uv_b,
                    w_dq_b, row(g_q_lat[0]), w_uq_pad, cos_t, sin_t, tm, n_pos_blocks)

    xp = x_prompt.reshape(tp, d)
    z = _conv_in(xp, row(g_mix[0]), w_in_b, row(conv_b_in[0]), 512)
    z3 = z.reshape(bp, sp, d)
    conv_prompt = z3[:, sp - (CONV_W - 1):, :][None]
    ctx0 = jnp.zeros((bp, HALO, d), F32)
    h = _conv_mix(z3, ctx0, x_prompt, wdw_pad, row(conv_b_dw[0]), row(conv_ln_g[0]), row(conv_ln_b[0]),
                  w_out_b, row(conv_b_out[0])).reshape(tp, d)
    h = moe_ple(h, 0, p_prompt[0].reshape(tp, PLE_DIM), 512, False)
    cos_p, sin_p = _rope_tables(jnp.arange(sp, dtype=jnp.int32))
    lat_p, kpe_p, kcat, v, q = kvq(h, cos_p, sin_p, 512, sp // 512)
    o = _flash(q, kcat, v, bp, sp)
    y_prompt = moe_ple(h, 1, p_prompt[1].reshape(tp, PLE_DIM), 512, True, proj=(o, w_o_b)).reshape(bp, sp, d)
    latent_prompt = lat_p.reshape(bp, sp, KV_LORA)
    kpe_prompt = kpe_p[:, ROPE_LO:ROPE_HI].reshape(bp, sp, QK_ROPE)

    xs = x_sample.reshape(nd, d)
    zs = _conv_in(xs, row(g_mix[0]), w_in_b, row(conv_b_in[0]), nd)
    ctx_t = jnp.transpose(cache_conv[0], (1, 0, 2))
    conv_sample = jnp.transpose(jnp.concatenate([ctx_t[1:], zs[None]], axis=0), (1, 0, 2))[None]
    hs = _conv_step(ctx_t, conv_w_dw[0][:, None, :], zs, xs,
                    row(conv_b_dw[0]), row(conv_ln_g[0]), row(conv_ln_b[0]), w_out_b, row(conv_b_out[0]))
    hs = moe_ple(hs, 0, p_sample[0].reshape(nd, PLE_DIM), nd, False)
    n_pages = page_table.shape[1]
    pos_s = jnp.full((nd,), n_pages * PAGE_SIZE, jnp.int32)
    cos_s, sin_s = _rope_tables(pos_s)
    lat_s, kpe_s, _, _, qs = kvq(hs, cos_s, sin_s, nd, 1)
    ql = _qlat(qs, w_ukt_pad).reshape(nd, N_HEADS, KV_LORA)
    qp = qs.reshape(nd, N_HEADS, HEAD_PAD)[:, :, ROPE_LO:ROPE_HI]
    kpe_new = kpe_s[:, ROPE_LO:ROPE_HI]
    o_lat = _paged_attn(page_table.reshape(-1), ql, qp, lat_s.reshape(nd, 1, KV_LORA),
                        kpe_new.reshape(nd, 1, QK_ROPE), cache_latent.reshape(-1, KV_LORA),
                        jnp.transpose(cache_kpe, (0, 2, 1)))
    hs = _attn_out(o_lat.reshape(nd, N_HEADS * KV_LORA), w_uv_h, w_o_b, hs)
    y_sample = moe_ple(hs, 1, p_sample[1].reshape(nd, PLE_DIM), nd, True).reshape(nd, 1, d)
    latent_sample = lat_s.reshape(nd, 1, KV_LORA)
    kpe_sample = kpe_new.reshape(nd, 1, QK_ROPE)

    return (y_prompt, y_sample, conv_prompt, conv_sample, latent_prompt, kpe_prompt, latent_sample, kpe_sample)
```

```python
import functools
import math

import jax
import jax.numpy as jnp
from jax import lax
from jax.experimental import pallas as pl
from jax.experimental.pallas import tpu as pltpu

F32 = jnp.float32
BF16 = jnp.bfloat16

D_MODEL = 1024
PLE_DIM = 256
CONV_W = 31
N_HEADS = 16
QK_NOPE = 64
QK_ROPE = 32
V_HEAD = 64
Q_LORA = 384
KV_LORA = 256
ROPE_THETA = 10000.0
SCALE = 1.0 / math.sqrt(QK_NOPE + QK_ROPE)
LOG2E = 1.4426950408889634
N_GROUPS = 4
EXPERTS_PER_GROUP = 8
N_EXPERTS = N_GROUPS * EXPERTS_PER_GROUP
TOP_K = 2
EXPERT_FF = 512
PAGE_SIZE = 128
EPS = 1e-6

HEAD_PAD = 128
ROPE_LO = QK_NOPE
ROPE_MID = QK_NOPE + QK_ROPE // 2
ROPE_HI = QK_NOPE + QK_ROPE
HALO = 32
MOE_HALF = 128
MOE_BM = 2 * MOE_HALF
ROUTER_ROWS = 48
RANK_BITS = 20
VMEM_LIMIT = 56 * 1024 * 1024


def _cparams(sem):
    return pltpu.CompilerParams(dimension_semantics=sem, vmem_limit_bytes=VMEM_LIMIT)


def _rms(x, g):
    return x * lax.rsqrt(jnp.mean(x * x, axis=-1, keepdims=True) + EPS) * g


def _full(shape):
    n = len(shape)
    return pl.BlockSpec(shape, lambda *_: (0,) * n)


LANES = 128
SUBLANES = 8
TOK_ROWS = D_MODEL // LANES


def _load_token_tiles(ref, n):
    return jnp.concatenate([ref[pl.ds(s, n, stride=TOK_ROWS), :] for s in range(TOK_ROWS)], axis=1)


def _store_token_tiles(ref, x):
    n = x.shape[0]
    for s in range(TOK_ROWS):
        ref[pl.ds(s, n, stride=TOK_ROWS), :] = x[:, s * LANES:(s + 1) * LANES]


def _conv_in_kernel(x_ref, g_ref, w_ref, b_ref, z_ref):
    u = _rms(x_ref[...], g_ref[...])
    a = jnp.dot(u.astype(BF16), w_ref[...], preferred_element_type=F32) + b_ref[...]
    z_ref[...] = a[:, :D_MODEL] * jax.nn.sigmoid(a[:, D_MODEL:])


def _conv_in(x, g, w_in, b_in, tm):
    t = x.shape[0]
    return pl.pallas_call(
        _conv_in_kernel,
        grid=(t // tm,),
        in_specs=[pl.BlockSpec((tm, D_MODEL), lambda i: (i, 0)),
                  _full((1, D_MODEL)), _full((D_MODEL, 2 * D_MODEL)), _full((1, 2 * D_MODEL))],
        out_specs=pl.BlockSpec((tm, D_MODEL), lambda i: (i, 0)),
        out_shape=jax.ShapeDtypeStruct((t, D_MODEL), F32),
        compiler_params=_cparams(("parallel",)),
        name="conv_in",
    )(x, g, w_in, b_in)


def _ln_silu_out(y, lng, lnb, wout, bout):
    mu = jnp.mean(y, axis=-1, keepdims=True)
    yc = y - mu
    var = jnp.mean(yc * yc, axis=-1, keepdims=True)
    n = yc * lax.rsqrt(var + EPS) * lng + lnb
    act = n * jax.nn.sigmoid(n)
    return jnp.dot(act.astype(BF16), wout, preferred_element_type=F32) + bout


CONV_TS = 256
CONV_RC = 128
CONV_LC = 256


def _conv_mix_kernel(z_ref, halo_ref, ctx_ref, x_ref, wdw_ref, bdw_ref, lng_ref, lnb_ref,
                     wout_ref, bout_ref, h_ref, buf_ref, y_ref):
    s = pl.program_id(1)

    @pl.when(s == 0)
    def _():
        buf_ref[0, 0:HALO, :] = ctx_ref[...]

    @pl.when(s > 0)
    def _():
        buf_ref[0, 0:HALO, :] = halo_ref[...]

    buf_ref[0, HALO:, :] = z_ref[...]
    shifted_rows = CONV_TS + HALO - SUBLANES
    for sh in range(1, SUBLANES):
        buf_ref[sh, 0:shifted_rows, :] = buf_ref[0, sh:sh + shifted_rows, :]
    first = HALO - (CONV_W - 1)
    for lc in range(D_MODEL // CONV_LC):
        lanes = slice(lc * CONV_LC, (lc + 1) * CONV_LC)
        for rc in range(CONV_TS // CONV_RC):
            r0 = rc * CONV_RC
            acc = jnp.broadcast_to(bdw_ref[:, lanes], (CONV_RC, CONV_LC))
            for k in range(CONV_W):
                sh = (first + k) % SUBLANES
                a0 = r0 + first + k - sh
                acc = acc + wdw_ref[k:k + 1, lanes] * buf_ref[sh, a0:a0 + CONV_RC, lanes]
            y_ref[r0:r0 + CONV_RC, lanes] = acc
    out = _ln_silu_out(y_ref[...], lng_ref[...], lnb_ref[...], wout_ref[...], bout_ref[...])
    h_ref[...] = x_ref[...] + out


def _conv_mix(z3, ctx, x3, wdw, bdw, lng, lnb, wout, bout):
    b, s, d = z3.shape
    ts = CONV_TS
    hb = ts // HALO
    return pl.pallas_call(
        _conv_mix_kernel,
        grid=(b, s // ts),
        in_specs=[pl.BlockSpec((None, ts, d), lambda i, j: (i, j, 0)),
                  pl.BlockSpec((None, HALO, d), lambda i, j: (i, jnp.maximum(j * hb - 1, 0), 0)),
                  pl.BlockSpec((None, HALO, d), lambda i, j: (i, 0, 0)),
                  pl.BlockSpec((None, ts, d), lambda i, j: (i, j, 0)),
                  _full((HALO, d)), _full((1, d)), _full((1, d)), _full((1, d)),
                  _full((d, d)), _full((1, d))],
        out_specs=pl.BlockSpec((None, ts, d), lambda i, j: (i, j, 0)),
        out_shape=jax.ShapeDtypeStruct((b, s, d), F32),
        scratch_shapes=[pltpu.VMEM((SUBLANES, ts + HALO, d), F32), pltpu.VMEM((ts, d), F32)],
        compiler_params=_cparams(("parallel", "arbitrary")),
        name="conv_mix",
    )(z3, z3, ctx, x3, wdw, bdw, lng, lnb, wout, bout)


def _conv_step_kernel(ctx_ref, w_ref, z_ref, x_ref, bdw_ref, lng_ref, lnb_ref, wout_ref, bout_ref,
                      h_ref, acc_ref):
    k = pl.program_id(0)

    @pl.when(k == 0)
    def _():
        acc_ref[...] = jnp.broadcast_to(bdw_ref[...], acc_ref.shape)

    @pl.when(k < CONV_W - 1)
    def _():
        acc_ref[...] += w_ref[...] * ctx_ref[...]

    @pl.when(k == CONV_W - 1)
    def _():
        y = acc_ref[...] + w_ref[...] * z_ref[...]
        out = _ln_silu_out(y, lng_ref[...], lnb_ref[...], wout_ref[...], bout_ref[...])
        h_ref[...] = x_ref[...] + out


def _conv_step(ctx_t, wdw3, z, x, bdw, lng, lnb, wout, bout):
    n, d = z.shape
    last = CONV_W - 2
    return pl.pallas_call(
        _conv_step_kernel,
        grid=(CONV_W,),
        in_specs=[pl.BlockSpec((None, n, d), lambda k: (jnp.minimum(k, last), 0, 0)),
                  pl.BlockSpec((None, 1, d), lambda k: (k, 0, 0)),
                  _full((n, d)), _full((n, d)), _full((1, d)), _full((1, d)), _full((1, d)),
                  _full((d, d)), _full((1, d))],
        out_specs=_full((n, d)),
        out_shape=jax.ShapeDtypeStruct((n, d), F32),
        scratch_shapes=[pltpu.VMEM((n, d), F32)],
        compiler_params=_cparams(("arbitrary",)),
        name="conv_step",
    )(ctx_t, wdw3, z, x, bdw, lng, lnb, wout, bout)


def _router_kernel(*refs, with_proj):
    if with_proj:
        (h_ref, g_ref, wr_ref, br_ref, before_ref, a_ref, w_ref,
         u_ref, code_ref, gate_ref, cnt_ref, hout_ref, carry_ref) = refs
    else:
        h_ref, g_ref, wr_ref, br_ref, before_ref, u_ref, code_ref, gate_ref, cnt_ref, carry_ref = refs
    i = pl.program_id(0)

    @pl.when(i == 0)
    def _():
        carry_ref[...] = jnp.zeros_like(carry_ref)

    h = h_ref[...]
    if with_proj:
        h = h + jnp.dot(a_ref[...], w_ref[...], preferred_element_type=F32)
        hout_ref[...] = h
    u = _rms(h, g_ref[...])
    _store_token_tiles(u_ref, u)
    nt = (((1,), (1,)), ((), ()))
    u_hi = u.astype(BF16)
    u_lo = (u - u_hi.astype(F32)).astype(BF16)
    part = lax.dot_general(wr_ref[...], u_hi, nt, preferred_element_type=F32)
    logits = (part[0:ROUTER_ROWS] + part[ROUTER_ROWS:]
              + lax.dot_general(wr_ref[0:ROUTER_ROWS, :], u_lo, nt, preferred_element_type=F32) + br_ref[...])
    gl = logits[0:N_GROUPS]
    il = logits[8:8 + N_EXPERTS]
    gmax = jnp.max(gl, axis=0, keepdims=True)
    giota = lax.broadcasted_iota(jnp.int32, gl.shape, 0)
    gsel = jnp.min(jnp.where(gl == gmax, giota, N_GROUPS), axis=0, keepdims=True)
    gw = 1.0 / jnp.sum(jnp.exp(gl - gmax), axis=0, keepdims=True)
    eiota = lax.broadcasted_iota(jnp.int32, il.shape, 0)
    ml = jnp.where((eiota >> 3) == gsel, il, -jnp.inf)
    v1 = jnp.max(ml, axis=0, keepdims=True)
    i1 = jnp.min(jnp.where(ml == v1, eiota, N_EXPERTS), axis=0, keepdims=True)
    ml2 = jnp.where(eiota == i1, -jnp.inf, ml)
    v2 = jnp.max(ml2, axis=0, keepdims=True)
    i2 = jnp.min(jnp.where(ml2 == v2, eiota, N_EXPERTS), axis=0, keepdims=True)
    e2 = jnp.exp(v2 - v1)
    p1 = 1.0 / (1.0 + e2)
    gate_ref[0:1, :] = p1 * gw
    gate_ref[1:2, :] = e2 * p1 * gw
    hit1 = eiota == i1
    hit2 = eiota == i2
    onehot = jnp.where(hit1 | hit2, 1.0, 0.0)
    cum = jnp.dot(onehot.astype(BF16), before_ref[...], preferred_element_type=F32) + carry_ref[:, 0:1]
    rank1 = jnp.sum(jnp.where(hit1, cum, 0.0), axis=0, keepdims=True).astype(jnp.int32)
    rank2 = jnp.sum(jnp.where(hit2, cum, 0.0), axis=0, keepdims=True).astype(jnp.int32)
    code_ref[0:1, :] = (i1 << RANK_BITS) | rank1
    code_ref[1:2, :] = (i2 << RANK_BITS) | rank2
    carry_ref[...] += jnp.sum(onehot, axis=1, keepdims=True)
    cnt_ref[...] = carry_ref[...]


def _router(h, g, wr, br, tm, proj=None):
    t = h.shape[0]
    row2 = pl.BlockSpec((TOP_K, tm), lambda i: (0, i))
    rows = pl.BlockSpec((tm, D_MODEL), lambda i: (i, 0))
    before = (jnp.arange(tm)[:, None] < jnp.arange(tm)[None, :]).astype(BF16)
    wr_hi = wr.astype(BF16)
    wr_split = jnp.concatenate([wr_hi, (wr - wr_hi.astype(F32)).astype(BF16)], axis=0)
    in_specs = [rows, _full((1, D_MODEL)), _full((2 * ROUTER_ROWS, D_MODEL)), _full((ROUTER_ROWS, 1)),
                _full((tm, tm))]
    out_specs = [pl.BlockSpec((tm * TOK_ROWS, LANES), lambda i: (i, 0)), row2, row2, _full((N_EXPERTS, 128))]
    out_shape = [jax.ShapeDtypeStruct((t * TOK_ROWS, LANES), F32),
                 jax.ShapeDtypeStruct((TOP_K, t), jnp.int32),
                 jax.ShapeDtypeStruct((TOP_K, t), F32),
                 jax.ShapeDtypeStruct((N_EXPERTS, 128), F32)]
    args = [h, g, wr_split, br, before]
    if proj is not None:
        a, w = proj
        in_specs += [pl.BlockSpec((tm, a.shape[1]), lambda i: (i, 0)), _full(w.shape)]
        out_specs.append(rows)
        out_shape.append(jax.ShapeDtypeStruct((t, D_MODEL), F32))
        args += [a, w]
    outs = pl.pallas_call(
        functools.partial(_router_kernel, with_proj=proj is not None),
        grid=(t // tm,),
        in_specs=in_specs,
        out_specs=out_specs,
        out_shape=out_shape,
        scratch_shapes=[pltpu.VMEM((N_EXPERTS, 128), F32)],
        compiler_params=_cparams(("arbitrary",)),
        name="router",
    )(*args)
    if proj is not None:
        return (outs[4],) + tuple(outs[:4])
    return (h,) + tuple(outs)


def _expert_kernel(blk_ref, meta_ref, order_ref, u_hbm, w1_ref, w3_ref, w2_ref, yp_hbm, xbuf, ybuf,
                   w1b, w3b, w2b, gsem, ssem, *, n_tok, n_steps):
    j = pl.program_id(0)
    prime0 = n_steps * MOE_BM
    half_rows = MOE_HALF * TOK_ROWS

    def tile_of(t):
        return pl.ds(pl.multiple_of(t * TOK_ROWS, TOK_ROWS), TOK_ROWS)

    def sorted_rows(step, half):
        base = step * MOE_BM + half * MOE_HALF
        pos0 = base - meta_ref[step]
        n_pairs = meta_ref[n_steps + step] - base
        for r in range(MOE_HALF):
            yield r, r < n_pairs, order_ref[pos0 + r]

    def gather_start(step, half, par):
        base = step * MOE_BM + half * MOE_HALF
        for r, valid, pair in sorted_rows(step, half):
            tok = jnp.where(valid, pair, base + r) & (n_tok - 1)
            pltpu.make_async_copy(u_hbm.at[tile_of(tok)], xbuf.at[par, half, tile_of(r)],
                                  gsem.at[par, half]).start()

    def gather_wait(half, par):
        pltpu.make_async_copy(u_hbm.at[pl.ds(0, half_rows)], xbuf.at[par, half], gsem.at[par, half]).wait()

    def prime_rows(half, par):
        return prime0 + (2 * par + half) * MOE_HALF

    def scatter_start(step, half, par, to_prime=None):
        trash0 = TOP_K * n_tok + step * MOE_BM + half * MOE_HALF - meta_ref[2 * n_steps + step]
        for r, valid, pair in sorted_rows(step, half):
            dst = jnp.where(valid, pair, trash0 + r)
            if to_prime is not None:
                dst = jnp.where(to_prime, prime_rows(half, par) + r, dst)
            pltpu.make_async_copy(ybuf.at[par, half, tile_of(r)], yp_hbm.at[tile_of(dst)],
                                  ssem.at[par, half]).start()

    def scatter_wait(half, par):
        pltpu.make_async_copy(ybuf.at[par, half], yp_hbm.at[pl.ds(0, half_rows)], ssem.at[par, half]).wait()

    @pl.when(j == 0)
    def _():
        ybuf[...] = jnp.zeros_like(ybuf)
        for par, half in ((0, 0), (0, 1), (1, 0)):
            pltpu.make_async_copy(ybuf.at[par, half],
                                  yp_hbm.at[pl.ds(prime_rows(half, par) * TOK_ROWS, half_rows)],
                                  ssem.at[par, half]).start()
        gather_start(0, 0, 0)
        gather_start(0, 1, 0)

    e = blk_ref[j]
    prev = blk_ref[jnp.maximum(j - 1, 0)]

    @pl.when((j == 0) | (e != prev))
    def _():
        w1b[...] = w1_ref[...].astype(BF16)
        w3b[...] = w3_ref[...].astype(BF16)
        w2b[...] = w2_ref[...].astype(BF16)

    def compute(half, par):
        x = _load_token_tiles(xbuf.at[par, half], MOE_HALF).astype(BF16)
        a = jnp.dot(x, w1b[...], preferred_element_type=F32)
        b = jnp.dot(x, w3b[...], preferred_element_type=F32)
        hmid = (a * jax.nn.sigmoid(a)) * b
        _store_token_tiles(ybuf.at[par, half],
                           jnp.dot(hmid.astype(BF16), w2b[...], preferred_element_type=F32))

    nxt = jnp.minimum(j + 1, n_steps - 1)

    def step_body(par):
        gather_wait(0, par)
        scatter_wait(0, par)
        gather_start(nxt, 0, 1 - par)
        scatter_start(jnp.maximum(j - 1, 0), 1, 1 - par, to_prime=(j == 0))
        compute(0, par)
        gather_wait(1, par)
        scatter_wait(1, par)
        gather_start(nxt, 1, 1 - par)
        scatter_start(j, 0, par)
        compute(1, par)

    for par in range(2):
        pl.when(j % 2 == par)(functools.partial(step_body, par))

    @pl.when(j == n_steps - 1)
    def _():
        last = (n_steps - 1) % 2
        scatter_start(j, 1, last)
        for half in range(2):
            gather_wait(half, 1 - last)
            scatter_wait(half, 1 - last)
            scatter_wait(half, last)


def _experts(blk_e, meta, order, u, w1, w3, w2, layer, n_steps, n_out_rows):
    wspec = lambda s: pl.BlockSpec((None, None) + s, lambda j, b, m, o: (layer, b[j], 0, 0))
    return pl.pallas_call(
        functools.partial(_expert_kernel, n_tok=u.shape[0] // TOK_ROWS, n_steps=n_steps),
        grid_spec=pltpu.PrefetchScalarGridSpec(
            num_scalar_prefetch=3,
            grid=(n_steps,),
            in_specs=[pl.BlockSpec(memory_space=pl.ANY),
                      wspec((D_MODEL, EXPERT_FF)), wspec((D_MODEL, EXPERT_FF)), wspec((EXPERT_FF, D_MODEL))],
            out_specs=pl.BlockSpec(memory_space=pl.ANY),
            scratch_shapes=[pltpu.VMEM((2, 2, MOE_HALF * TOK_ROWS, LANES), F32),
                            pltpu.VMEM((2, 2, MOE_HALF * TOK_ROWS, LANES), F32),
                            pltpu.VMEM((D_MODEL, EXPERT_FF), BF16),
                            pltpu.VMEM((D_MODEL, EXPERT_FF), BF16),
                            pltpu.VMEM((EXPERT_FF, D_MODEL), BF16),
                            pltpu.SemaphoreType.DMA((2, 2)),
                            pltpu.SemaphoreType.DMA((2, 2))],
        ),
        out_shape=jax.ShapeDtypeStruct((n_out_rows * TOK_ROWS, LANES), F32),
        compiler_params=_cparams(("arbitrary",)),
        name="moe_experts",
    )(blk_e, meta, order, u, w1, w3, w2)


def _combine_kernel(y0_ref, y1_ref, h_ref, gate_ref, p_ref, g_ref, wg_ref, bg_ref, wp_ref, gf_ref,
                    o_ref, *, final):
    gate = gate_ref[...]
    tm = h_ref.shape[0]
    y0 = _load_token_tiles(y0_ref, tm)
    y1 = _load_token_tiles(y1_ref, tm)
    h = h_ref[...] + y0 * gate[:, 0:1] + y1 * gate[:, 1:2]
    u = _rms(h, g_ref[...])
    sg = jax.nn.sigmoid(jnp.dot(u.astype(BF16), wg_ref[...], preferred_element_type=F32) + bg_ref[...])
    pp = jnp.dot(p_ref[...].astype(BF16), wp_ref[...], preferred_element_type=F32)
    h = h + sg * pp
    if final:
        h = _rms(h, gf_ref[...])
    o_ref[...] = h


def _combine(yp, h, gate_t, p, g, wg, bg, wp, gf, tm, final):
    t = h.shape[0]
    nt = t // tm
    return pl.pallas_call(
        functools.partial(_combine_kernel, final=final),
        grid=(nt,),
        in_specs=[pl.BlockSpec((tm * TOK_ROWS, LANES), lambda i: (i, 0)),
                  pl.BlockSpec((tm * TOK_ROWS, LANES), lambda i: (nt + i, 0)),
                  pl.BlockSpec((tm, D_MODEL), lambda i: (i, 0)),
                  pl.BlockSpec((tm, TOP_K), lambda i: (i, 0)),
                  pl.BlockSpec((tm, PLE_DIM), lambda i: (i, 0)),
                  _full((1, D_MODEL)), _full((D_MODEL, D_MODEL)), _full((1, D_MODEL)),
                  _full((PLE_DIM, D_MODEL)), _full((1, D_MODEL))],
        out_specs=pl.BlockSpec((tm, D_MODEL), lambda i: (i, 0)),
        out_shape=jax.ShapeDtypeStruct((t, D_MODEL), F32),
        compiler_params=_cparams(("parallel",)),
        name="moe_combine_ple",
    )(yp, yp, h, gate_t, p, g, wg, bg, wp, gf)


def _rope_group(x, cos, sin, lane):
    half = QK_ROPE // 2
    rot = jnp.where(lane < ROPE_MID, -pltpu.roll(x, HEAD_PAD - half, 1), pltpu.roll(x, half, 1))
    return x * cos + rot * sin


def _kvq_kernel(h_ref, gkv_ref, gmix_ref, wdkv_ref, glat_ref, wkr_ref, wuk_ref, wuv_ref,
                wdq_ref, gq_ref, wuq_ref, cos_ref, sin_ref,
                lat_ref, kpe_ref, kcat_ref, v_ref, q_ref):
    h = h_ref[...]
    tm = h.shape[0]
    cos = cos_ref[...]
    sin = sin_ref[...]
    lane = lax.broadcasted_iota(jnp.int32, (tm, HEAD_PAD), 1)
    ukv = _rms(h, gkv_ref[...]).astype(BF16)
    c = _rms(jnp.dot(ukv, wdkv_ref[...], preferred_element_type=F32), glat_ref[...])
    lat_ref[...] = c
    kpe = _rope_group(jnp.dot(ukv, wkr_ref[...], preferred_element_type=F32), cos, sin, lane)
    kpe_ref[...] = kpe
    cb = c.astype(BF16)
    v_ref[...] = jnp.dot(cb, wuv_ref[...], preferred_element_type=F32).astype(BF16)
    kn = jnp.dot(cb, wuk_ref[...], preferred_element_type=F32)
    umix = _rms(h, gmix_ref[...]).astype(BF16)
    cq = _rms(jnp.dot(umix, wdq_ref[...], preferred_element_type=F32), gq_ref[...])
    q = jnp.dot(cq.astype(BF16), wuq_ref[...], preferred_element_type=F32)
    for hd in range(N_HEADS):
        lanes = slice(hd * HEAD_PAD, (hd + 1) * HEAD_PAD)
        kcat_ref[:, lanes] = (kn[:, lanes] + kpe).astype(BF16)
        q_ref[:, lanes] = (_rope_group(q[:, lanes], cos, sin, lane) * (SCALE * LOG2E)).astype(BF16)


def _kvq(h, gkv, gmix, wdkv, glat, wkr, wuk, wuv, wdq, gq, wuq, cos, sin, tm, n_pos_blocks):
    t = h.shape[0]
    row = lambda w: pl.BlockSpec((tm, w), lambda i: (i, 0))
    return pl.pallas_call(
        _kvq_kernel,
        grid=(t // tm,),
        in_specs=[row(D_MODEL), _full((1, D_MODEL)), _full((1, D_MODEL)),
                  _full((D_MODEL, KV_LORA)), _full((1, KV_LORA)), _full((D_MODEL, HEAD_PAD)),
                  _full((KV_LORA, N_HEADS * HEAD_PAD)), _full((KV_LORA, N_HEADS * V_HEAD)),
                  _full((D_MODEL, Q_LORA)), _full((1, Q_LORA)), _full((Q_LORA, N_HEADS * HEAD_PAD)),
                  pl.BlockSpec((tm, HEAD_PAD), lambda i: (i % n_pos_blocks, 0)),
                  pl.BlockSpec((tm, HEAD_PAD), lambda i: (i % n_pos_blocks, 0))],
        out_specs=[row(KV_LORA), row(HEAD_PAD), row(N_HEADS * HEAD_PAD), row(N_HEADS * V_HEAD),
                   row(N_HEADS * HEAD_PAD)],
        out_shape=[jax.ShapeDtypeStruct((t, KV_LORA), F32),
                   jax.ShapeDtypeStruct((t, HEAD_PAD), F32),
                   jax.ShapeDtypeStruct((t, N_HEADS * HEAD_PAD), BF16),
                   jax.ShapeDtypeStruct((t, N_HEADS * V_HEAD), BF16),
                   jax.ShapeDtypeStruct((t, N_HEADS * HEAD_PAD), BF16)],
        compiler_params=_cparams(("parallel",)),
        name="kvq_proj",
    )(h, gkv, gmix, wdkv, glat, wkr, wuk, wuv, wdq, gq, wuq, cos, sin)


ATT_TQ = 512
ATT_TK = 256


def _flash_kernel(q_ref, k_ref, v_ref, o_ref):
    s_len = q_ref.shape[0]
    tq, tk = ATT_TQ, ATT_TK
    row = lax.broadcasted_iota(jnp.int32, (tq, tk), 0)
    col = lax.broadcasted_iota(jnp.int32, (tq, tk), 1)
    lane = lax.broadcasted_iota(jnp.int32, (tq, 2 * V_HEAD), 1)
    nt = (((1,), (1,)), ((), ()))
    for qi in range(s_len // tq):
        q0 = qi * tq
        qs = [q_ref[q0:q0 + tq, hh * HEAD_PAD:(hh + 1) * HEAD_PAD] for hh in range(2)]
        m = [jnp.full((tq, 1), -jnp.inf, F32) for _ in range(2)]
        l = [jnp.zeros((tq, 1), F32) for _ in range(2)]
        acc = [jnp.zeros((tq, 2 * V_HEAD), F32) for _ in range(2)]
        for kj in range((q0 + tq) // tk):
            k0 = kj * tk
            v = v_ref[k0:k0 + tk, :]
            for hh in range(2):
                k = k_ref[k0:k0 + tk, hh * HEAD_PAD:(hh + 1) * HEAD_PAD]
                s = lax.dot_general(qs[hh], k, nt, preferred_element_type=F32)
                if k0 + tk - 1 > q0:
                    s = jnp.where(col + k0 <= row + q0, s, -jnp.inf)
                m_new = jnp.maximum(m[hh], jnp.max(s, axis=-1, keepdims=True))
                alpha = jnp.exp2(m[hh] - m_new)
                p = jnp.exp2(s - m_new)
                l[hh] = alpha * l[hh] + jnp.sum(p, axis=-1, keepdims=True)
                acc[hh] = alpha * acc[hh] + jnp.dot(p.astype(BF16), v, preferred_element_type=F32)
                m[hh] = m_new
        o = jnp.where(lane < V_HEAD, acc[0] / l[0], acc[1] / l[1])
        o_ref[q0:q0 + tq, :] = o.astype(BF16)


def _flash(q, kcat, v, b, s):
    return pl.pallas_call(
        _flash_kernel,
        grid=(b, N_HEADS // 2),
        in_specs=[pl.BlockSpec((s, 2 * HEAD_PAD), lambda i, h: (i, h)),
                  pl.BlockSpec((s, 2 * HEAD_PAD), lambda i, h: (i, h)),
                  pl.BlockSpec((s, 2 * V_HEAD), lambda i, h: (i, h))],
        out_specs=pl.BlockSpec((s, 2 * V_HEAD), lambda i, h: (i, h)),
        out_shape=jax.ShapeDtypeStruct((b * s, N_HEADS * V_HEAD), BF16),
        compiler_params=_cparams(("parallel", "parallel")),
        name="flash_attn",
    )(q, kcat, v)


def _qlat_kernel(q_ref, wukt_ref, o_ref):
    for hd in range(N_HEADS):
        qh = q_ref[:, hd * HEAD_PAD:(hd + 1) * HEAD_PAD]
        o_ref[:, hd * KV_LORA:(hd + 1) * KV_LORA] = jnp.dot(
            qh, wukt_ref[hd], preferred_element_type=F32).astype(BF16)


def _qlat(q, wukt):
    n = q.shape[0]
    return pl.pallas_call(
        _qlat_kernel,
        in_specs=[_full(q.shape), _full(wukt.shape)],
        out_specs=_full((n, N_HEADS * KV_LORA)),
        out_shape=jax.ShapeDtypeStruct((n, N_HEADS * KV_LORA), BF16),
        grid=(1,),
        compiler_params=_cparams(("arbitrary",)),
        name="q_absorb",
    )(q, wukt)


PAGES_PER_CHUNK = 32
CHUNK_SLOTS = 4
CHUNK_AHEAD = 2


def _paged_kernel(pt_ref, ql_ref, qp_ref, cn_ref, kn_ref, lat_hbm, kpt_hbm, o_ref, lbuf, kbuf, sems, *, nb):
    b = pl.program_id(0)
    n_pages = pt_ref.shape[0] // nb
    cp = PAGES_PER_CHUNK
    n_chunks = n_pages // cp
    assert n_chunks == CHUNK_SLOTS
    total = nb * n_chunks

    def start_chunk(g, slot):
        gg = jnp.where(g < total, g, 0)
        for j in range(cp):
            page = pt_ref[gg * cp + j]
            src = pl.ds(pl.multiple_of(page * PAGE_SIZE, PAGE_SIZE), PAGE_SIZE)
            pltpu.make_async_copy(lat_hbm.at[src], lbuf.at[slot, pl.ds(j * PAGE_SIZE, PAGE_SIZE)],
                                  sems.at[0, slot]).start()
            pltpu.make_async_copy(kpt_hbm.at[page], kbuf.at[slot, j], sems.at[1, slot]).start()

    def wait_chunk(g, slot):
        del g
        pltpu.make_async_copy(lat_hbm.at[pl.ds(0, cp * PAGE_SIZE)], lbuf.at[slot], sems.at[0, slot]).wait()
        pltpu.make_async_copy(kpt_hbm.at[pl.ds(0, cp)], kbuf.at[slot], sems.at[1, slot]).wait()

    @pl.when(b == 0)
    def _():
        for a in range(CHUNK_AHEAD):
            start_chunk(a, a)

    ql = ql_ref[...]
    qp = qp_ref[...]
    nt = (((1,), (1,)), ((), ()))
    m = jnp.full((N_HEADS, 1), -jnp.inf, F32)
    l = jnp.zeros((N_HEADS, 1), F32)
    acc = jnp.zeros((N_HEADS, KV_LORA), F32)
    rows = cp * PAGE_SIZE
    for c0 in range(0, n_chunks, CHUNK_AHEAD):
        g = b * n_chunks + c0
        group = range(c0, c0 + CHUNK_AHEAD)
        for a in range(CHUNK_AHEAD):
            start_chunk(g + CHUNK_AHEAD + a, (c0 + CHUNK_AHEAD + a) % CHUNK_SLOTS)
        for a in range(CHUNK_AHEAD):
            wait_chunk(g + a, c0 + a)
        kcs = [lbuf[c].astype(BF16) for c in group]
        kps = [jnp.concatenate([kbuf[c, jj] for jj in range(cp)], axis=1).astype(BF16)
               for c in group]
        s = jnp.concatenate([lax.dot_general(ql, kc, nt, preferred_element_type=F32)
                             + jnp.dot(qp, kp, preferred_element_type=F32)
                             for kc, kp in zip(kcs, kps)], axis=1)
        m_new = jnp.maximum(m, jnp.max(s, axis=-1, keepdims=True))
        alpha = jnp.exp2(m - m_new)
        p = jnp.exp2(s - m_new)
        l = alpha * l + jnp.sum(p, axis=-1, keepdims=True)
        acc = alpha * acc
        for a, kc in enumerate(kcs):
            acc = acc + jnp.dot(p[:, a * rows:(a + 1) * rows].astype(BF16), kc, preferred_element_type=F32)
        m = m_new
    cn = cn_ref[...].astype(BF16).astype(F32)
    kn = kn_ref[...].astype(BF16).astype(F32)
    s = (jnp.sum(ql.astype(F32) * cn, axis=-1, keepdims=True)
         + jnp.sum(qp.astype(F32) * kn, axis=-1, keepdims=True))
    m_new = jnp.maximum(m, s)
    alpha = jnp.exp2(m - m_new)
    p = jnp.exp2(s - m_new)
    l = alpha * l + p
    acc = alpha * acc + p.astype(BF16).astype(F32) * cn
    o_ref[...] = acc / l

    @pl.when(b == nb - 1)
    def _():
        for a in range(CHUNK_AHEAD):
            wait_chunk(total + a, a)


def _paged_attn(pt_flat, ql, qp, c_new, k_new, cache_latent, cache_kpt):
    n = ql.shape[0]
    rows = PAGES_PER_CHUNK * PAGE_SIZE
    per_seq = lambda w: pl.BlockSpec((None,) + w, lambda i, pt: (i, 0, 0))
    return pl.pallas_call(
        functools.partial(_paged_kernel, nb=n),
        grid_spec=pltpu.PrefetchScalarGridSpec(
            num_scalar_prefetch=1,
            grid=(n,),
            in_specs=[per_seq((N_HEADS, KV_LORA)), per_seq((N_HEADS, QK_ROPE)),
                      per_seq((1, KV_LORA)), per_seq((1, QK_ROPE)),
                      pl.BlockSpec(memory_space=pl.ANY), pl.BlockSpec(memory_space=pl.ANY)],
            out_specs=per_seq((N_HEADS, KV_LORA)),
            scratch_shapes=[pltpu.VMEM((CHUNK_SLOTS, rows, KV_LORA), F32),
                            pltpu.VMEM((CHUNK_SLOTS, PAGES_PER_CHUNK, QK_ROPE, PAGE_SIZE), F32),
                            pltpu.SemaphoreType.DMA((2, CHUNK_SLOTS))],
        ),
        out_shape=jax.ShapeDtypeStruct((n, N_HEADS, KV_LORA), F32),
        compiler_params=_cparams(("arbitrary",)),
        name="paged_attn",
    )(pt_flat, ql, qp, c_new, k_new, cache_latent, cache_kpt)


def _attn_out_kernel(ol_ref, wuv_ref, wo_ref, h_ref, o_ref):
    acc = h_ref[...]
    for hd in range(N_HEADS):
        oh = ol_ref[:, hd * KV_LORA:(hd + 1) * KV_LORA].astype(BF16)
        th = jnp.dot(oh, wuv_ref[hd], preferred_element_type=F32).astype(BF16)
        acc = acc + jnp.dot(th, wo_ref[hd * V_HEAD:(hd + 1) * V_HEAD, :], preferred_element_type=F32)
    o_ref[...] = acc


def _attn_out(ol, wuv_h, wo, h):
    n = h.shape[0]
    return pl.pallas_call(
        _attn_out_kernel,
        grid=(1,),
        in_specs=[_full(ol.shape), _full(wuv_h.shape), _full(wo.shape), _full(h.shape)],
        out_specs=_full(h.shape),
        out_shape=jax.ShapeDtypeStruct((n, D_MODEL), F32),
        compiler_params=_cparams(("arbitrary",)),
        name="attn_out",
    )(ol, wuv_h, wo, h)


def _rope_tables(pos):
    half = QK_ROPE // 2
    inv = ROPE_THETA ** (-jnp.arange(half, dtype=F32) / half)
    ang = pos.astype(F32)[:, None] * inv[None, :]
    cos, sin = jnp.cos(ang), jnp.sin(ang)
    n = pos.shape[0]
    cos_t = jnp.concatenate([jnp.ones((n, QK_NOPE), F32), cos, cos,
                             jnp.ones((n, HEAD_PAD - ROPE_HI), F32)], axis=1)
    sin_t = jnp.concatenate([jnp.zeros((n, QK_NOPE), F32), sin, sin,
                             jnp.zeros((n, HEAD_PAD - ROPE_HI), F32)], axis=1)
    return cos_t, sin_t


def _moe(h, g_ffn, wr, br, w1, w3, w2, layer, tm, proj=None):
    t = h.shape[0]
    h, u, code, gate, cnt = _router(h, g_ffn, wr, br, tm, proj)
    counts = cnt[:, 0].astype(jnp.int32)
    padded = (counts + MOE_BM - 1) // MOE_BM * MOE_BM
    pends = jnp.cumsum(padded)
    pstarts = pends - padded
    n_steps = (t * TOP_K + N_EXPERTS * (MOE_BM - 1) + MOE_BM - 1) // MOE_BM
    n_rows = n_steps * MOE_BM
    eid = code >> RANK_BITS
    eiota = jnp.arange(N_EXPERTS, dtype=jnp.int32)
    start_of = jnp.sum(jnp.where(eid[..., None] == eiota, pstarts, 0), axis=-1)
    dest = (start_of + (code & ((1 << RANK_BITS) - 1))).reshape(-1)
    block_start = jnp.arange(n_steps, dtype=jnp.int32) * MOE_BM
    blk_e = jnp.minimum(jnp.sum((pends[None, :] <= block_start[:, None]).astype(jnp.int32), axis=1),
                        N_EXPERTS - 1)
    order = jnp.concatenate([jnp.argsort(dest).astype(jnp.int32),
                             jnp.zeros((n_rows - TOP_K * t,), jnp.int32)])
    starts = jnp.cumsum(counts) - counts
    per_expert = jnp.stack([pstarts - starts, pstarts + counts, starts + counts])
    meta = jnp.sum(jnp.where(blk_e[None, :, None] == eiota, per_expert[:, None, :], 0), axis=-1)
    n_out = n_rows + 2 * MOE_BM
    assert t & (t - 1) == 0
    yp = _experts(blk_e, meta.reshape(-1).astype(jnp.int32), order, u, w1, w3, w2, layer, n_steps, n_out)
    return h, yp, gate.T


def kernel(x_prompt, x_sample, cache_conv, cache_latent, cache_kpe, page_table, p_prompt, p_sample, g_mix, g_ffn, g_ple, g_final, conv_w_in, conv_b_in, conv_w_dw, conv_b_dw, conv_ln_g, conv_ln_b, conv_w_out, conv_b_out, g_kv_in, w_dkv, g_kv_lat, w_kr, w_uk, w_uv, w_dq, g_q_lat, w_uq, w_o, w_group, b_group, w_inner, b_inner, w_exp_gate, w_exp_up, w_exp_down, w_ple_gate, b_ple_gate, w_ple_proj):
    bp, sp, d = x_prompt.shape
    nd = x_sample.shape[0]
    tp = bp * sp
    row = lambda a: a.reshape(1, -1)

    w_in_b = conv_w_in[0].astype(BF16)
    w_out_b = conv_w_out[0].astype(BF16)
    wdw_pad = jnp.concatenate([conv_w_dw[0], jnp.zeros((HALO - CONV_W, d), F32)], axis=0)
    wr, br = [], []
    for i in range(2):
        tail = ROUTER_ROWS - 8 - N_EXPERTS
        wr.append(jnp.concatenate([w_group[i].T, jnp.zeros((8 - N_GROUPS, d), F32),
                                   w_inner[i].reshape(d, N_EXPERTS).T, jnp.zeros((tail, d), F32)], axis=0))
        br.append(jnp.concatenate([b_group[i], jnp.zeros((8 - N_GROUPS,), F32),
                                   b_inner[i].reshape(N_EXPERTS), jnp.zeros((tail,), F32)]).reshape(ROUTER_ROWS, 1))
    w_dkv_b = w_dkv.astype(BF16)
    w_kr_pad = jnp.zeros((d, HEAD_PAD), F32).at[:, ROPE_LO:ROPE_HI].set(w_kr).astype(BF16)
    w_uk_pad = jnp.pad(w_uk, ((0, 0), (0, 0), (0, HEAD_PAD - QK_NOPE))).reshape(KV_LORA, -1).astype(BF16)
    w_uv_b = w_uv.reshape(KV_LORA, -1).astype(BF16)
    w_dq_b = w_dq[0].astype(BF16)
    w_uq_pad = jnp.pad(w_uq[0].reshape(Q_LORA, N_HEADS, QK_NOPE + QK_ROPE),
                       ((0, 0), (0, 0), (0, HEAD_PAD - ROPE_HI))).reshape(Q_LORA, -1).astype(BF16)
    w_o_b = w_o[0].astype(BF16)
    w_ukt_pad = jnp.pad(jnp.transpose(w_uk, (1, 2, 0)),
                        ((0, 0), (0, HEAD_PAD - QK_NOPE), (0, 0))).astype(BF16)
    w_uv_h = jnp.transpose(w_uv, (1, 0, 2)).astype(BF16)
    wg_b = [w_ple_gate[i].astype(BF16) for i in range(2)]
    wp_b = [w_ple_proj[i].astype(BF16) for i in range(2)]

    def moe_ple(h, layer, p, tm, final, proj=None):
        h, yp, gate_t = _moe(h, row(g_ffn[layer]), wr[layer], br[layer],
                             w_exp_gate, w_exp_up, w_exp_down, layer, tm, proj)
        return _combine(yp, h, gate_t, p, row(g_ple[layer]), wg_b[layer], row(b_ple_gate[layer]),
                        wp_b[layer], row(g_final), tm, final)

    def kvq(h, cos_t, sin_t, tm, n_pos_blocks):
        return _kvq(h, row(g_kv_in), row(g_mix[1]), w_dkv_b, row(g_kv_lat), w_kr_pad, w_uk_pad, w_uv_b,
                    w_dq_b, row(g_q_lat[0]), w_uq_pad, cos_t, sin_t, tm, n_pos_blocks)

    xp = x_prompt.reshape(tp, d)
    z = _conv_in(xp, row(g_mix[0]), w_in_b, row(conv_b_in[0]), 512)
    z3 = z.reshape(bp, sp, d)
    conv_prompt = z3[:, sp - (CONV_W - 1):, :][None]
    ctx0 = jnp.zeros((bp, HALO, d), F32)
    h = _conv_mix(z3, ctx0, x_prompt, wdw_pad, row(conv_b_dw[0]), row(conv_ln_g[0]), row(conv_ln_b[0]),
                  w_out_b, row(conv_b_out[0])).reshape(tp, d)
    h = moe_ple(h, 0, p_prompt[0].reshape(tp, PLE_DIM), 512, False)
    cos_p, sin_p = _rope_tables(jnp.arange(sp, dtype=jnp.int32))
    lat_p, kpe_p, kcat, v, q = kvq(h, cos_p, sin_p, 512, sp // 512)
    o = _flash(q, kcat, v, bp, sp)
    y_prompt = moe_ple(h, 1, p_prompt[1].reshape(tp, PLE_DIM), 512, True, proj=(o, w_o_b)).reshape(bp, sp, d)
    latent_prompt = lat_p.reshape(bp, sp, KV_LORA)
    kpe_prompt = kpe_p[:, ROPE_LO:ROPE_HI].reshape(bp, sp, QK_ROPE)

    xs = x_sample.reshape(nd, d)
    zs = _conv_in(xs, row(g_mix[0]), w_in_b, row(conv_b_in[0]), nd)
    ctx_t = jnp.transpose(cache_conv[0], (1, 0, 2))
    conv_sample = jnp.transpose(jnp.concatenate([ctx_t[1:], zs[None]], axis=0), (1, 0, 2))[None]
    hs = _conv_step(ctx_t, conv_w_dw[0][:, None, :], zs, xs,
                    row(conv_b_dw[0]), row(conv_ln_g[0]), row(conv_ln_b[0]), w_out_b, row(conv_b_out[0]))
    hs = moe_ple(hs, 0, p_sample[0].reshape(nd, PLE_DIM), nd, False)
    n_pages = page_table.shape[1]
    pos_s = jnp.full((nd,), n_pages * PAGE_SIZE, jnp.int32)
    cos_s, sin_s = _rope_tables(pos_s)
    lat_s, kpe_s, _, _, qs = kvq(hs, cos_s, sin_s, nd, 1)
    ql = _qlat(qs, w_ukt_pad).reshape(nd, N_HEADS, KV_LORA)
    qp = qs.reshape(nd, N_HEADS, HEAD_PAD)[:, :, ROPE_LO:ROPE_HI]
    kpe_new = kpe_s[:, ROPE_LO:ROPE_HI]
    o_lat = _paged_attn(page_table.reshape(-1), ql, qp, lat_s.reshape(nd, 1, KV_LORA),
                        kpe_new.reshape(nd, 1, QK_ROPE), cache_latent.reshape(-1, KV_LORA),
                        jnp.transpose(cache_kpe, (0, 2, 1)))
    hs = _attn_out(o_lat.reshape(nd, N_HEADS * KV_LORA), w_uv_h, w_o_b, hs)
    y_sample = moe_ple(hs, 1, p_sample[1].reshape(nd, PLE_DIM), nd, True).reshape(nd, 1, d)
    latent_sample = lat_s.reshape(nd, 1, KV_LORA)
    kpe_sample = kpe_new.reshape(nd, 1, QK_ROPE)

    return (y_prompt, y_sample, conv_prompt, conv_sample, latent_prompt, kpe_prompt, latent_sample, kpe_sample)
```
